```python
import math
import jax, jax.numpy as jnp
from jax import lax
import numpy as np

D_MODEL = 1024
BATCH = 2
SEQ = 16384
DEPTH = 4

GRID_W = 64
CTX_LEN = 256
N_HEADS = D_MODEL // 128
HEAD_DIM = 64
V_DIM = 2 * HEAD_DIM
QK_WIDTH = N_HEADS * 2 * HEAD_DIM
V_WIDTH = N_HEADS * V_DIM
F_GROUPS = 4
F_WIDTH = D_MODEL // 2
F_GROUP_DIM = F_WIDTH // F_GROUPS
IN_WIDTH = 2 * QK_WIDTH + V_WIDTH + F_WIDTH
Q_BLOCK = 128
ROPE_THETA = 10000.0
ROPE_AXIS_DIM = HEAD_DIM // 2
ROPE_HALF = ROPE_AXIS_DIM // 2
ATTN_SCALE = HEAD_DIM ** -0.5
N_EXPERTS = 16
N_GROUPS = 4
EXPERTS_PER_GROUP = N_EXPERTS // N_GROUPS
TOP_K = 2
D_EXPERT = D_MODEL // 2
EPS = 1e-6

kernel_name = "hybrid_diffattn_fourier_groupmoe_dit"


def rmsnorm(x, g):
    xf = x.astype(jnp.float32)
    y = xf * lax.rsqrt(jnp.mean(xf * xf, axis=-1, keepdims=True) + EPS)
    return (y * g.astype(jnp.float32)).astype(x.dtype)


def ada_chunks(cond, w, b):
    m = jax.nn.silu(cond) @ w + b
    return jnp.split(m, 6, axis=-1)


def modulate(h, shift, scale):
    return h * (1 + scale) + shift


def axial_rope_tables(n_tokens):
    rows = n_tokens // GRID_W
    row = jnp.broadcast_to(jnp.arange(rows)[:, None], (rows, GRID_W)).reshape(-1)
    col = jnp.broadcast_to(jnp.arange(GRID_W)[None, :], (rows, GRID_W)).reshape(-1)
    inv = ROPE_THETA ** (-jnp.arange(ROPE_HALF, dtype=jnp.float32) / ROPE_HALF)
    ang = jnp.concatenate([row[:, None] * inv, col[:, None] * inv], axis=-1)
    return jnp.cos(ang), jnp.sin(ang)


def apply_axial_rope(x, cos, sin):
    S = x.shape[1]
    xs = x.astype(jnp.float32).reshape(*x.shape[:-1], 2, 2, ROPE_HALF)
    x1, x2 = xs[..., 0, :], xs[..., 1, :]
    c = cos.reshape(S, 2, ROPE_HALF)[None, :, None, None]
    s = sin.reshape(S, 2, ROPE_HALF)[None, :, None, None]
    out = jnp.stack([x1 * c - x2 * s, x1 * s + x2 * c], axis=-2)
    return out.reshape(x.shape).astype(x.dtype)


def diff_attend(q, k, v, lam):
    s = jnp.einsum('bqhmd,bkhmd->bhmqk', q, k, preferred_element_type=jnp.float32) * ATTN_SCALE
    p = jax.nn.softmax(s, axis=-1)
    a = p[:, :, 0] - lam * p[:, :, 1]
    o = jnp.einsum('bhqk,bkhe->bqhe', a.astype(v.dtype), v, preferred_element_type=jnp.float32)
    return o.astype(v.dtype)


def fourier_mix(f):
    B, L, _ = f.shape
    fg = f.astype(jnp.float32).reshape(B, L, F_GROUPS, F_GROUP_DIM)
    z = jnp.fft.fftn(fg, axes=(1, 3), norm="ortho").real
    return z.reshape(B, L, F_WIDTH).astype(f.dtype)


def split_proj(u):
    lead = u.shape[:-1]
    q = u[..., :QK_WIDTH].reshape(*lead, N_HEADS, 2, HEAD_DIM)
    k = u[..., QK_WIDTH:2 * QK_WIDTH].reshape(*lead, N_HEADS, 2, HEAD_DIM)
    v = u[..., 2 * QK_WIDTH:2 * QK_WIDTH + V_WIDTH].reshape(*lead, N_HEADS, V_DIM)
    f = u[..., 2 * QK_WIDTH + V_WIDTH:]
    return q, k, v, f


def merge_branches(h, o, f, lam_init, g_subln, w_attn_out, w_four_out, w_gate, b_gate, w_out):
    o = rmsnorm(o, g_subln) * (1.0 - lam_init)
    y_attn = o.reshape(*o.shape[:-2], V_WIDTH) @ w_attn_out
    y_four = fourier_mix(f) @ w_four_out
    gates = jax.nn.sigmoid((h @ w_gate + b_gate).astype(jnp.float32)).astype(h.dtype)
    g_attn, g_four = gates[..., :D_MODEL], gates[..., D_MODEL:]
    return (g_attn * y_attn + g_four * y_four) @ w_out


def token_mixer(h_lat, h_ctx, w_in, lam, lam_init, g_subln, w_attn_out, w_four_out,
                w_gate, b_gate, w_out, cos, sin, with_ctx_out):
    B, S, _ = h_lat.shape
    q_l, k_l, v_l, f_l = split_proj(h_lat @ w_in)
    q_l = apply_axial_rope(q_l, cos, sin)
    k_l = apply_axial_rope(k_l, cos, sin)
    if with_ctx_out:
        q_c, k_c, v_c, f_c = split_proj(h_ctx @ w_in)
    else:
        u_kv = h_ctx @ w_in[:, QK_WIDTH:2 * QK_WIDTH + V_WIDTH]
        k_c = u_kv[..., :QK_WIDTH].reshape(*u_kv.shape[:-1], N_HEADS, 2, HEAD_DIM)
        v_c = u_kv[..., QK_WIDTH:].reshape(*u_kv.shape[:-1], N_HEADS, V_DIM)
    k_all = jnp.concatenate([k_c, k_l], axis=1)
    v_all = jnp.concatenate([v_c, v_l], axis=1)
    n_blk = S // Q_BLOCK
    qb = q_l.reshape(B, n_blk, Q_BLOCK, N_HEADS, 2, HEAD_DIM).swapaxes(0, 1)
    o_l = lax.map(lambda qq: diff_attend(qq, k_all, v_all, lam), qb)
    o_l = o_l.swapaxes(0, 1).reshape(B, S, N_HEADS, V_DIM)
    out_l = merge_branches(h_lat, o_l, f_l, lam_init, g_subln, w_attn_out, w_four_out,
                           w_gate, b_gate, w_out)
    if with_ctx_out:
        o_c = diff_attend(q_c, k_c, v_c, lam)
        out_c = merge_branches(h_ctx, o_c, f_c, lam_init, g_subln, w_attn_out, w_four_out,
                               w_gate, b_gate, w_out)
        return out_l, out_c
    return out_l, None


def moe(h, w_router, b_router, w_e_gate, w_e_up, w_e_down):
    aff = jax.nn.sigmoid(jnp.matmul(h, w_router, preferred_element_type=jnp.float32))
    sel = aff + b_router.astype(jnp.float32)
    group_score = lax.top_k(sel.reshape(-1, N_GROUPS, EXPERTS_PER_GROUP), TOP_K)[0].sum(-1)
    g_idx = jnp.argmax(group_score, axis=-1)
    in_group = (jnp.arange(N_EXPERTS) // EXPERTS_PER_GROUP)[None, :] == g_idx[:, None]
    _, top_idx = lax.top_k(jnp.where(in_group, sel, -jnp.inf), TOP_K)
    w_top = jnp.take_along_axis(aff, top_idx, axis=-1)
    w_top = w_top / jnp.sum(w_top, axis=-1, keepdims=True)
    gates = jnp.einsum('nk,nke->ne', w_top,
                       jax.nn.one_hot(top_idx, N_EXPERTS, dtype=jnp.float32)).astype(h.dtype)
    y = jnp.zeros_like(h)
    for e in range(N_EXPERTS):
        he = jax.nn.silu(h @ w_e_gate[e]) * (h @ w_e_up[e])
        y = y + gates[:, e:e + 1] * (he @ w_e_down[e])
    return y


def setup_inputs(seed: int = 0) -> dict:
    key = jax.random.key(seed)
    ks = jax.random.split(key, 24)
    f32 = jnp.float32
    L, D = DEPTH, D_MODEL
    nrm = lambda k, shape, s: jax.random.normal(k, shape, f32) * s
    return {
        "x": nrm(ks[0], (BATCH, SEQ, D), 1.0),
        "c": nrm(ks[1], (BATCH, D), 1.0),
        "ctx": nrm(ks[2], (BATCH, CTX_LEN, D), 1.0),
        "c_ctx": nrm(ks[3], (D,), 1.0),
        "w_ada": nrm(ks[4], (L, D, 6 * D), 0.5 * D ** -0.5),
        "b_ada": nrm(ks[5], (L, 6 * D), 0.02),
        "g_norm1": 1.0 + nrm(ks[6], (L, D), 0.02),
        "w_in": nrm(ks[7], (L, D, IN_WIDTH), D ** -0.5),
        "lam_qk": nrm(ks[8], (L, 4, HEAD_DIM), 0.1),
        "g_subln": 1.0 + nrm(ks[9], (L, V_DIM), 0.02),
        "w_attn_out": nrm(ks[10], (L, V_WIDTH, D), V_WIDTH ** -0.5),
        "w_four_out": nrm(ks[11], (L, F_WIDTH, D), F_WIDTH ** -0.5),
        "w_gate": nrm(ks[12], (L, D, 2 * D), D ** -0.5),
        "b_gate": nrm(ks[13], (L, 2 * D), 0.02),
        "w_out": nrm(ks[14], (L, D, D), D ** -0.5),
        "g_norm2": 1.0 + nrm(ks[15], (L, D), 0.02),
        "w_router": nrm(ks[16], (D, N_EXPERTS), D ** -0.5),
        "b_router": nrm(ks[17], (N_EXPERTS,), 0.01),
        "w_e_gate": nrm(ks[18], (L, N_EXPERTS, D, D_EXPERT), D ** -0.5),
        "w_e_up": nrm(ks[19], (L, N_EXPERTS, D, D_EXPERT), D ** -0.5),
        "w_e_down": nrm(ks[20], (L, N_EXPERTS, D_EXPERT, D), D_EXPERT ** -0.5),
        "g_final": 1.0 + nrm(ks[21], (D,), 0.02),
    }


def reference(x, c, ctx, c_ctx, w_ada, b_ada, g_norm1, w_in, lam_qk, g_subln, w_attn_out,
              w_four_out, w_gate, b_gate, w_out, g_norm2, w_router, b_router,
              w_e_gate, w_e_up, w_e_down, g_final):
    B, S, D = x.shape
    cos, sin = axial_rope_tables(S)
    for i in range(DEPTH):
        last = i == DEPTH - 1
        lam_init = 0.8 - 0.6 * math.exp(-0.3 * i)
        lp = lam_qk[i].astype(jnp.float32)
        lam = jnp.exp(jnp.sum(lp[0] * lp[1])) - jnp.exp(jnp.sum(lp[2] * lp[3])) + lam_init
        sh1, sc1, ga1, sh2, sc2, ga2 = [m[:, None, :] for m in ada_chunks(c, w_ada[i], b_ada[i])]
        csh1, csc1, cga1, csh2, csc2, cga2 = ada_chunks(c_ctx, w_ada[i], b_ada[i])
        h_l = modulate(rmsnorm(x, g_norm1[i]), sh1, sc1)
        h_c = modulate(rmsnorm(ctx, g_norm1[i]), csh1, csc1)
        out_l, out_c = token_mixer(h_l, h_c, w_in[i], lam, lam_init, g_subln[i], w_attn_out[i],
                                   w_four_out[i], w_gate[i], b_gate[i], w_out[i], cos, sin,
                                   not last)
        x = x + ga1 * out_l
        h2_l = modulate(rmsnorm(x, g_norm2[i]), sh2, sc2)
        if last:
            y = moe(h2_l.reshape(-1, D), w_router, b_router, w_e_gate[i], w_e_up[i], w_e_down[i])
            x = x + ga2 * y.reshape(B, S, D)
        else:
            ctx = ctx + cga1 * out_c
            h2_c = modulate(rmsnorm(ctx, g_norm2[i]), csh2, csc2)
            toks = jnp.concatenate([h2_c, h2_l], axis=1)
            y = moe(toks.reshape(-1, D), w_router, b_router, w_e_gate[i], w_e_up[i],
                    w_e_down[i]).reshape(B, CTX_LEN + S, D)
            ctx = ctx + cga2 * y[:, :CTX_LEN]
            x = x + ga2 * y[:, CTX_LEN:]
    return rmsnorm(x, g_final)
```

```python
import functools
import math

import numpy as np
import jax
import jax.numpy as jnp
from jax import lax
from jax.experimental import pallas as pl
from jax.experimental.pallas import tpu as pltpu

F32 = jnp.float32
BF16 = jnp.bfloat16
HIGHEST = lax.Precision.HIGHEST

D_MODEL = 1024
GRID_W = 64
CTX_LEN = 256
N_HEADS = 8
HEAD_DIM = 64
V_DIM = 128
QK_WIDTH = 1024
V_WIDTH = 1024
F_GROUPS = 4
F_WIDTH = 512
F_GROUP_DIM = 128
IN_WIDTH = 3584
ROPE_THETA = 10000.0
ROPE_HALF = 16
N_EXPERTS = 16
N_GROUPS = 4
EXPERTS_PER_GROUP = 4
D_EXPERT = 512
EPS = 1e-6
N_MOD = 6
MOD_ROWS = 8

LANES = 128
TM = 256
TQ = 256
TK = 1024
TMOE = 640
ADA_TN = 512
DFT2 = 128
Q_SCALE = (HEAD_DIM ** -0.5) * math.log2(math.e)
VMEM_LIMIT = 56 * 1024 * 1024


def _cparams(n_axes):
    return pltpu.CompilerParams(dimension_semantics=("arbitrary",) * n_axes,
                                vmem_limit_bytes=VMEM_LIMIT)


def _rms_mod(x, g, shift, scale):
    y = x * lax.rsqrt(jnp.mean(x * x, axis=-1, keepdims=True) + EPS) * g
    return y * (1.0 + scale) + shift


def _ada_kernel(c_ref, w_ref, b_ref, o_ref):
    c = c_ref[...]
    s = c * jax.nn.sigmoid(c)
    o_ref[0] = jnp.dot(s, w_ref[0], preferred_element_type=F32, precision=HIGHEST) + b_ref[0]


def _ada_call(cond, w_ada, b_ada):
    depth = w_ada.shape[0]
    n_out = w_ada.shape[2]
    return pl.pallas_call(
        _ada_kernel,
        grid=(depth, n_out // ADA_TN),
        in_specs=[
            pl.BlockSpec((MOD_ROWS, D_MODEL), lambda l, j: (0, 0)),
            pl.BlockSpec((1, D_MODEL, ADA_TN), lambda l, j: (l, 0, j)),
            pl.BlockSpec((1, 1, ADA_TN), lambda l, j: (l, 0, j)),
        ],
        out_specs=pl.BlockSpec((1, MOD_ROWS, ADA_TN), lambda l, j: (l, 0, j)),
        out_shape=jax.ShapeDtypeStruct((depth, MOD_ROWS, n_out), F32),
        compiler_params=_cparams(2),
        name="ada",
    )(cond, w_ada, b_ada.reshape(depth, 1, n_out))


def _proj_kernel(*refs, has_y):
    if has_y:
        (x_ref, y_ref, pmod_ref, mod_ref, g1_ref, win_ref, cos_ref, sin_ref,
         xo_ref, q_ref, k_ref, v_ref, f_ref) = refs
        x = x_ref[0] + pmod_ref[0][5:6] * y_ref[0]
        xo_ref[0] = x
    else:
        (x_ref, mod_ref, g1_ref, win_ref, cos_ref, sin_ref,
         q_ref, k_ref, v_ref, f_ref) = refs
        x = x_ref[0]
    mod = mod_ref[0]
    hb = _rms_mod(x, g1_ref[...], mod[0:1], mod[1:2]).astype(BF16)
    cos = cos_ref[...]
    sin = sin_ref[...]
    lane = lax.broadcasted_iota(jnp.int32, (TM, LANES), 1)
    first = (lane % (2 * ROPE_HALF)) < ROPE_HALF

    def rope(xc):
        up = pltpu.roll(xc, LANES - ROPE_HALF, 1)
        dn = pltpu.roll(xc, ROPE_HALF, 1)
        return xc * cos + jnp.where(first, up, dn) * sin

    w2 = 2 * LANES
    for c in range(QK_WIDTH // w2):
        uq = jnp.dot(hb, win_ref[:, c * w2:(c + 1) * w2], preferred_element_type=F32)
        uk = jnp.dot(hb, win_ref[:, QK_WIDTH + c * w2:QK_WIDTH + (c + 1) * w2],
                     preferred_element_type=F32)
        for j in range(2):
            lo = c * w2 + j * LANES
            q_ref[0, :, lo:lo + LANES] = (rope(uq[:, j * LANES:(j + 1) * LANES]) * Q_SCALE).astype(BF16)
            k_ref[0, :, lo:lo + LANES] = rope(uk[:, j * LANES:(j + 1) * LANES]).astype(BF16)
    for c in range(V_WIDTH // w2):
        lo = 2 * QK_WIDTH + c * w2
        v_ref[0, :, c * w2:(c + 1) * w2] = jnp.dot(
            hb, win_ref[:, lo:lo + w2], preferred_element_type=F32).astype(BF16)
    for c in range(F_WIDTH // w2):
        lo = 2 * QK_WIDTH + V_WIDTH + c * w2
        f_ref[0, :, c * w2:(c + 1) * w2] = jnp.dot(
            hb, win_ref[:, lo:lo + w2], preferred_element_type=F32)


def _proj_call(x, y, pmod, mod, g1, w_in, cos_t, sin_t, n_batch):
    bsz, n_tok, _ = x.shape
    nt = n_tok // TM
    has_y = y is not None

    def mod_map(b, t):
        return (jnp.where(t == nt - 1, n_batch, b), 0, 0)

    tok = lambda w: pl.BlockSpec((1, TM, w), lambda b, t: (b, t, 0))
    modspec = pl.BlockSpec((1, N_MOD, D_MODEL), mod_map)
    in_specs = [tok(D_MODEL)]
    args = [x]
    if has_y:
        in_specs += [tok(D_MODEL), modspec]
        args += [y, pmod]
    in_specs += [
        modspec,
        pl.BlockSpec((1, D_MODEL), lambda b, t: (0, 0)),
        pl.BlockSpec((D_MODEL, IN_WIDTH), lambda b, t: (0, 0)),
        pl.BlockSpec((TM, LANES), lambda b, t: (t, 0)),
        pl.BlockSpec((TM, LANES), lambda b, t: (t, 0)),
    ]
    args += [mod, g1, w_in, cos_t, sin_t]
    out_specs = [tok(QK_WIDTH), tok(QK_WIDTH), tok(V_WIDTH), tok(F_WIDTH)]
    out_shape = [
        jax.ShapeDtypeStruct((bsz, n_tok, QK_WIDTH), BF16),
        jax.ShapeDtypeStruct((bsz, n_tok, QK_WIDTH), BF16),
        jax.ShapeDtypeStruct((bsz, n_tok, V_WIDTH), BF16),
        jax.ShapeDtypeStruct((bsz, n_tok, F_WIDTH), F32),
    ]
    if has_y:
        out_specs = [tok(D_MODEL)] + out_specs
        out_shape = [jax.ShapeDtypeStruct((bsz, n_tok, D_MODEL), F32)] + out_shape
    outs = pl.pallas_call(
        functools.partial(_proj_kernel, has_y=has_y),
        grid=(bsz, nt),
        in_specs=in_specs,
        out_specs=out_specs,
        out_shape=out_shape,
        compiler_params=_cparams(2),
        name="proj",
    )(*args)
    if has_y:
        return outs
    return [x] + list(outs)


def _attn_kernel(lam_ref, gs_ref, q_ref, k_ref, v_ref, o_ref, m_ref, l_ref, acc_ref, *,
                 n_lat, lam_init):
    qt = pl.program_id(2)
    is_ctx = qt == pl.num_programs(2) - 1
    q = q_ref[0]
    lane = lax.broadcasted_iota(jnp.int32, (TQ, LANES), 1)
    zero = jnp.zeros_like(q)
    q2 = jnp.concatenate([jnp.where(lane < HEAD_DIM, q, zero),
                          jnp.where(lane >= HEAD_DIM, q, zero)], axis=0)
    m_ref[...] = jnp.full(m_ref.shape, -jnp.inf, F32)
    l_ref[...] = jnp.zeros(l_ref.shape, F32)
    acc_ref[...] = jnp.zeros(acc_ref.shape, F32)

    def block(kb, vb):
        s = lax.dot_general(q2, kb, (((1,), (1,)), ((), ())), preferred_element_type=F32)
        m_prev = m_ref[...]
        m_new = jnp.maximum(m_prev, jnp.max(s, axis=1, keepdims=True))
        alpha = jnp.exp2(m_prev - m_new)
        p = jnp.exp2(s - m_new)
        l_ref[...] = alpha * l_ref[...] + jnp.sum(p, axis=1, keepdims=True)
        acc_ref[...] = alpha * acc_ref[...] + jnp.dot(p.astype(BF16), vb,
                                                      preferred_element_type=F32)
        m_ref[...] = m_new

    block(k_ref[0, n_lat * TK:n_lat * TK + CTX_LEN, :], v_ref[0, n_lat * TK:n_lat * TK + CTX_LEN, :])

    def body(j, carry):
        off = pl.multiple_of(j * TK, TK)
        block(k_ref[0, pl.ds(off, TK), :], v_ref[0, pl.ds(off, TK), :])
        return carry

    lax.fori_loop(0, jnp.where(is_ctx, 0, n_lat), body, 0)

    lp = lam_ref[...]
    lam = (jnp.exp(jnp.sum(lp[0:1] * lp[1:2], axis=1, keepdims=True))
           - jnp.exp(jnp.sum(lp[2:3] * lp[3:4], axis=1, keepdims=True)) + lam_init)
    acc = acc_ref[...]
    l = l_ref[...]
    o = acc[:TQ] / l[:TQ] - lam * (acc[TQ:] / l[TQ:])
    o = o * lax.rsqrt(jnp.mean(o * o, axis=-1, keepdims=True) + EPS) * gs_ref[...]
    o_ref[0] = (o * (1.0 - lam_init)).astype(BF16)


def _attn_call(q, k, v, lam_qk, g_subln, lam_init):
    bsz, n_tok, _ = q.shape
    n_lat = (n_tok - CTX_LEN) // TK
    return pl.pallas_call(
        functools.partial(_attn_kernel, n_lat=n_lat, lam_init=lam_init),
        grid=(bsz, N_HEADS, n_tok // TQ),
        in_specs=[
            pl.BlockSpec((4, HEAD_DIM), lambda b, h, t: (0, 0)),
            pl.BlockSpec((1, V_DIM), lambda b, h, t: (0, 0)),
            pl.BlockSpec((1, TQ, LANES), lambda b, h, t: (b, t, h)),
            pl.BlockSpec((1, n_tok, LANES), lambda b, h, t: (b, 0, h)),
            pl.BlockSpec((1, n_tok, LANES), lambda b, h, t: (b, 0, h)),
        ],
        out_specs=pl.BlockSpec((1, TQ, V_DIM), lambda b, h, t: (b, t, h)),
        out_shape=jax.ShapeDtypeStruct((bsz, n_tok, V_WIDTH), BF16),
        scratch_shapes=[
            pltpu.VMEM((2 * TQ, 1), F32),
            pltpu.VMEM((2 * TQ, 1), F32),
            pltpu.VMEM((2 * TQ, V_DIM), F32),
        ],
        compiler_params=_cparams(3),
        name="diff_attn",
    )(lam_qk, g_subln.reshape(1, V_DIM), q, k, v)


def _dft1_kernel(u_ref, c1_ref, s1_ref, twc_ref, tws_ref, o_ref):
    u = u_ref[0]
    ar = jnp.dot(c1_ref[...], u, preferred_element_type=F32, precision=HIGHEST)
    ai = -jnp.dot(s1_ref[...], u, preferred_element_type=F32, precision=HIGHEST)
    twc = twc_ref[0]
    tws = tws_ref[0]
    for g in range(F_GROUPS):
        sl = slice(g * LANES, (g + 1) * LANES)
        o_ref[0, :, g * LANES:(g + 1) * LANES] = ar[:, sl] * twc + ai[:, sl] * tws
        o_ref[0, :, F_WIDTH + g * LANES:F_WIDTH + (g + 1) * LANES] = ai[:, sl] * twc - ar[:, sl] * tws


def _dft2_kernel(b_ref, c2_ref, s2_ref, cc_ref, sc_ref, o_ref, *, norm):
    bb = b_ref[0, 0]
    cb = jnp.dot(c2_ref[...], bb, preferred_element_type=F32, precision=HIGHEST)
    sb = jnp.dot(s2_ref[...], bb, preferred_element_type=F32, precision=HIGHEST)
    gr = cb[:, :F_WIDTH] + sb[:, F_WIDTH:]
    gi = cb[:, F_WIDTH:] - sb[:, :F_WIDTH]
    for g in range(F_GROUPS):
        sl = slice(g * LANES, (g + 1) * LANES)
        z = (jnp.dot(gr[:, sl], cc_ref[...], preferred_element_type=F32, precision=HIGHEST)
             + jnp.dot(gi[:, sl], sc_ref[...], preferred_element_type=F32, precision=HIGHEST))
        o_ref[0, :, sl] = z * norm


def _dft_ctx_kernel(f_ref, c_ref, s_ref, cc_ref, sc_ref, z_any, o_ref, *, norm):
    del z_any
    f = f_ref[0]
    gr = jnp.dot(c_ref[...], f, preferred_element_type=F32, precision=HIGHEST)
    gi = -jnp.dot(s_ref[...], f, preferred_element_type=F32, precision=HIGHEST)
    for g in range(F_GROUPS):
        sl = slice(g * LANES, (g + 1) * LANES)
        z = (jnp.dot(gr[:, sl], cc_ref[...], preferred_element_type=F32, precision=HIGHEST)
             + jnp.dot(gi[:, sl], sc_ref[...], preferred_element_type=F32, precision=HIGHEST))
        o_ref[0, :, sl] = z * norm


def _dft_mats(n):
    idx = np.arange(n, dtype=np.float64)
    ang = 2.0 * np.pi * np.outer(idx, idx) / n
    return jnp.asarray(np.cos(ang), F32), jnp.asarray(np.sin(ang), F32)


def _fourier_call(f, n_lat_tok):
    bsz, n_tok, _ = f.shape
    n1 = n_lat_tok // DFT2
    full = lambda *shape: pl.BlockSpec(shape, lambda *_: (0,) * len(shape))
    c1, s1 = _dft_mats(n1)
    c2, s2 = _dft_mats(DFT2)
    cc, sc = _dft_mats(F_GROUP_DIM)
    cctx, sctx = _dft_mats(CTX_LEN)
    tw_ang = 2.0 * np.pi * np.outer(np.arange(DFT2, dtype=np.float64),
                                    np.arange(n1, dtype=np.float64)) / n_lat_tok
    twc = jnp.asarray(np.broadcast_to(np.cos(tw_ang)[:, :, None], (DFT2, n1, LANES)), F32)
    tws = jnp.asarray(np.broadcast_to(np.sin(tw_ang)[:, :, None], (DFT2, n1, LANES)), F32)

    fv = f.reshape(bsz, n_tok // DFT2, DFT2 * F_WIDTH)
    mid = pl.pallas_call(
        _dft1_kernel,
        grid=(bsz, DFT2),
        in_specs=[
            pl.BlockSpec((1, n1, F_WIDTH), lambda b, j: (b, 0, j)),
            full(n1, n1), full(n1, n1),
            pl.BlockSpec((1, n1, LANES), lambda b, j: (j, 0, 0)),
            pl.BlockSpec((1, n1, LANES), lambda b, j: (j, 0, 0)),
        ],
        out_specs=pl.BlockSpec((1, n1, 2 * F_WIDTH), lambda b, j: (b, 0, j)),
        out_shape=jax.ShapeDtypeStruct((bsz, n1, DFT2 * 2 * F_WIDTH), F32),
        compiler_params=_cparams(2),
        name="dft_outer",
    )(fv, c1, s1, twc, tws)
    mid = mid.reshape(bsz, n1, DFT2, 2 * F_WIDTH)

    z = pl.pallas_call(
        functools.partial(_dft2_kernel, norm=1.0 / math.sqrt(n_lat_tok * F_GROUP_DIM)),
        grid=(bsz, n1),
        in_specs=[
            pl.BlockSpec((1, 1, DFT2, 2 * F_WIDTH), lambda b, j: (b, j, 0, 0)),
            full(DFT2, DFT2), full(DFT2, DFT2),
            full(F_GROUP_DIM, F_GROUP_DIM), full(F_GROUP_DIM, F_GROUP_DIM),
        ],
        out_specs=pl.BlockSpec((1, DFT2, F_WIDTH), lambda b, j: (b, 0, j)),
        out_shape=jax.ShapeDtypeStruct((bsz, n_tok // n1, n1 * F_WIDTH), F32),
        compiler_params=_cparams(2),
        name="dft_inner",
    )(mid, c2, s2, cc, sc)
    z = z.reshape(bsz, n_tok, F_WIDTH)

    ctx_blk = n_lat_tok // CTX_LEN
    z = pl.pallas_call(
        functools.partial(_dft_ctx_kernel, norm=1.0 / math.sqrt(CTX_LEN * F_GROUP_DIM)),
        grid=(bsz,),
        in_specs=[
            pl.BlockSpec((1, CTX_LEN, F_WIDTH), lambda b: (b, ctx_blk, 0)),
            full(CTX_LEN, CTX_LEN), full(CTX_LEN, CTX_LEN),
            full(F_GROUP_DIM, F_GROUP_DIM), full(F_GROUP_DIM, F_GROUP_DIM),
            pl.BlockSpec(memory_space=pl.ANY),
        ],
        out_specs=pl.BlockSpec((1, CTX_LEN, F_WIDTH), lambda b: (b, ctx_blk, 0)),
        out_shape=jax.ShapeDtypeStruct((bsz, n_tok, F_WIDTH), F32),
        input_output_aliases={5: 0},
        compiler_params=_cparams(1),
        name="dft_ctx",
    )(f, cctx, sctx, cc, sc, z)
    return z


def _merge_kernel(x_ref, o_ref, z_ref, mod_ref, g1_ref, g2_ref, wgate_ref, bgate_ref,
                  wao_ref, wfo_ref, wout_ref, wr_ref, br_ref, xo_ref, h2_ref, gates_ref):
    x = x_ref[0]
    mod = mod_ref[0]
    hb = _rms_mod(x, g1_ref[...], mod[0:1], mod[1:2]).astype(BF16)
    gates = jax.nn.sigmoid(jnp.dot(hb, wgate_ref[...], preferred_element_type=F32) + bgate_ref[...])
    ya = jnp.dot(o_ref[0], wao_ref[...], preferred_element_type=F32)
    yf = jnp.dot(z_ref[0].astype(BF16), wfo_ref[...], preferred_element_type=F32)
    merged = gates[:, :D_MODEL] * ya + gates[:, D_MODEL:] * yf
    out = jnp.dot(merged.astype(BF16), wout_ref[...], preferred_element_type=F32)
    xn = x + mod[2:3] * out
    xo_ref[0] = xn
    h2 = _rms_mod(xn, g2_ref[...], mod[3:4], mod[4:5])
    h2_ref[0] = h2.astype(BF16)

    logits = jnp.dot(h2, wr_ref[...], preferred_element_type=F32, precision=HIGHEST)
    aff = jax.nn.sigmoid(logits)
    lane = lax.broadcasted_iota(jnp.int32, (TM, LANES), 1)
    lanef = lane.astype(F32)
    neg = jnp.full((TM, LANES), -jnp.inf, F32)
    sel = jnp.where(lane < N_EXPERTS, aff + br_ref[...], neg)
    grp = lane // EXPERTS_PER_GROUP
    best = i1b = i2b = None
    for g in range(N_GROUPS):
        sg = jnp.where(grp == g, sel, neg)
        m1 = jnp.max(sg, axis=1, keepdims=True)
        i1 = jnp.min(jnp.where(sg == m1, lanef, float(LANES)), axis=1, keepdims=True)
        sg2 = jnp.where(lanef == i1, neg, sg)
        m2 = jnp.max(sg2, axis=1, keepdims=True)
        i2 = jnp.min(jnp.where(sg2 == m2, lanef, float(LANES)), axis=1, keepdims=True)
        score = m1 + m2
        if g == 0:
            best, i1b, i2b = score, i1, i2
        else:
            upd = score > best
            best = jnp.where(upd, score, best)
            i1b = jnp.where(upd, i1, i1b)
            i2b = jnp.where(upd, i2, i2b)
    hit1 = lanef == i1b
    hit2 = lanef == i2b
    a1 = jnp.sum(jnp.where(hit1, aff, 0.0), axis=1, keepdims=True)
    a2 = jnp.sum(jnp.where(hit2, aff, 0.0), axis=1, keepdims=True)
    den = a1 + a2
    gates_ref[0] = jnp.where(hit1, a1 / den, 0.0) + jnp.where(hit2, a2 / den, 0.0)


def _merge_call(x, o, z, mod, g1, g2, w_gate, b_gate, w_ao, w_fo, w_out, w_r, b_r, n_batch):
    bsz, n_tok, _ = x.shape
    nt = n_tok // TM
    tok = lambda w: pl.BlockSpec((1, TM, w), lambda b, t: (b, t, 0))
    full = lambda *shape: pl.BlockSpec(shape, lambda b, t: (0,) * len(shape))
    return pl.pallas_call(
        _merge_kernel,
        grid=(bsz, nt),
        in_specs=[
            tok(D_MODEL), tok(V_WIDTH), tok(F_WIDTH),
            pl.BlockSpec((1, N_MOD, D_MODEL),
                         lambda b, t: (jnp.where(t == nt - 1, n_batch, b), 0, 0)),
            full(1, D_MODEL), full(1, D_MODEL),
            full(D_MODEL, 2 * D_MODEL), full(1, 2 * D_MODEL),
            full(V_WIDTH, D_MODEL), full(F_WIDTH, D_MODEL), full(D_MODEL, D_MODEL),
            full(D_MODEL, LANES), full(1, LANES),
        ],
        out_specs=[tok(D_MODEL), tok(D_MODEL), tok(LANES)],
        out_shape=[
            jax.ShapeDtypeStruct((bsz, n_tok, D_MODEL), F32),
            jax.ShapeDtypeStruct((bsz, n_tok, D_MODEL), BF16),
            jax.ShapeDtypeStruct((bsz, n_tok, LANES), F32),
        ],
        compiler_params=_cparams(2),
        name="merge_route",
    )(x, o, z, mod, g1, g2, w_gate, b_gate, w_ao, w_fo, w_out, w_r, b_r)


def _moe_kernel(h_ref, g_ref, wg_ref, wu_ref, wd_ref, y_ref):
    e = pl.program_id(1)

    @pl.when(e == 0)
    def _():
        y_ref[...] = jnp.zeros(y_ref.shape, F32)

    h = h_ref[...]
    a = jnp.dot(h, wg_ref[0], preferred_element_type=F32)
    u = jnp.dot(h, wu_ref[0], preferred_element_type=F32)
    lane = lax.broadcasted_iota(jnp.int32, (TMOE, LANES), 1)
    ge = jnp.sum(jnp.where(lane == e, g_ref[...], 0.0), axis=1, keepdims=True)
    he = (a * jax.nn.sigmoid(a) * u * ge).astype(BF16)
    y_ref[...] += jnp.dot(he, wd_ref[0], preferred_element_type=F32)


def _moe_call(h2, gates, w_eg, w_eu, w_ed):
    n = h2.shape[0]
    return pl.pallas_call(
        _moe_kernel,
        grid=(n // TMOE, N_EXPERTS),
        in_specs=[
            pl.BlockSpec((TMOE, D_MODEL), lambda t, e: (t, 0)),
            pl.BlockSpec((TMOE, LANES), lambda t, e: (t, 0)),
            pl.BlockSpec((1, D_MODEL, D_EXPERT), lambda t, e: (e, 0, 0)),
            pl.BlockSpec((1, D_MODEL, D_EXPERT), lambda t, e: (e, 0, 0)),
            pl.BlockSpec((1, D_EXPERT, D_MODEL), lambda t, e: (e, 0, 0)),
        ],
        out_specs=pl.BlockSpec((TMOE, D_MODEL), lambda t, e: (t, 0)),
        out_shape=jax.ShapeDtypeStruct((n, D_MODEL), F32),
        compiler_params=_cparams(2),
        name="experts",
    )(h2, gates, w_eg, w_eu, w_ed)


def _final_kernel(x_ref, y_ref, mod_ref, g_ref, o_ref):
    x = x_ref[0] + mod_ref[0][5:6] * y_ref[0]
    o_ref[0] = x * lax.rsqrt(jnp.mean(x * x, axis=-1, keepdims=True) + EPS) * g_ref[...]


def _final_call(x, y, mod, g_final, n_lat_tok):
    bsz = x.shape[0]
    tok = pl.BlockSpec((1, TM, D_MODEL), lambda b, t: (b, t, 0))
    return pl.pallas_call(
        _final_kernel,
        grid=(bsz, n_lat_tok // TM),
        in_specs=[tok, tok,
                  pl.BlockSpec((1, N_MOD, D_MODEL), lambda b, t: (b, 0, 0)),
                  pl.BlockSpec((1, D_MODEL), lambda b, t: (0, 0))],
        out_specs=tok,
        out_shape=jax.ShapeDtypeStruct((bsz, n_lat_tok, D_MODEL), F32),
        compiler_params=_cparams(2),
        name="final_norm",
    )(x, y, mod, g_final)


def _rope_tables(n_lat_tok):
    t = np.arange(n_lat_tok)
    pos = np.stack([t // GRID_W, t % GRID_W], axis=0).astype(np.float64)
    inv = ROPE_THETA ** (-np.arange(ROPE_HALF, dtype=np.float64) / ROPE_HALF)
    d = np.arange(LANES) % HEAD_DIM
    axis = d // (2 * ROPE_HALF)
    freq = d % ROPE_HALF
    ang = pos[axis, :].T * inv[freq][None, :]
    sign = np.where((d % (2 * ROPE_HALF)) < ROPE_HALF, -1.0, 1.0)
    cos = np.concatenate([np.cos(ang), np.ones((CTX_LEN, LANES))], axis=0)
    sin = np.concatenate([np.sin(ang) * sign[None, :], np.zeros((CTX_LEN, LANES))], axis=0)
    return jnp.asarray(cos, F32), jnp.asarray(sin, F32)


def kernel(x, c, ctx, c_ctx, w_ada, b_ada, g_norm1, w_in, lam_qk, g_subln, w_attn_out, w_four_out,
           w_gate, b_gate, w_out, g_norm2, w_router, b_router, w_e_gate, w_e_up, w_e_down, g_final):
    bsz, n_lat_tok, d = x.shape
    depth = w_ada.shape[0]
    assert d == D_MODEL and ctx.shape[1] == CTX_LEN and bsz < MOD_ROWS
    assert n_lat_tok % TK == 0 and n_lat_tok % (DFT2 * 8) == 0
    n_tok = n_lat_tok + CTX_LEN
    assert (bsz * n_tok) % TMOE == 0

    cond = jnp.zeros((MOD_ROWS, D_MODEL), F32).at[:bsz].set(c).at[bsz].set(c_ctx)
    mods = _ada_call(cond, w_ada, b_ada).reshape(depth, MOD_ROWS, N_MOD, D_MODEL)
    cos_t, sin_t = _rope_tables(n_lat_tok)
    w_r = jnp.zeros((D_MODEL, LANES), F32).at[:, :N_EXPERTS].set(w_router)
    b_r = jnp.zeros((1, LANES), F32).at[0, :N_EXPERTS].set(b_router)

    xs = jnp.concatenate([x, ctx], axis=1)
    y = None
    for i in range(depth):
        lam_init = 0.8 - 0.6 * math.exp(-0.3 * i)
        xs, q, k, v, f = _proj_call(
            xs, y, mods[i - 1] if i else None, mods[i], g_norm1[i].reshape(1, D_MODEL),
            w_in[i].astype(BF16), cos_t, sin_t, bsz)
        o = _attn_call(q, k, v, lam_qk[i], g_subln[i], lam_init)
        z = _fourier_call(f, n_lat_tok)
        xs, h2, gates = _merge_call(
            xs, o, z, mods[i], g_norm1[i].reshape(1, D_MODEL), g_norm2[i].reshape(1, D_MODEL),
            w_gate[i].astype(BF16), b_gate[i].reshape(1, 2 * D_MODEL), w_attn_out[i].astype(BF16),
            w_four_out[i].astype(BF16), w_out[i].astype(BF16), w_r, b_r, bsz)
        y = _moe_call(h2.reshape(bsz * n_tok, D_MODEL), gates.reshape(bsz * n_tok, LANES),
                      w_e_gate[i].astype(BF16), w_e_up[i].astype(BF16), w_e_down[i].astype(BF16))
        y = y.reshape(bsz, n_tok, D_MODEL)
    return _final_call(xs, y, mods[depth - 1], g_final.reshape(1, D_MODEL), n_lat_tok)
```

```python
import functools
import math

import numpy as np
import jax
import jax.numpy as jnp
from jax import lax
from jax.experimental import pallas as pl
from jax.experimental.pallas import tpu as pltpu

F32 = jnp.float32
BF16 = jnp.bfloat16
HIGHEST = lax.Precision.HIGHEST

D_MODEL = 1024
GRID_W = 64
CTX_LEN = 256
N_HEADS = 8
HEAD_DIM = 64
V_DIM = 128
QK_WIDTH = 1024
V_WIDTH = 1024
F_GROUPS = 4
F_WIDTH = 512
F_GROUP_DIM = 128
IN_WIDTH = 3584
ROPE_THETA = 10000.0
ROPE_HALF = 16
N_EXPERTS = 16
N_GROUPS = 4
EXPERTS_PER_GROUP = 4
D_EXPERT = 512
EPS = 1e-6
N_MOD = 6
MOD_ROWS = 8

LANES = 128
TM = 256
TQ = 512
MXU_TILE = 256
BF16_ROWS = 16
SUM_ROWS = 16
MAX_KEY_TILES = 5
TMOE = 640
ADA_TN = 512
DFT2 = 128
Q_SCALE = (HEAD_DIM ** -0.5) * math.log2(math.e)
VMEM_LIMIT = 56 * 1024 * 1024


def _cparams(n_axes):
    return pltpu.CompilerParams(dimension_semantics=("arbitrary",) * n_axes,
                                vmem_limit_bytes=VMEM_LIMIT)


def _rms_mod(x, g, shift, scale):
    y = x * lax.rsqrt(jnp.mean(x * x, axis=-1, keepdims=True) + EPS) * g
    return y * (1.0 + scale) + shift


def _ada_kernel(c_ref, w_ref, b_ref, o_ref):
    c = c_ref[...]
    s = c * jax.nn.sigmoid(c)
    o_ref[0] = jnp.dot(s, w_ref[0], preferred_element_type=F32, precision=HIGHEST) + b_ref[0]


def _ada_call(cond, w_ada, b_ada):
    depth = w_ada.shape[0]
    n_out = w_ada.shape[2]
    return pl.pallas_call(
        _ada_kernel,
        grid=(depth, n_out // ADA_TN),
        in_specs=[
            pl.BlockSpec((MOD_ROWS, D_MODEL), lambda l, j: (0, 0)),
            pl.BlockSpec((1, D_MODEL, ADA_TN), lambda l, j: (l, 0, j)),
            pl.BlockSpec((1, 1, ADA_TN), lambda l, j: (l, 0, j)),
        ],
        out_specs=pl.BlockSpec((1, MOD_ROWS, ADA_TN), lambda l, j: (l, 0, j)),
        out_shape=jax.ShapeDtypeStruct((depth, MOD_ROWS, n_out), F32),
        compiler_params=_cparams(2),
        name="ada",
    )(cond, w_ada, b_ada.reshape(depth, 1, n_out))


def _proj_kernel(*refs, has_y):
    if has_y:
        (x_ref, y_ref, pmod_ref, mod_ref, g1_ref, win_ref, cos_ref, sin_ref,
         xo_ref, q_ref, k_ref, v_ref, f_ref) = refs
        x = x_ref[0] + pmod_ref[0][5:6] * y_ref[0]
        xo_ref[0] = x
    else:
        (x_ref, mod_ref, g1_ref, win_ref, cos_ref, sin_ref,
         q_ref, k_ref, v_ref, f_ref) = refs
        x = x_ref[0]
    mod = mod_ref[0]
    hb = _rms_mod(x, g1_ref[...], mod[0:1], mod[1:2]).astype(BF16)
    cos = cos_ref[...]
    sin = sin_ref[...]
    lane = lax.broadcasted_iota(jnp.int32, (TM, LANES), 1)
    first = (lane % (2 * ROPE_HALF)) < ROPE_HALF

    def rope(xc):
        up = pltpu.roll(xc, LANES - ROPE_HALF, 1)
        dn = pltpu.roll(xc, ROPE_HALF, 1)
        return xc * cos + jnp.where(first, up, dn) * sin

    w2 = 2 * LANES
    for c in range(QK_WIDTH // w2):
        uq = jnp.dot(hb, win_ref[:, c * w2:(c + 1) * w2], preferred_element_type=F32)
        uk = jnp.dot(hb, win_ref[:, QK_WIDTH + c * w2:QK_WIDTH + (c + 1) * w2],
                     preferred_element_type=F32)
        for j in range(2):
            lo = c * w2 + j * LANES
            q_ref[0, :, lo:lo + LANES] = (rope(uq[:, j * LANES:(j + 1) * LANES]) * Q_SCALE).astype(BF16)
            k_ref[0, :, lo:lo + LANES] = rope(uk[:, j * LANES:(j + 1) * LANES]).astype(BF16)
    for c in range(V_WIDTH // w2):
        lo = 2 * QK_WIDTH + c * w2
        v_ref[0, :, c * w2:(c + 1) * w2] = jnp.dot(
            hb, win_ref[:, lo:lo + w2], preferred_element_type=F32).astype(BF16)
    for c in range(F_WIDTH // w2):
        lo = 2 * QK_WIDTH + V_WIDTH + c * w2
        f_ref[0, :, c * w2:(c + 1) * w2] = jnp.dot(
            hb, win_ref[:, lo:lo + w2], preferred_element_type=F32)


def _proj_call(x, y, pmod, mod, g1, w_in, cos_t, sin_t, n_batch):
    bsz, n_tok, _ = x.shape
    nt = n_tok // TM
    has_y = y is not None

    def mod_map(b, t):
        return (jnp.where(t == nt - 1, n_batch, b), 0, 0)

    tok = lambda w: pl.BlockSpec((1, TM, w), lambda b, t: (b, t, 0))
    modspec = pl.BlockSpec((1, N_MOD, D_MODEL), mod_map)
    in_specs = [tok(D_MODEL)]
    args = [x]
    if has_y:
        in_specs += [tok(D_MODEL), modspec]
        args += [y, pmod]
    in_specs += [
        modspec,
        pl.BlockSpec((1, D_MODEL), lambda b, t: (0, 0)),
        pl.BlockSpec((D_MODEL, IN_WIDTH), lambda b, t: (0, 0)),
        pl.BlockSpec((TM, LANES), lambda b, t: (t, 0)),
        pl.BlockSpec((TM, LANES), lambda b, t: (t, 0)),
    ]
    args += [mod, g1, w_in, cos_t, sin_t]
    out_specs = [tok(QK_WIDTH), tok(QK_WIDTH), tok(V_WIDTH), tok(F_WIDTH)]
    out_shape = [
        jax.ShapeDtypeStruct((bsz, n_tok, QK_WIDTH), BF16),
        jax.ShapeDtypeStruct((bsz, n_tok, QK_WIDTH), BF16),
        jax.ShapeDtypeStruct((bsz, n_tok, V_WIDTH), BF16),
        jax.ShapeDtypeStruct((bsz, n_tok, F_WIDTH), F32),
    ]
    if has_y:
        out_specs = [tok(D_MODEL)] + out_specs
        out_shape = [jax.ShapeDtypeStruct((bsz, n_tok, D_MODEL), F32)] + out_shape
    outs = pl.pallas_call(
        functools.partial(_proj_kernel, has_y=has_y),
        grid=(bsz, nt),
        in_specs=in_specs,
        out_specs=out_specs,
        out_shape=out_shape,
        compiler_params=_cparams(2),
        name="proj",
    )(*args)
    if has_y:
        return outs
    return [x] + list(outs)


def _split_maps_t(qt):
    row = lax.broadcasted_iota(jnp.int32, qt.shape, 0)
    zero = jnp.zeros_like(qt)
    return jnp.concatenate([jnp.where(row < HEAD_DIM, qt, zero),
                            jnp.where(row >= HEAD_DIM, qt, zero)], axis=1)


def _ones_rows(n_cols):
    row = lax.broadcasted_iota(jnp.int32, (SUM_ROWS, n_cols), 0)
    return jnp.where(row == 0, 1.0, 0.0).astype(BF16)


def _softmax_block_t(q2t, kb, vbt_ext, m_prev, acc_prev):
    st = jnp.dot(kb, q2t, preferred_element_type=F32)
    m_new = jnp.maximum(m_prev, jnp.max(st, axis=0, keepdims=True))
    alpha = jnp.exp2(m_prev - m_new)
    pt = jnp.exp2(st - m_new).astype(BF16)
    acc = alpha * acc_prev + jnp.dot(vbt_ext, pt, preferred_element_type=F32)
    return m_new, acc


def _diff_combine_t(acc, lam_ref, gs_ref, lam_init):
    tq = acc.shape[1] // 2
    lp = lam_ref[...]
    lam = (jnp.exp(jnp.sum(lp[0:1] * lp[1:2], axis=1, keepdims=True))
           - jnp.exp(jnp.sum(lp[2:3] * lp[3:4], axis=1, keepdims=True)) + lam_init)
    pv = acc[:V_DIM]
    l = acc[V_DIM:V_DIM + 1]
    ot = pv[:, :tq] / l[:, :tq] - lam * (pv[:, tq:] / l[:, tq:])
    ot = ot * lax.rsqrt(jnp.mean(ot * ot, axis=0, keepdims=True) + EPS) * gs_ref[...]
    return (ot * (1.0 - lam_init)).T.astype(BF16)


def _attn_kernel(lam_ref, gs_ref, q_ref, k_ref, v_ref, o_ref, vt_ref, q2t_ref, m_ref, acc_ref,
                 s_ref, mb_ref, pt_ref, *, n_blocks, tk, lam_init):
    @pl.when(pl.program_id(2) == 0)
    def _():
        for j in range(n_blocks):
            vt_ref[:, j * tk:(j + 1) * tk] = v_ref[0, j * tk:(j + 1) * tk, :].astype(F32).T.astype(BF16)

    q2t_ref[...] = _split_maps_t(q_ref[0].astype(F32).T.astype(BF16))
    m_ref[...] = jnp.full(m_ref.shape, -jnp.inf, F32)
    acc_ref[...] = jnp.zeros(acc_ref.shape, F32)
    ones = _ones_rows(tk)
    chunks = range(0, tk, MXU_TILE)

    def key_offset(j):
        return j * tk if isinstance(j, int) else pl.multiple_of(j * tk, tk)

    def score_chunk(j, r, mb):
        st = jnp.dot(k_ref[0, pl.ds(key_offset(j) + r, MXU_TILE), :], q2t_ref[...],
                     preferred_element_type=F32)
        s_ref[r:r + MXU_TILE, :] = st
        cmax = jnp.max(st, axis=0, keepdims=True)
        return cmax if mb is None else jnp.maximum(mb, cmax)

    def exp_chunk(r, m_new):
        for rr in range(r, r + MXU_TILE, BF16_ROWS):
            pt_ref[rr:rr + BF16_ROWS, :] = jnp.exp2(s_ref[rr:rr + BF16_ROWS, :] - m_new).astype(BF16)

    def step(j, with_next):
        m_prev = m_ref[...]
        m_new = jnp.maximum(m_prev, mb_ref[...])
        alpha = jnp.exp2(m_prev - m_new)
        mb = None
        for r in chunks:
            exp_chunk(r, m_new)
            if with_next:
                mb = score_chunk(j + 1, r, mb)
        vbt = jnp.concatenate([vt_ref[:, pl.ds(key_offset(j), tk)], ones], axis=0)
        acc_ref[...] = alpha * acc_ref[...] + jnp.dot(vbt, pt_ref[...], preferred_element_type=F32)
        m_ref[...] = m_new
        if with_next:
            mb_ref[...] = mb

    mb = None
    for r in chunks:
        mb = score_chunk(0, r, mb)
    mb_ref[...] = mb

    def body(j, carry):
        step(j, True)
        return carry

    lax.fori_loop(0, n_blocks - 1, body, 0)
    step(n_blocks - 1, False)
    o_ref[0] = _diff_combine_t(acc_ref[...], lam_ref, gs_ref, lam_init)


def _attn_ctx_kernel(lam_ref, gs_ref, q_ref, k_ref, v_ref, o_any, o_ref, *, lam_init):
    del o_any
    q2t = _split_maps_t(q_ref[0].astype(F32).T.astype(BF16))
    vbt = jnp.concatenate([v_ref[0].astype(F32).T.astype(BF16), _ones_rows(CTX_LEN)], axis=0)
    m0 = jnp.full((1, 2 * CTX_LEN), -jnp.inf, F32)
    acc0 = jnp.zeros((V_DIM + SUM_ROWS, 2 * CTX_LEN), F32)
    _, acc = _softmax_block_t(q2t, k_ref[0], vbt, m0, acc0)
    o_ref[0] = _diff_combine_t(acc, lam_ref, gs_ref, lam_init)


def _key_block(n_tok):
    for mult in range(MAX_KEY_TILES, 0, -1):
        if n_tok % (mult * MXU_TILE) == 0:
            return mult * MXU_TILE
    raise ValueError(n_tok)


def _attn_call(q, k, v, lam_qk, g_subln, lam_init):
    bsz, n_tok, _ = q.shape
    n_lat_tok = n_tok - CTX_LEN
    tk = _key_block(n_tok)
    tq_step = TQ
    n_col = 2 * TQ
    small = [pl.BlockSpec((4, HEAD_DIM), lambda *_: (0, 0)),
             pl.BlockSpec((V_DIM, 1), lambda *_: (0, 0))]
    gs = g_subln.reshape(V_DIM, 1)
    o = pl.pallas_call(
        functools.partial(_attn_kernel, n_blocks=n_tok // tk, tk=tk, lam_init=lam_init),
        grid=(bsz, N_HEADS, n_lat_tok // tq_step),
        in_specs=small + [
            pl.BlockSpec((1, tq_step, LANES), lambda b, h, t: (b, t, h)),
            pl.BlockSpec((1, n_tok, LANES), lambda b, h, t: (b, 0, h)),
            pl.BlockSpec((1, n_tok, LANES), lambda b, h, t: (b, 0, h)),
        ],
        out_specs=pl.BlockSpec((1, tq_step, V_DIM), lambda b, h, t: (b, t, h)),
        out_shape=jax.ShapeDtypeStruct((bsz, n_tok, V_WIDTH), BF16),
        scratch_shapes=[
            pltpu.VMEM((V_DIM, n_tok), BF16),
            pltpu.VMEM((LANES, n_col), BF16),
            pltpu.VMEM((1, n_col), F32),
            pltpu.VMEM((V_DIM + SUM_ROWS, n_col), F32),
            pltpu.VMEM((tk, n_col), F32),
            pltpu.VMEM((1, n_col), F32),
            pltpu.VMEM((tk, n_col), BF16),
        ],
        compiler_params=_cparams(3),
        name="diff_attn",
    )(lam_qk, gs, q, k, v)
    ctx_blk = n_lat_tok // CTX_LEN
    ctx_spec = pl.BlockSpec((1, CTX_LEN, LANES), lambda b, h: (b, ctx_blk, h))
    return pl.pallas_call(
        functools.partial(_attn_ctx_kernel, lam_init=lam_init),
        grid=(bsz, N_HEADS),
        in_specs=small + [ctx_spec, ctx_spec, ctx_spec, pl.BlockSpec(memory_space=pl.ANY)],
        out_specs=ctx_spec,
        out_shape=jax.ShapeDtypeStruct((bsz, n_tok, V_WIDTH), BF16),
        input_output_aliases={5: 0},
        compiler_params=_cparams(2),
        name="diff_attn_ctx",
    )(lam_qk, gs, q, k, v, o)


def _dft1_kernel(u_ref, c1_ref, s1_ref, twc_ref, tws_ref, o_ref):
    u = u_ref[0]
    ar = jnp.dot(c1_ref[...], u, preferred_element_type=F32, precision=HIGHEST)
    ai = -jnp.dot(s1_ref[...], u, preferred_element_type=F32, precision=HIGHEST)
    twc = twc_ref[0]
    tws = tws_ref[0]
    for g in range(F_GROUPS):
        sl = slice(g * LANES, (g + 1) * LANES)
        o_ref[0, :, g * LANES:(g + 1) * LANES] = ar[:, sl] * twc + ai[:, sl] * tws
        o_ref[0, :, F_WIDTH + g * LANES:F_WIDTH + (g + 1) * LANES] = ai[:, sl] * twc - ar[:, sl] * tws


def _dft2_kernel(b_ref, c2_ref, s2_ref, cc_ref, sc_ref, o_ref, *, norm):
    bb = b_ref[0, 0]
    cb = jnp.dot(c2_ref[...], bb, preferred_element_type=F32, precision=HIGHEST)
    sb = jnp.dot(s2_ref[...], bb, preferred_element_type=F32, precision=HIGHEST)
    gr = cb[:, :F_WIDTH] + sb[:, F_WIDTH:]
    gi = cb[:, F_WIDTH:] - sb[:, :F_WIDTH]
    for g in range(F_GROUPS):
        sl = slice(g * LANES, (g + 1) * LANES)
        z = (jnp.dot(gr[:, sl], cc_ref[...], preferred_element_type=F32, precision=HIGHEST)
             + jnp.dot(gi[:, sl], sc_ref[...], preferred_element_type=F32, precision=HIGHEST))
        o_ref[0, :, sl] = z * norm


def _dft_ctx_kernel(f_ref, c_ref, s_ref, cc_ref, sc_ref, z_any, o_ref, *, norm):
    del z_any
    f = f_ref[0]
    gr = jnp.dot(c_ref[...], f, preferred_element_type=F32, precision=HIGHEST)
    gi = -jnp.dot(s_ref[...], f, preferred_element_type=F32, precision=HIGHEST)
    for g in range(F_GROUPS):
        sl = slice(g * LANES, (g + 1) * LANES)
        z = (jnp.dot(gr[:, sl], cc_ref[...], preferred_element_type=F32, precision=HIGHEST)
             + jnp.dot(gi[:, sl], sc_ref[...], preferred_element_type=F32, precision=HIGHEST))
        o_ref[0, :, sl] = z * norm


def _dft_mats(n):
    idx = np.arange(n, dtype=np.float64)
    ang = 2.0 * np.pi * np.outer(idx, idx) / n
    return jnp.asarray(np.cos(ang), F32), jnp.asarray(np.sin(ang), F32)


def _fourier_call(f, n_lat_tok):
    bsz, n_tok, _ = f.shape
    n1 = n_lat_tok // DFT2
    full = lambda *shape: pl.BlockSpec(shape, lambda *_: (0,) * len(shape))
    c1, s1 = _dft_mats(n1)
    c2, s2 = _dft_mats(DFT2)
    cc, sc = _dft_mats(F_GROUP_DIM)
    cctx, sctx = _dft_mats(CTX_LEN)
    tw_ang = 2.0 * np.pi * np.outer(np.arange(DFT2, dtype=np.float64),
                                    np.arange(n1, dtype=np.float64)) / n_lat_tok
    twc = jnp.asarray(np.broadcast_to(np.cos(tw_ang)[:, :, None], (DFT2, n1, LANES)), F32)
    tws = jnp.asarray(np.broadcast_to(np.sin(tw_ang)[:, :, None], (DFT2, n1, LANES)), F32)

    fv = f.reshape(bsz, n_tok // DFT2, DFT2 * F_WIDTH)
    mid = pl.pallas_call(
        _dft1_kernel,
        grid=(bsz, DFT2),
        in_specs=[
            pl.BlockSpec((1, n1, F_WIDTH), lambda b, j: (b, 0, j)),
            full(n1, n1), full(n1, n1),
            pl.BlockSpec((1, n1, LANES), lambda b, j: (j, 0, 0)),
            pl.BlockSpec((1, n1, LANES), lambda b, j: (j, 0, 0)),
        ],
        out_specs=pl.BlockSpec((1, n1, 2 * F_WIDTH), lambda b, j: (b, 0, j)),
        out_shape=jax.ShapeDtypeStruct((bsz, n1, DFT2 * 2 * F_WIDTH), F32),
        compiler_params=_cparams(2),
        name="dft_outer",
    )(fv, c1, s1, twc, tws)
    mid = mid.reshape(bsz, n1, DFT2, 2 * F_WIDTH)

    z = pl.pallas_call(
        functools.partial(_dft2_kernel, norm=1.0 / math.sqrt(n_lat_tok * F_GROUP_DIM)),
        grid=(bsz, n1),
        in_specs=[
            pl.BlockSpec((1, 1, DFT2, 2 * F_WIDTH), lambda b, j: (b, j, 0, 0)),
            full(DFT2, DFT2), full(DFT2, DFT2),
            full(F_GROUP_DIM, F_GROUP_DIM), full(F_GROUP_DIM, F_GROUP_DIM),
        ],
        out_specs=pl.BlockSpec((1, DFT2, F_WIDTH), lambda b, j: (b, 0, j)),
        out_shape=jax.ShapeDtypeStruct((bsz, n_tok // n1, n1 * F_WIDTH), F32),
        compiler_params=_cparams(2),
        name="dft_inner",
    )(mid, c2, s2, cc, sc)
    z = z.reshape(bsz, n_tok, F_WIDTH)

    ctx_blk = n_lat_tok // CTX_LEN
    z = pl.pallas_call(
        functools.partial(_dft_ctx_kernel, norm=1.0 / math.sqrt(CTX_LEN * F_GROUP_DIM)),
        grid=(bsz,),
        in_specs=[
            pl.BlockSpec((1, CTX_LEN, F_WIDTH), lambda b: (b, ctx_blk, 0)),
            full(CTX_LEN, CTX_LEN), full(CTX_LEN, CTX_LEN),
            full(F_GROUP_DIM, F_GROUP_DIM), full(F_GROUP_DIM, F_GROUP_DIM),
            pl.BlockSpec(memory_space=pl.ANY),
        ],
        out_specs=pl.BlockSpec((1, CTX_LEN, F_WIDTH), lambda b: (b, ctx_blk, 0)),
        out_shape=jax.ShapeDtypeStruct((bsz, n_tok, F_WIDTH), F32),
        input_output_aliases={5: 0},
        compiler_params=_cparams(1),
        name="dft_ctx",
    )(f, cctx, sctx, cc, sc, z)
    return z


def _merge_kernel(x_ref, o_ref, z_ref, mod_ref, g1_ref, g2_ref, wgate_ref, bgate_ref,
                  wao_ref, wfo_ref, wout_ref, wr_ref, br_ref, xo_ref, h2_ref, gates_ref):
    x = x_ref[0]
    mod = mod_ref[0]
    hb = _rms_mod(x, g1_ref[...], mod[0:1], mod[1:2]).astype(BF16)
    gates = jax.nn.sigmoid(jnp.dot(hb, wgate_ref[...], preferred_element_type=F32) + bgate_ref[...])
    ya = jnp.dot(o_ref[0], wao_ref[...], preferred_element_type=F32)
    yf = jnp.dot(z_ref[0].astype(BF16), wfo_ref[...], preferred_element_type=F32)
    merged = gates[:, :D_MODEL] * ya + gates[:, D_MODEL:] * yf
    out = jnp.dot(merged.astype(BF16), wout_ref[...], preferred_element_type=F32)
    xn = x + mod[2:3] * out
    xo_ref[0] = xn
    h2 = _rms_mod(xn, g2_ref[...], mod[3:4], mod[4:5])
    h2_ref[0] = h2.astype(BF16)

    logits = jnp.dot(h2, wr_ref[...], preferred_element_type=F32, precision=HIGHEST)
    aff = jax.nn.sigmoid(logits)
    lane = lax.broadcasted_iota(jnp.int32, (TM, LANES), 1)
    lanef = lane.astype(F32)
    neg = jnp.full((TM, LANES), -jnp.inf, F32)
    sel = jnp.where(lane < N_EXPERTS, aff + br_ref[...], neg)
    grp = lane // EXPERTS_PER_GROUP
    best = i1b = i2b = None
    for g in range(N_GROUPS):
        sg = jnp.where(grp == g, sel, neg)
        m1 = jnp.max(sg, axis=1, keepdims=True)
        i1 = jnp.min(jnp.where(sg == m1, lanef, float(LANES)), axis=1, keepdims=True)
        sg2 = jnp.where(lanef == i1, neg, sg)
        m2 = jnp.max(sg2, axis=1, keepdims=True)
        i2 = jnp.min(jnp.where(sg2 == m2, lanef, float(LANES)), axis=1, keepdims=True)
        score = m1 + m2
        if g == 0:
            best, i1b, i2b = score, i1, i2
        else:
            upd = score > best
            best = jnp.where(upd, score, best)
            i1b = jnp.where(upd, i1, i1b)
            i2b = jnp.where(upd, i2, i2b)
    hit1 = lanef == i1b
    hit2 = lanef == i2b
    a1 = jnp.sum(jnp.where(hit1, aff, 0.0), axis=1, keepdims=True)
    a2 = jnp.sum(jnp.where(hit2, aff, 0.0), axis=1, keepdims=True)
    den = a1 + a2
    gates_ref[0] = jnp.where(hit1, a1 / den, 0.0) + jnp.where(hit2, a2 / den, 0.0)


def _merge_call(x, o, z, mod, g1, g2, w_gate, b_gate, w_ao, w_fo, w_out, w_r, b_r, n_batch):
    bsz, n_tok, _ = x.shape
    nt = n_tok // TM
    tok = lambda w: pl.BlockSpec((1, TM, w), lambda b, t: (b, t, 0))
    full = lambda *shape: pl.BlockSpec(shape, lambda b, t: (0,) * len(shape))
    return pl.pallas_call(
        _merge_kernel,
        grid=(bsz, nt),
        in_specs=[
            tok(D_MODEL), tok(V_WIDTH), tok(F_WIDTH),
            pl.BlockSpec((1, N_MOD, D_MODEL),
                         lambda b, t: (jnp.where(t == nt - 1, n_batch, b), 0, 0)),
            full(1, D_MODEL), full(1, D_MODEL),
            full(D_MODEL, 2 * D_MODEL), full(1, 2 * D_MODEL),
            full(V_WIDTH, D_MODEL), full(F_WIDTH, D_MODEL), full(D_MODEL, D_MODEL),
            full(D_MODEL, LANES), full(1, LANES),
        ],
        out_specs=[tok(D_MODEL), tok(D_MODEL), tok(LANES)],
        out_shape=[
            jax.ShapeDtypeStruct((bsz, n_tok, D_MODEL), F32),
            jax.ShapeDtypeStruct((bsz, n_tok, D_MODEL), BF16),
            jax.ShapeDtypeStruct((bsz, n_tok, LANES), F32),
        ],
        compiler_params=_cparams(2),
        name="merge_route",
    )(x, o, z, mod, g1, g2, w_gate, b_gate, w_ao, w_fo, w_out, w_r, b_r)


def _moe_kernel(h_ref, g_ref, wg_ref, wu_ref, wd_ref, y_ref):
    e = pl.program_id(1)

    @pl.when(e == 0)
    def _():
        y_ref[...] = jnp.zeros(y_ref.shape, F32)

    h = h_ref[...]
    a = jnp.dot(h, wg_ref[0], preferred_element_type=F32)
    u = jnp.dot(h, wu_ref[0], preferred_element_type=F32)
    lane = lax.broadcasted_iota(jnp.int32, (TMOE, LANES), 1)
    ge = jnp.sum(jnp.where(lane == e, g_ref[...], 0.0), axis=1, keepdims=True)
    he = (a * jax.nn.sigmoid(a) * u * ge).astype(BF16)
    y_ref[...] += jnp.dot(he, wd_ref[0], preferred_element_type=F32)


def _moe_call(h2, gates, w_eg, w_eu, w_ed):
    n = h2.shape[0]
    return pl.pallas_call(
        _moe_kernel,
        grid=(n // TMOE, N_EXPERTS),
        in_specs=[
            pl.BlockSpec((TMOE, D_MODEL), lambda t, e: (t, 0)),
            pl.BlockSpec((TMOE, LANES), lambda t, e: (t, 0)),
            pl.BlockSpec((1, D_MODEL, D_EXPERT), lambda t, e: (e, 0, 0)),
            pl.BlockSpec((1, D_MODEL, D_EXPERT), lambda t, e: (e, 0, 0)),
            pl.BlockSpec((1, D_EXPERT, D_MODEL), lambda t, e: (e, 0, 0)),
        ],
        out_specs=pl.BlockSpec((TMOE, D_MODEL), lambda t, e: (t, 0)),
        out_shape=jax.ShapeDtypeStruct((n, D_MODEL), F32),
        compiler_params=_cparams(2),
        name="experts",
    )(h2, gates, w_eg, w_eu, w_ed)


def _final_kernel(x_ref, y_ref, mod_ref, g_ref, o_ref):
    x = x_ref[0] + mod_ref[0][5:6] * y_ref[0]
    o_ref[0] = x * lax.rsqrt(jnp.mean(x * x, axis=-1, keepdims=True) + EPS) * g_ref[...]


def _final_call(x, y, mod, g_final, n_lat_tok):
    bsz = x.shape[0]
    tok = pl.BlockSpec((1, TM, D_MODEL), lambda b, t: (b, t, 0))
    return pl.pallas_call(
        _final_kernel,
        grid=(bsz, n_lat_tok // TM),
        in_specs=[tok, tok,
                  pl.BlockSpec((1, N_MOD, D_MODEL), lambda b, t: (b, 0, 0)),
                  pl.BlockSpec((1, D_MODEL), lambda b, t: (0, 0))],
        out_specs=tok,
        out_shape=jax.ShapeDtypeStruct((bsz, n_lat_tok, D_MODEL), F32),
        compiler_params=_cparams(2),
        name="final_norm",
    )(x, y, mod, g_final)


def _rope_tables(n_lat_tok):
    t = np.arange(n_lat_tok)
    pos = np.stack([t // GRID_W, t % GRID_W], axis=0).astype(np.float64)
    inv = ROPE_THETA ** (-np.arange(ROPE_HALF, dtype=np.float64) / ROPE_HALF)
    d = np.arange(LANES) % HEAD_DIM
    axis = d // (2 * ROPE_HALF)
    freq = d % ROPE_HALF
    ang = pos[axis, :].T * inv[freq][None, :]
    sign = np.where((d % (2 * ROPE_HALF)) < ROPE_HALF, -1.0, 1.0)
    cos = np.concatenate([np.cos(ang), np.ones((CTX_LEN, LANES))], axis=0)
    sin = np.concatenate([np.sin(ang) * sign[None, :], np.zeros((CTX_LEN, LANES))], axis=0)
    return jnp.asarray(cos, F32), jnp.asarray(sin, F32)


def kernel(x, c, ctx, c_ctx, w_ada, b_ada, g_norm1, w_in, lam_qk, g_subln, w_attn_out, w_four_out,
           w_gate, b_gate, w_out, g_norm2, w_router, b_router, w_e_gate, w_e_up, w_e_down, g_final):
    bsz, n_lat_tok, d = x.shape
    depth = w_ada.shape[0]
    assert d == D_MODEL and ctx.shape[1] == CTX_LEN and bsz < MOD_ROWS
    assert n_lat_tok % TQ == 0 and n_lat_tok % (DFT2 * 8) == 0
    n_tok = n_lat_tok + CTX_LEN
    assert (bsz * n_tok) % TMOE == 0

    cond = jnp.zeros((MOD_ROWS, D_MODEL), F32).at[:bsz].set(c).at[bsz].set(c_ctx)
    mods = _ada_call(cond, w_ada, b_ada).reshape(depth, MOD_ROWS, N_MOD, D_MODEL)
    cos_t, sin_t = _rope_tables(n_lat_tok)
    w_r = jnp.zeros((D_MODEL, LANES), F32).at[:, :N_EXPERTS].set(w_router)
    b_r = jnp.zeros((1, LANES), F32).at[0, :N_EXPERTS].set(b_router)

    xs = jnp.concatenate([x, ctx], axis=1)
    y = None
    for i in range(depth):
        lam_init = 0.8 - 0.6 * math.exp(-0.3 * i)
        xs, q, k, v, f = _proj_call(
            xs, y, mods[i - 1] if i else None, mods[i], g_norm1[i].reshape(1, D_MODEL),
            w_in[i].astype(BF16), cos_t, sin_t, bsz)
        o = _attn_call(q, k, v, lam_qk[i], g_subln[i], lam_init)
        z = _fourier_call(f, n_lat_tok)
        xs, h2, gates = _merge_call(
            xs, o, z, mods[i], g_norm1[i].reshape(1, D_MODEL), g_norm2[i].reshape(1, D_MODEL),
            w_gate[i].astype(BF16), b_gate[i].reshape(1, 2 * D_MODEL), w_attn_out[i].astype(BF16),
            w_four_out[i].astype(BF16), w_out[i].astype(BF16), w_r, b_r, bsz)
        y = _moe_call(h2.reshape(bsz * n_tok, D_MODEL), gates.reshape(bsz * n_tok, LANES),
                      w_e_gate[i].astype(BF16), w_e_up[i].astype(BF16), w_e_down[i].astype(BF16))
        y = y.reshape(bsz, n_tok, D_MODEL)
    return _final_call(xs, y, mods[depth - 1], g_final.reshape(1, D_MODEL), n_lat_tok)
```

```python
import functools
import math

import numpy as np
import jax
import jax.numpy as jnp
from jax import lax
from jax.experimental import pallas as pl
from jax.experimental.pallas import tpu as pltpu

F32 = jnp.float32
BF16 = jnp.bfloat16
HIGHEST = lax.Precision.HIGHEST

D_MODEL = 1024
GRID_W = 64
CTX_LEN = 256
N_HEADS = 8
HEAD_DIM = 64
V_DIM = 128
QK_WIDTH = 1024
V_WIDTH = 1024
F_GROUPS = 4
F_WIDTH = 512
F_GROUP_DIM = 128
IN_WIDTH = 3584
ROPE_THETA = 10000.0
ROPE_HALF = 16
N_EXPERTS = 16
N_GROUPS = 4
EXPERTS_PER_GROUP = 4
D_EXPERT = 512
EPS = 1e-6
N_MOD = 6
MOD_ROWS = 8

LANES = 128
TM = 256
TQ = 512
MXU_TILE = 256
BF16_ROWS = 16
SUM_ROWS = 16
MAX_KEY_TILES = 5
TMOE = 640
ADA_TN = 512
DFT2 = 128
DFT_GROUP = 8
Q_SCALE = (HEAD_DIM ** -0.5) * math.log2(math.e)
VMEM_LIMIT = 56 * 1024 * 1024


def _cparams(n_axes):
    return pltpu.CompilerParams(dimension_semantics=("arbitrary",) * n_axes,
                                vmem_limit_bytes=VMEM_LIMIT)


def _rms_mod(x, g, shift, scale):
    y = x * lax.rsqrt(jnp.mean(x * x, axis=-1, keepdims=True) + EPS) * g
    return y * (1.0 + scale) + shift


def _ada_kernel(c_ref, w_ref, b_ref, o_ref):
    c = c_ref[...]
    s = c * jax.nn.sigmoid(c)
    o_ref[0] = jnp.dot(s, w_ref[0], preferred_element_type=F32, precision=HIGHEST) + b_ref[0]


def _ada_call(cond, w_ada, b_ada):
    depth = w_ada.shape[0]
    n_out = w_ada.shape[2]
    return pl.pallas_call(
        _ada_kernel,
        grid=(depth, n_out // ADA_TN),
        in_specs=[
            pl.BlockSpec((MOD_ROWS, D_MODEL), lambda l, j: (0, 0)),
            pl.BlockSpec((1, D_MODEL, ADA_TN), lambda l, j: (l, 0, j)),
            pl.BlockSpec((1, 1, ADA_TN), lambda l, j: (l, 0, j)),
        ],
        out_specs=pl.BlockSpec((1, MOD_ROWS, ADA_TN), lambda l, j: (l, 0, j)),
        out_shape=jax.ShapeDtypeStruct((depth, MOD_ROWS, n_out), F32),
        compiler_params=_cparams(2),
        name="ada",
    )(cond, w_ada, b_ada.reshape(depth, 1, n_out))


def _proj_kernel(*refs, has_y):
    if has_y:
        (x_ref, y_ref, pmod_ref, mod_ref, g1_ref, win_ref, cos_ref, sin_ref,
         xo_ref, q_ref, k_ref, v_ref, f_ref) = refs
        x = x_ref[0] + pmod_ref[0][5:6] * y_ref[0]
        xo_ref[0] = x
    else:
        (x_ref, mod_ref, g1_ref, win_ref, cos_ref, sin_ref,
         q_ref, k_ref, v_ref, f_ref) = refs
        x = x_ref[0]
    mod = mod_ref[0]
    hb = _rms_mod(x, g1_ref[...], mod[0:1], mod[1:2]).astype(BF16)
    cos = cos_ref[...]
    sin = sin_ref[...]
    lane = lax.broadcasted_iota(jnp.int32, (TM, LANES), 1)
    first = (lane % (2 * ROPE_HALF)) < ROPE_HALF

    def rope(xc):
        up = pltpu.roll(xc, LANES - ROPE_HALF, 1)
        dn = pltpu.roll(xc, ROPE_HALF, 1)
        return xc * cos + jnp.where(first, up, dn) * sin

    w2 = 2 * LANES
    for c in range(QK_WIDTH // w2):
        uq = jnp.dot(hb, win_ref[:, c * w2:(c + 1) * w2], preferred_element_type=F32)
        uk = jnp.dot(hb, win_ref[:, QK_WIDTH + c * w2:QK_WIDTH + (c + 1) * w2],
                     preferred_element_type=F32)
        for j in range(2):
            lo = c * w2 + j * LANES
            q_ref[0, :, lo:lo + LANES] = (rope(uq[:, j * LANES:(j + 1) * LANES]) * Q_SCALE).astype(BF16)
            k_ref[0, :, lo:lo + LANES] = rope(uk[:, j * LANES:(j + 1) * LANES]).astype(BF16)
    for c in range(V_WIDTH // w2):
        lo = 2 * QK_WIDTH + c * w2
        v_ref[0, :, c * w2:(c + 1) * w2] = jnp.dot(
            hb, win_ref[:, lo:lo + w2], preferred_element_type=F32).astype(BF16)
    for c in range(F_WIDTH // w2):
        lo = 2 * QK_WIDTH + V_WIDTH + c * w2
        f_ref[0, :, c * w2:(c + 1) * w2] = jnp.dot(
            hb, win_ref[:, lo:lo + w2], preferred_element_type=F32)


def _proj_call(x, y, pmod, mod, g1, w_in, cos_t, sin_t, n_batch):
    bsz, n_tok, _ = x.shape
    nt = n_tok // TM
    has_y = y is not None

    def mod_map(b, t):
        return (jnp.where(t == nt - 1, n_batch, b), 0, 0)

    tok = lambda w: pl.BlockSpec((1, TM, w), lambda b, t: (b, t, 0))
    modspec = pl.BlockSpec((1, N_MOD, D_MODEL), mod_map)
    in_specs = [tok(D_MODEL)]
    args = [x]
    if has_y:
        in_specs += [tok(D_MODEL), modspec]
        args += [y, pmod]
    in_specs += [
        modspec,
        pl.BlockSpec((1, D_MODEL), lambda b, t: (0, 0)),
        pl.BlockSpec((D_MODEL, IN_WIDTH), lambda b, t: (0, 0)),
        pl.BlockSpec((TM, LANES), lambda b, t: (t, 0)),
        pl.BlockSpec((TM, LANES), lambda b, t: (t, 0)),
    ]
    args += [mod, g1, w_in, cos_t, sin_t]
    out_specs = [tok(QK_WIDTH), tok(QK_WIDTH), tok(V_WIDTH), tok(F_WIDTH)]
    out_shape = [
        jax.ShapeDtypeStruct((bsz, n_tok, QK_WIDTH), BF16),
        jax.ShapeDtypeStruct((bsz, n_tok, QK_WIDTH), BF16),
        jax.ShapeDtypeStruct((bsz, n_tok, V_WIDTH), BF16),
        jax.ShapeDtypeStruct((bsz, n_tok, F_WIDTH), F32),
    ]
    if has_y:
        out_specs = [tok(D_MODEL)] + out_specs
        out_shape = [jax.ShapeDtypeStruct((bsz, n_tok, D_MODEL), F32)] + out_shape
    outs = pl.pallas_call(
        functools.partial(_proj_kernel, has_y=has_y),
        grid=(bsz, nt),
        in_specs=in_specs,
        out_specs=out_specs,
        out_shape=out_shape,
        compiler_params=_cparams(2),
        name="proj",
    )(*args)
    if has_y:
        return outs
    return [x] + list(outs)


def _split_maps_t(qt):
    row = lax.broadcasted_iota(jnp.int32, qt.shape, 0)
    zero = jnp.zeros_like(qt)
    return jnp.concatenate([jnp.where(row < HEAD_DIM, qt, zero),
                            jnp.where(row >= HEAD_DIM, qt, zero)], axis=1)


def _ones_rows(n_cols):
    row = lax.broadcasted_iota(jnp.int32, (SUM_ROWS, n_cols), 0)
    return jnp.where(row == 0, 1.0, 0.0).astype(BF16)


def _softmax_block_t(q2t, kb, vbt_ext, m_prev, acc_prev):
    st = jnp.dot(kb, q2t, preferred_element_type=F32)
    m_new = jnp.maximum(m_prev, jnp.max(st, axis=0, keepdims=True))
    alpha = jnp.exp2(m_prev - m_new)
    pt = jnp.exp2(st - m_new).astype(BF16)
    acc = alpha * acc_prev + jnp.dot(vbt_ext, pt, preferred_element_type=F32)
    return m_new, acc


def _diff_combine_t(acc, lam_ref, gs_ref, lam_init):
    tq = acc.shape[1] // 2
    lp = lam_ref[...]
    lam = (jnp.exp(jnp.sum(lp[0:1] * lp[1:2], axis=1, keepdims=True))
           - jnp.exp(jnp.sum(lp[2:3] * lp[3:4], axis=1, keepdims=True)) + lam_init)
    pv = acc[:V_DIM]
    l = acc[V_DIM:V_DIM + 1]
    ot = pv[:, :tq] / l[:, :tq] - lam * (pv[:, tq:] / l[:, tq:])
    ot = ot * lax.rsqrt(jnp.mean(ot * ot, axis=0, keepdims=True) + EPS) * gs_ref[...]
    return (ot * (1.0 - lam_init)).T.astype(BF16)


def _attn_kernel(lam_ref, gs_ref, q_ref, k_ref, v_ref, o_ref, vt_ref, q2t_ref, m_ref, acc_ref,
                 s_ref, mb_ref, pt_ref, *, n_blocks, tk, lam_init):
    @pl.when(pl.program_id(2) == 0)
    def _():
        for j in range(n_blocks):
            vt_ref[:, j * tk:(j + 1) * tk] = v_ref[0, j * tk:(j + 1) * tk, :].astype(F32).T.astype(BF16)

    q2t_ref[...] = _split_maps_t(q_ref[0].astype(F32).T.astype(BF16))
    m_ref[...] = jnp.full(m_ref.shape, -jnp.inf, F32)
    acc_ref[...] = jnp.zeros(acc_ref.shape, F32)
    ones = _ones_rows(tk)
    chunks = range(0, tk, MXU_TILE)

    def key_offset(j):
        return j * tk if isinstance(j, int) else pl.multiple_of(j * tk, tk)

    def score_chunk(j, r, mb):
        st = jnp.dot(k_ref[0, pl.ds(key_offset(j) + r, MXU_TILE), :], q2t_ref[...],
                     preferred_element_type=F32)
        s_ref[r:r + MXU_TILE, :] = st
        cmax = jnp.max(st, axis=0, keepdims=True)
        return cmax if mb is None else jnp.maximum(mb, cmax)

    def exp_chunk(r, m_new):
        for rr in range(r, r + MXU_TILE, BF16_ROWS):
            pt_ref[rr:rr + BF16_ROWS, :] = jnp.exp2(s_ref[rr:rr + BF16_ROWS, :] - m_new).astype(BF16)

    def step(j, with_next):
        m_prev = m_ref[...]
        m_new = jnp.maximum(m_prev, mb_ref[...])
        alpha = jnp.exp2(m_prev - m_new)
        mb = None
        for r in chunks:
            exp_chunk(r, m_new)
            if with_next:
                mb = score_chunk(j + 1, r, mb)
        vbt = jnp.concatenate([vt_ref[:, pl.ds(key_offset(j), tk)], ones], axis=0)
        acc_ref[...] = alpha * acc_ref[...] + jnp.dot(vbt, pt_ref[...], preferred_element_type=F32)
        m_ref[...] = m_new
        if with_next:
            mb_ref[...] = mb

    mb = None
    for r in chunks:
        mb = score_chunk(0, r, mb)
    mb_ref[...] = mb

    def body(j, carry):
        step(j, True)
        return carry

    lax.fori_loop(0, n_blocks - 1, body, 0)
    step(n_blocks - 1, False)
    o_ref[0] = _diff_combine_t(acc_ref[...], lam_ref, gs_ref, lam_init)


def _attn_ctx_kernel(lam_ref, gs_ref, q_ref, k_ref, v_ref, o_any, o_ref, *, lam_init):
    del o_any
    q2t = _split_maps_t(q_ref[0].astype(F32).T.astype(BF16))
    vbt = jnp.concatenate([v_ref[0].astype(F32).T.astype(BF16), _ones_rows(CTX_LEN)], axis=0)
    m0 = jnp.full((1, 2 * CTX_LEN), -jnp.inf, F32)
    acc0 = jnp.zeros((V_DIM + SUM_ROWS, 2 * CTX_LEN), F32)
    _, acc = _softmax_block_t(q2t, k_ref[0], vbt, m0, acc0)
    o_ref[0] = _diff_combine_t(acc, lam_ref, gs_ref, lam_init)


def _key_block(n_tok):
    for mult in range(MAX_KEY_TILES, 0, -1):
        if n_tok % (mult * MXU_TILE) == 0:
            return mult * MXU_TILE
    raise ValueError(n_tok)


def _attn_call(q, k, v, lam_qk, g_subln, lam_init):
    bsz, n_tok, _ = q.shape
    n_lat_tok = n_tok - CTX_LEN
    tk = _key_block(n_tok)
    tq_step = TQ
    n_col = 2 * TQ
    small = [pl.BlockSpec((4, HEAD_DIM), lambda *_: (0, 0)),
             pl.BlockSpec((V_DIM, 1), lambda *_: (0, 0))]
    gs = g_subln.reshape(V_DIM, 1)
    o = pl.pallas_call(
        functools.partial(_attn_kernel, n_blocks=n_tok // tk, tk=tk, lam_init=lam_init),
        grid=(bsz, N_HEADS, n_lat_tok // tq_step),
        in_specs=small + [
            pl.BlockSpec((1, tq_step, LANES), lambda b, h, t: (b, t, h)),
            pl.BlockSpec((1, n_tok, LANES), lambda b, h, t: (b, 0, h)),
            pl.BlockSpec((1, n_tok, LANES), lambda b, h, t: (b, 0, h)),
        ],
        out_specs=pl.BlockSpec((1, tq_step, V_DIM), lambda b, h, t: (b, t, h)),
        out_shape=jax.ShapeDtypeStruct((bsz, n_tok, V_WIDTH), BF16),
        scratch_shapes=[
            pltpu.VMEM((V_DIM, n_tok), BF16),
            pltpu.VMEM((LANES, n_col), BF16),
            pltpu.VMEM((1, n_col), F32),
            pltpu.VMEM((V_DIM + SUM_ROWS, n_col), F32),
            pltpu.VMEM((tk, n_col), F32),
            pltpu.VMEM((1, n_col), F32),
            pltpu.VMEM((tk, n_col), BF16),
        ],
        compiler_params=_cparams(3),
        name="diff_attn",
    )(lam_qk, gs, q, k, v)
    ctx_blk = n_lat_tok // CTX_LEN
    ctx_spec = pl.BlockSpec((1, CTX_LEN, LANES), lambda b, h: (b, ctx_blk, h))
    return pl.pallas_call(
        functools.partial(_attn_ctx_kernel, lam_init=lam_init),
        grid=(bsz, N_HEADS),
        in_specs=small + [ctx_spec, ctx_spec, ctx_spec, pl.BlockSpec(memory_space=pl.ANY)],
        out_specs=ctx_spec,
        out_shape=jax.ShapeDtypeStruct((bsz, n_tok, V_WIDTH), BF16),
        input_output_aliases={5: 0},
        compiler_params=_cparams(2),
        name="diff_attn_ctx",
    )(lam_qk, gs, q, k, v, o)


def _dft1_kernel(u_ref, c1_ref, s1_ref, twc_ref, tws_ref, o_ref):
    for j in range(DFT_GROUP):
        u = u_ref[0, :, j, :].astype(BF16)
        ar = jnp.dot(c1_ref[...], u, preferred_element_type=F32)
        ai = -jnp.dot(s1_ref[...], u, preferred_element_type=F32)
        twc = twc_ref[j]
        tws = tws_ref[j]
        for g in range(F_GROUPS):
            sl = slice(g * LANES, (g + 1) * LANES)
            o_ref[0, :, j, g * LANES:(g + 1) * LANES] = ar[:, sl] * twc + ai[:, sl] * tws
            o_ref[0, :, j, F_WIDTH + g * LANES:F_WIDTH + (g + 1) * LANES] = ai[:, sl] * twc - ar[:, sl] * tws


def _channel_dft(gr, gi, cc_ref, sc_ref, norm):
    grb = gr.astype(BF16)
    gib = gi.astype(BF16)
    out = []
    for g in range(F_GROUPS):
        sl = slice(g * LANES, (g + 1) * LANES)
        out.append((jnp.dot(grb[:, sl], cc_ref[...], preferred_element_type=F32)
                    + jnp.dot(gib[:, sl], sc_ref[...], preferred_element_type=F32)) * norm)
    return out


def _dft2_kernel(b_ref, c2_ref, s2_ref, cc_ref, sc_ref, o_ref, *, norm):
    for j in range(DFT_GROUP):
        bb = b_ref[0, j].astype(BF16)
        cb = jnp.dot(c2_ref[...], bb, preferred_element_type=F32)
        sb = jnp.dot(s2_ref[...], bb, preferred_element_type=F32)
        gr = cb[:, :F_WIDTH] + sb[:, F_WIDTH:]
        gi = cb[:, F_WIDTH:] - sb[:, :F_WIDTH]
        for g, z in enumerate(_channel_dft(gr, gi, cc_ref, sc_ref, norm)):
            o_ref[0, :, j, g * LANES:(g + 1) * LANES] = z


def _dft_ctx_kernel(f_ref, c_ref, s_ref, cc_ref, sc_ref, z_any, o_ref, *, norm):
    del z_any
    f = f_ref[0].astype(BF16)
    gr = jnp.dot(c_ref[...], f, preferred_element_type=F32)
    gi = -jnp.dot(s_ref[...], f, preferred_element_type=F32)
    for g, z in enumerate(_channel_dft(gr, gi, cc_ref, sc_ref, norm)):
        o_ref[0, :, g * LANES:(g + 1) * LANES] = z


def _dft_mats(n):
    idx = np.arange(n, dtype=np.float64)
    ang = 2.0 * np.pi * np.outer(idx, idx) / n
    return jnp.asarray(np.cos(ang), BF16), jnp.asarray(np.sin(ang), BF16)


def _fourier_call(f, n_lat_tok):
    bsz, n_tok, _ = f.shape
    n1 = n_lat_tok // DFT2
    full = lambda *shape: pl.BlockSpec(shape, lambda *_: (0,) * len(shape))
    c1, s1 = _dft_mats(n1)
    c2, s2 = _dft_mats(DFT2)
    cc, sc = _dft_mats(F_GROUP_DIM)
    cctx, sctx = _dft_mats(CTX_LEN)
    tw_ang = 2.0 * np.pi * np.outer(np.arange(DFT2, dtype=np.float64),
                                    np.arange(n1, dtype=np.float64)) / n_lat_tok
    twc = jnp.asarray(np.broadcast_to(np.cos(tw_ang)[:, :, None], (DFT2, n1, LANES)), F32)
    tws = jnp.asarray(np.broadcast_to(np.sin(tw_ang)[:, :, None], (DFT2, n1, LANES)), F32)

    fv = f.reshape(bsz, n_tok // DFT2, DFT2, F_WIDTH)
    mid = pl.pallas_call(
        _dft1_kernel,
        grid=(bsz, DFT2 // DFT_GROUP),
        in_specs=[
            pl.BlockSpec((1, n1, DFT_GROUP, F_WIDTH), lambda b, j: (b, 0, j, 0)),
            full(n1, n1), full(n1, n1),
            pl.BlockSpec((DFT_GROUP, n1, LANES), lambda b, j: (j, 0, 0)),
            pl.BlockSpec((DFT_GROUP, n1, LANES), lambda b, j: (j, 0, 0)),
        ],
        out_specs=pl.BlockSpec((1, n1, DFT_GROUP, 2 * F_WIDTH), lambda b, j: (b, 0, j, 0)),
        out_shape=jax.ShapeDtypeStruct((bsz, n1, DFT2, 2 * F_WIDTH), F32),
        compiler_params=_cparams(2),
        name="dft_outer",
    )(fv, c1, s1, twc, tws)

    z = pl.pallas_call(
        functools.partial(_dft2_kernel, norm=1.0 / math.sqrt(n_lat_tok * F_GROUP_DIM)),
        grid=(bsz, n1 // DFT_GROUP),
        in_specs=[
            pl.BlockSpec((1, DFT_GROUP, DFT2, 2 * F_WIDTH), lambda b, j: (b, j, 0, 0)),
            full(DFT2, DFT2), full(DFT2, DFT2),
            full(F_GROUP_DIM, F_GROUP_DIM), full(F_GROUP_DIM, F_GROUP_DIM),
        ],
        out_specs=pl.BlockSpec((1, DFT2, DFT_GROUP, F_WIDTH), lambda b, j: (b, 0, j, 0)),
        out_shape=jax.ShapeDtypeStruct((bsz, n_tok // n1, n1, F_WIDTH), F32),
        compiler_params=_cparams(2),
        name="dft_inner",
    )(mid, c2, s2, cc, sc)
    z = z.reshape(bsz, n_tok, F_WIDTH)

    ctx_blk = n_lat_tok // CTX_LEN
    z = pl.pallas_call(
        functools.partial(_dft_ctx_kernel, norm=1.0 / math.sqrt(CTX_LEN * F_GROUP_DIM)),
        grid=(bsz,),
        in_specs=[
            pl.BlockSpec((1, CTX_LEN, F_WIDTH), lambda b: (b, ctx_blk, 0)),
            full(CTX_LEN, CTX_LEN), full(CTX_LEN, CTX_LEN),
            full(F_GROUP_DIM, F_GROUP_DIM), full(F_GROUP_DIM, F_GROUP_DIM),
            pl.BlockSpec(memory_space=pl.ANY),
        ],
        out_specs=pl.BlockSpec((1, CTX_LEN, F_WIDTH), lambda b: (b, ctx_blk, 0)),
        out_shape=jax.ShapeDtypeStruct((bsz, n_tok, F_WIDTH), F32),
        input_output_aliases={5: 0},
        compiler_params=_cparams(1),
        name="dft_ctx",
    )(f, cctx, sctx, cc, sc, z)
    return z


def _merge_kernel(x_ref, o_ref, z_ref, mod_ref, g1_ref, g2_ref, wgate_ref, bgate_ref,
                  wao_ref, wfo_ref, wout_ref, wr_ref, br_ref, xo_ref, h2_ref, gates_ref):
    x = x_ref[0]
    mod = mod_ref[0]
    hb = _rms_mod(x, g1_ref[...], mod[0:1], mod[1:2]).astype(BF16)
    gates = jax.nn.sigmoid(jnp.dot(hb, wgate_ref[...], preferred_element_type=F32) + bgate_ref[...])
    ya = jnp.dot(o_ref[0], wao_ref[...], preferred_element_type=F32)
    yf = jnp.dot(z_ref[0].astype(BF16), wfo_ref[...], preferred_element_type=F32)
    merged = gates[:, :D_MODEL] * ya + gates[:, D_MODEL:] * yf
    out = jnp.dot(merged.astype(BF16), wout_ref[...], preferred_element_type=F32)
    xn = x + mod[2:3] * out
    xo_ref[0] = xn
    h2 = _rms_mod(xn, g2_ref[...], mod[3:4], mod[4:5])
    h2_ref[0] = h2.astype(BF16)

    logits = jnp.dot(h2, wr_ref[...], preferred_element_type=F32, precision=HIGHEST)
    aff = jax.nn.sigmoid(logits)
    lane = lax.broadcasted_iota(jnp.int32, (TM, LANES), 1)
    lanef = lane.astype(F32)
    neg = jnp.full((TM, LANES), -jnp.inf, F32)
    sel = jnp.where(lane < N_EXPERTS, aff + br_ref[...], neg)
    grp = lane // EXPERTS_PER_GROUP
    best = i1b = i2b = None
    for g in range(N_GROUPS):
        sg = jnp.where(grp == g, sel, neg)
        m1 = jnp.max(sg, axis=1, keepdims=True)
        i1 = jnp.min(jnp.where(sg == m1, lanef, float(LANES)), axis=1, keepdims=True)
        sg2 = jnp.where(lanef == i1, neg, sg)
        m2 = jnp.max(sg2, axis=1, keepdims=True)
        i2 = jnp.min(jnp.where(sg2 == m2, lanef, float(LANES)), axis=1, keepdims=True)
        score = m1 + m2
        if g == 0:
            best, i1b, i2b = score, i1, i2
        else:
            upd = score > best
            best = jnp.where(upd, score, best)
            i1b = jnp.where(upd, i1, i1b)
            i2b = jnp.where(upd, i2, i2b)
    hit1 = lanef == i1b
    hit2 = lanef == i2b
    a1 = jnp.sum(jnp.where(hit1, aff, 0.0), axis=1, keepdims=True)
    a2 = jnp.sum(jnp.where(hit2, aff, 0.0), axis=1, keepdims=True)
    den = a1 + a2
    gates_ref[0] = jnp.where(hit1, a1 / den, 0.0) + jnp.where(hit2, a2 / den, 0.0)


def _merge_call(x, o, z, mod, g1, g2, w_gate, b_gate, w_ao, w_fo, w_out, w_r, b_r, n_batch):
    bsz, n_tok, _ = x.shape
    nt = n_tok // TM
    tok = lambda w: pl.BlockSpec((1, TM, w), lambda b, t: (b, t, 0))
    full = lambda *shape: pl.BlockSpec(shape, lambda b, t: (0,) * len(shape))
    return pl.pallas_call(
        _merge_kernel,
        grid=(bsz, nt),
        in_specs=[
            tok(D_MODEL), tok(V_WIDTH), tok(F_WIDTH),
            pl.BlockSpec((1, N_MOD, D_MODEL),
                         lambda b, t: (jnp.where(t == nt - 1, n_batch, b), 0, 0)),
            full(1, D_MODEL), full(1, D_MODEL),
            full(D_MODEL, 2 * D_MODEL), full(1, 2 * D_MODEL),
            full(V_WIDTH, D_MODEL), full(F_WIDTH, D_MODEL), full(D_MODEL, D_MODEL),
            full(D_MODEL, LANES), full(1, LANES),
        ],
        out_specs=[tok(D_MODEL), tok(D_MODEL), tok(LANES)],
        out_shape=[
            jax.ShapeDtypeStruct((bsz, n_tok, D_MODEL), F32),
            jax.ShapeDtypeStruct((bsz, n_tok, D_MODEL), BF16),
            jax.ShapeDtypeStruct((bsz, n_tok, LANES), F32),
        ],
        compiler_params=_cparams(2),
        name="merge_route",
    )(x, o, z, mod, g1, g2, w_gate, b_gate, w_ao, w_fo, w_out, w_r, b_r)


def _moe_kernel(h_ref, g_ref, wg_ref, wu_ref, wd_ref, y_ref):
    e = pl.program_id(1)

    @pl.when(e == 0)
    def _():
        y_ref[...] = jnp.zeros(y_ref.shape, F32)

    h = h_ref[...]
    a = jnp.dot(h, wg_ref[0], preferred_element_type=F32)
    u = jnp.dot(h, wu_ref[0], preferred_element_type=F32)
    lane = lax.broadcasted_iota(jnp.int32, (TMOE, LANES), 1)
    ge = jnp.sum(jnp.where(lane == e, g_ref[...], 0.0), axis=1, keepdims=True)
    he = (a * jax.nn.sigmoid(a) * u * ge).astype(BF16)
    y_ref[...] += jnp.dot(he, wd_ref[0], preferred_element_type=F32)


def _moe_call(h2, gates, w_eg, w_eu, w_ed):
    n = h2.shape[0]
    return pl.pallas_call(
        _moe_kernel,
        grid=(n // TMOE, N_EXPERTS),
        in_specs=[
            pl.BlockSpec((TMOE, D_MODEL), lambda t, e: (t, 0)),
            pl.BlockSpec((TMOE, LANES), lambda t, e: (t, 0)),
            pl.BlockSpec((1, D_MODEL, D_EXPERT), lambda t, e: (e, 0, 0)),
            pl.BlockSpec((1, D_MODEL, D_EXPERT), lambda t, e: (e, 0, 0)),
            pl.BlockSpec((1, D_EXPERT, D_MODEL), lambda t, e: (e, 0, 0)),
        ],
        out_specs=pl.BlockSpec((TMOE, D_MODEL), lambda t, e: (t, 0)),
        out_shape=jax.ShapeDtypeStruct((n, D_MODEL), F32),
        compiler_params=_cparams(2),
        name="experts",
    )(h2, gates, w_eg, w_eu, w_ed)


def _final_kernel(x_ref, y_ref, mod_ref, g_ref, o_ref):
    x = x_ref[0] + mod_ref[0][5:6] * y_ref[0]
    o_ref[0] = x * lax.rsqrt(jnp.mean(x * x, axis=-1, keepdims=True) + EPS) * g_ref[...]


def _final_call(x, y, mod, g_final, n_lat_tok):
    bsz = x.shape[0]
    tok = pl.BlockSpec((1, TM, D_MODEL), lambda b, t: (b, t, 0))
    return pl.pallas_call(
        _final_kernel,
        grid=(bsz, n_lat_tok // TM),
        in_specs=[tok, tok,
                  pl.BlockSpec((1, N_MOD, D_MODEL), lambda b, t: (b, 0, 0)),
                  pl.BlockSpec((1, D_MODEL), lambda b, t: (0, 0))],
        out_specs=tok,
        out_shape=jax.ShapeDtypeStruct((bsz, n_lat_tok, D_MODEL), F32),
        compiler_params=_cparams(2),
        name="final_norm",
    )(x, y, mod, g_final)


def _rope_tables(n_lat_tok):
    t = np.arange(n_lat_tok)
    pos = np.stack([t // GRID_W, t % GRID_W], axis=0).astype(np.float64)
    inv = ROPE_THETA ** (-np.arange(ROPE_HALF, dtype=np.float64) / ROPE_HALF)
    d = np.arange(LANES) % HEAD_DIM
    axis = d // (2 * ROPE_HALF)
    freq = d % ROPE_HALF
    ang = pos[axis, :].T * inv[freq][None, :]
    sign = np.where((d % (2 * ROPE_HALF)) < ROPE_HALF, -1.0, 1.0)
    cos = np.concatenate([np.cos(ang), np.ones((CTX_LEN, LANES))], axis=0)
    sin = np.concatenate([np.sin(ang) * sign[None, :], np.zeros((CTX_LEN, LANES))], axis=0)
    return jnp.asarray(cos, F32), jnp.asarray(sin, F32)


def kernel(x, c, ctx, c_ctx, w_ada, b_ada, g_norm1, w_in, lam_qk, g_subln, w_attn_out, w_four_out,
           w_gate, b_gate, w_out, g_norm2, w_router, b_router, w_e_gate, w_e_up, w_e_down, g_final):
    bsz, n_lat_tok, d = x.shape
    depth = w_ada.shape[0]
    assert d == D_MODEL and ctx.shape[1] == CTX_LEN and bsz < MOD_ROWS
    assert n_lat_tok % TQ == 0 and n_lat_tok % (DFT2 * 8) == 0
    n_tok = n_lat_tok + CTX_LEN
    assert (bsz * n_tok) % TMOE == 0

    cond = jnp.zeros((MOD_ROWS, D_MODEL), F32).at[:bsz].set(c).at[bsz].set(c_ctx)
    mods = _ada_call(cond, w_ada, b_ada).reshape(depth, MOD_ROWS, N_MOD, D_MODEL)
    cos_t, sin_t = _rope_tables(n_lat_tok)
    w_r = jnp.zeros((D_MODEL, LANES), F32).at[:, :N_EXPERTS].set(w_router)
    b_r = jnp.zeros((1, LANES), F32).at[0, :N_EXPERTS].set(b_router)

    xs = jnp.concatenate([x, ctx], axis=1)
    y = None
    for i in range(depth):
        lam_init = 0.8 - 0.6 * math.exp(-0.3 * i)
        xs, q, k, v, f = _proj_call(
            xs, y, mods[i - 1] if i else None, mods[i], g_norm1[i].reshape(1, D_MODEL),
            w_in[i].astype(BF16), cos_t, sin_t, bsz)
        o = _attn_call(q, k, v, lam_qk[i], g_subln[i], lam_init)
        z = _fourier_call(f, n_lat_tok)
        xs, h2, gates = _merge_call(
            xs, o, z, mods[i], g_norm1[i].reshape(1, D_MODEL), g_norm2[i].reshape(1, D_MODEL),
            w_gate[i].astype(BF16), b_gate[i].reshape(1, 2 * D_MODEL), w_attn_out[i].astype(BF16),
            w_four_out[i].astype(BF16), w_out[i].astype(BF16), w_r, b_r, bsz)
        y = _moe_call(h2.reshape(bsz * n_tok, D_MODEL), gates.reshape(bsz * n_tok, LANES),
                      w_e_gate[i].astype(BF16), w_e_up[i].astype(BF16), w_e_down[i].astype(BF16))
        y = y.reshape(bsz, n_tok, D_MODEL)
    return _final_call(xs, y, mods[depth - 1], g_final.reshape(1, D_MODEL), n_lat_tok)
```

```python
import functools
import math

import numpy as np
import jax
import jax.numpy as jnp
from jax import lax
from jax.experimental import pallas as pl
from jax.experimental.pallas import tpu as pltpu

F32 = jnp.float32
BF16 = jnp.bfloat16
HIGHEST = lax.Precision.HIGHEST

D_MODEL = 1024
GRID_W = 64
CTX_LEN = 256
N_HEADS = 8
HEAD_DIM = 64
V_DIM = 128
QK_WIDTH = 1024
V_WIDTH = 1024
F_GROUPS = 4
F_WIDTH = 512
F_GROUP_DIM = 128
IN_WIDTH = 3584
ROPE_THETA = 10000.0
ROPE_HALF = 16
N_EXPERTS = 16
N_GROUPS = 4
EXPERTS_PER_GROUP = 4
D_EXPERT = 512
EPS = 1e-6
N_MOD = 6
MOD_ROWS = 8

LANES = 128
TM = 256
TQ = 512
MXU_TILE = 256
BF16_ROWS = 16
SUM_ROWS = 16
MAX_KEY_TILES = 8
QK_ROWS = 128
LAG_LIMIT = 100.0
TMOE = 640
ADA_TN = 512
DFT2 = 128
DFT_GROUP = 8
Q_SCALE = (HEAD_DIM ** -0.5) * math.log2(math.e)
VMEM_LIMIT = 56 * 1024 * 1024


def _cparams(n_axes):
    return pltpu.CompilerParams(dimension_semantics=("arbitrary",) * n_axes,
                                vmem_limit_bytes=VMEM_LIMIT)


def _rms_mod(x, g, shift, scale):
    y = x * lax.rsqrt(jnp.mean(x * x, axis=-1, keepdims=True) + EPS) * g
    return y * (1.0 + scale) + shift


def _ada_kernel(c_ref, w_ref, b_ref, o_ref):
    c = c_ref[...]
    s = c * jax.nn.sigmoid(c)
    o_ref[0] = jnp.dot(s, w_ref[0], preferred_element_type=F32, precision=HIGHEST) + b_ref[0]


def _ada_call(cond, w_ada, b_ada):
    depth = w_ada.shape[0]
    n_out = w_ada.shape[2]
    return pl.pallas_call(
        _ada_kernel,
        grid=(depth, n_out // ADA_TN),
        in_specs=[
            pl.BlockSpec((MOD_ROWS, D_MODEL), lambda l, j: (0, 0)),
            pl.BlockSpec((1, D_MODEL, ADA_TN), lambda l, j: (l, 0, j)),
            pl.BlockSpec((1, 1, ADA_TN), lambda l, j: (l, 0, j)),
        ],
        out_specs=pl.BlockSpec((1, MOD_ROWS, ADA_TN), lambda l, j: (l, 0, j)),
        out_shape=jax.ShapeDtypeStruct((depth, MOD_ROWS, n_out), F32),
        compiler_params=_cparams(2),
        name="ada",
    )(cond, w_ada, b_ada.reshape(depth, 1, n_out))


def _proj_kernel(*refs, has_y):
    if has_y:
        (x_ref, y_ref, pmod_ref, mod_ref, g1_ref, win_ref, cos_ref, sin_ref,
         xo_ref, q_ref, k_ref, v_ref, f_ref) = refs
        x = x_ref[0] + pmod_ref[0][5:6] * y_ref[0]
        xo_ref[0] = x
    else:
        (x_ref, mod_ref, g1_ref, win_ref, cos_ref, sin_ref,
         q_ref, k_ref, v_ref, f_ref) = refs
        x = x_ref[0]
    mod = mod_ref[0]
    hb = _rms_mod(x, g1_ref[...], mod[0:1], mod[1:2]).astype(BF16)
    cos = cos_ref[...]
    sin = sin_ref[...]
    lane = lax.broadcasted_iota(jnp.int32, (TM, LANES), 1)
    first = (lane % (2 * ROPE_HALF)) < ROPE_HALF

    def rope(xc):
        up = pltpu.roll(xc, LANES - ROPE_HALF, 1)
        dn = pltpu.roll(xc, ROPE_HALF, 1)
        return xc * cos + jnp.where(first, up, dn) * sin

    w2 = 2 * LANES
    for c in range(QK_WIDTH // w2):
        uq = jnp.dot(hb, win_ref[:, c * w2:(c + 1) * w2], preferred_element_type=F32)
        uk = jnp.dot(hb, win_ref[:, QK_WIDTH + c * w2:QK_WIDTH + (c + 1) * w2],
                     preferred_element_type=F32)
        for j in range(2):
            lo = c * w2 + j * LANES
            q_ref[0, :, lo:lo + LANES] = (rope(uq[:, j * LANES:(j + 1) * LANES]) * Q_SCALE).astype(BF16)
            k_ref[0, :, lo:lo + LANES] = rope(uk[:, j * LANES:(j + 1) * LANES]).astype(BF16)
    for c in range(V_WIDTH // w2):
        lo = 2 * QK_WIDTH + c * w2
        v_ref[0, :, c * w2:(c + 1) * w2] = jnp.dot(
            hb, win_ref[:, lo:lo + w2], preferred_element_type=F32).astype(BF16)
    for c in range(F_WIDTH // w2):
        lo = 2 * QK_WIDTH + V_WIDTH + c * w2
        f_ref[0, :, c * w2:(c + 1) * w2] = jnp.dot(
            hb, win_ref[:, lo:lo + w2], preferred_element_type=F32)


def _proj_call(x, y, pmod, mod, g1, w_in, cos_t, sin_t, n_batch):
    bsz, n_tok, _ = x.shape
    nt = n_tok // TM
    has_y = y is not None

    def mod_map(b, t):
        return (jnp.where(t == nt - 1, n_batch, b), 0, 0)

    tok = lambda w: pl.BlockSpec((1, TM, w), lambda b, t: (b, t, 0))
    modspec = pl.BlockSpec((1, N_MOD, D_MODEL), mod_map)
    in_specs = [tok(D_MODEL)]
    args = [x]
    if has_y:
        in_specs += [tok(D_MODEL), modspec]
        args += [y, pmod]
    in_specs += [
        modspec,
        pl.BlockSpec((1, D_MODEL), lambda b, t: (0, 0)),
        pl.BlockSpec((D_MODEL, IN_WIDTH), lambda b, t: (0, 0)),
        pl.BlockSpec((TM, LANES), lambda b, t: (t, 0)),
        pl.BlockSpec((TM, LANES), lambda b, t: (t, 0)),
    ]
    args += [mod, g1, w_in, cos_t, sin_t]
    out_specs = [tok(QK_WIDTH), tok(QK_WIDTH), tok(V_WIDTH), tok(F_WIDTH)]
    out_shape = [
        jax.ShapeDtypeStruct((bsz, n_tok, QK_WIDTH), BF16),
        jax.ShapeDtypeStruct((bsz, n_tok, QK_WIDTH), BF16),
        jax.ShapeDtypeStruct((bsz, n_tok, V_WIDTH), BF16),
        jax.ShapeDtypeStruct((bsz, n_tok, F_WIDTH), F32),
    ]
    if has_y:
        out_specs = [tok(D_MODEL)] + out_specs
        out_shape = [jax.ShapeDtypeStruct((bsz, n_tok, D_MODEL), F32)] + out_shape
    outs = pl.pallas_call(
        functools.partial(_proj_kernel, has_y=has_y),
        grid=(bsz, nt),
        in_specs=in_specs,
        out_specs=out_specs,
        out_shape=out_shape,
        compiler_params=_cparams(2),
        name="proj",
    )(*args)
    if has_y:
        return outs
    return [x] + list(outs)


def _split_maps_t(qt):
    row = lax.broadcasted_iota(jnp.int32, qt.shape, 0)
    zero = jnp.zeros_like(qt)
    return jnp.concatenate([jnp.where(row < HEAD_DIM, qt, zero),
                            jnp.where(row >= HEAD_DIM, qt, zero)], axis=1)


def _ones_rows(n_cols):
    row = lax.broadcasted_iota(jnp.int32, (SUM_ROWS, n_cols), 0)
    return jnp.where(row == 0, 1.0, 0.0).astype(BF16)


def _vt_block(vt, n_keys):
    return jnp.concatenate([vt, _ones_rows(n_keys)], axis=0)


def _softmax_block_t(q2t, kb, vbt_ext, m_prev, acc_prev):
    st = jnp.dot(kb, q2t, preferred_element_type=F32)
    m_new = jnp.maximum(m_prev, jnp.max(st, axis=0, keepdims=True))
    alpha = jnp.exp2(m_prev - m_new)
    pt = jnp.exp2(st - m_new).astype(BF16)
    acc = alpha * acc_prev + jnp.dot(vbt_ext, pt, preferred_element_type=F32)
    return m_new, acc


def _diff_combine_t(acc, lam_ref, gs_ref, lam_init):
    tq = acc.shape[1] // 2
    lp = lam_ref[...]
    lam = (jnp.exp(jnp.sum(lp[0:1] * lp[1:2], axis=1, keepdims=True))
           - jnp.exp(jnp.sum(lp[2:3] * lp[3:4], axis=1, keepdims=True)) + lam_init)
    pv = acc[:V_DIM]
    l = acc[V_DIM:V_DIM + 1]
    ot = pv[:, :tq] / l[:, :tq] - lam * (pv[:, tq:] / l[:, tq:])
    ot = ot * lax.rsqrt(jnp.mean(ot * ot, axis=0, keepdims=True) + EPS) * gs_ref[...]
    return (ot * (1.0 - lam_init)).T.astype(BF16)


def _attn_kernel(lam_ref, gs_ref, q_ref, k_ref, v_ref, o_ref, vt_ref, q2t_ref, m_ref, acc_ref,
                 lag_ref, pt_ref, *, n_lat_tok, tk, lam_init):
    n_tok = n_lat_tok + CTX_LEN

    @pl.when(pl.program_id(2) == 0)
    def _():
        for j in range(n_tok // MXU_TILE):
            sl = slice(j * MXU_TILE, (j + 1) * MXU_TILE)
            vt_ref[:, sl] = v_ref[0, sl, :].astype(F32).T.astype(BF16)

    q2t_ref[...] = _split_maps_t(q_ref[0].astype(F32).T.astype(BF16))

    def reset():
        m_ref[...] = jnp.full(m_ref.shape, -jnp.inf, F32)
        acc_ref[...] = jnp.zeros(acc_ref.shape, F32)

    def exact_chunk(off):
        m_new, acc = _softmax_block_t(
            q2t_ref[...], k_ref[0, pl.ds(off, MXU_TILE), :],
            _vt_block(vt_ref[:, pl.ds(off, MXU_TILE)], MXU_TILE), m_ref[...], acc_ref[...])
        m_ref[...] = m_new
        acc_ref[...] = acc

    def lagged_block(j, carry):
        off = pl.multiple_of(j * tk, tk)
        m_prev = m_ref[...]
        pmax = None
        for r in range(0, tk, QK_ROWS):
            st = jnp.dot(k_ref[0, pl.ds(off + r, QK_ROWS), :], q2t_ref[...],
                         preferred_element_type=F32)
            p = jnp.exp2(st - m_prev).astype(BF16)
            pt_ref[r:r + QK_ROWS, :] = p
            for t in range(0, QK_ROWS, BF16_ROWS):
                tile = p[t:t + BF16_ROWS, :]
                pmax = tile if pmax is None else jnp.maximum(pmax, tile)
        acc = acc_ref[...] + jnp.dot(_vt_block(vt_ref[:, pl.ds(off, tk)], tk), pt_ref[...],
                                     preferred_element_type=F32)
        excess = jnp.log2(jnp.max(pmax.astype(F32), axis=0, keepdims=True))
        m_new = m_prev + jnp.maximum(excess, 0.0)
        acc_ref[...] = acc * jnp.exp2(m_prev - m_new)
        m_ref[...] = m_new
        lag_ref[...] = jnp.maximum(lag_ref[...], excess)
        return carry

    reset()
    lag_ref[...] = jnp.full(lag_ref.shape, -jnp.inf, F32)
    exact_chunk(n_lat_tok)
    lax.fori_loop(0, n_lat_tok // tk, lagged_block, 0)

    @pl.when(jnp.max(lag_ref[...]) > LAG_LIMIT)
    def _():
        reset()
        lax.fori_loop(0, n_tok // MXU_TILE,
                      lambda c, carry: (exact_chunk(pl.multiple_of(c * MXU_TILE, MXU_TILE)), carry)[1], 0)

    o_ref[0] = _diff_combine_t(acc_ref[...], lam_ref, gs_ref, lam_init)


def _attn_ctx_kernel(lam_ref, gs_ref, q_ref, k_ref, v_ref, o_any, o_ref, *, lam_init):
    del o_any
    q2t = _split_maps_t(q_ref[0].astype(F32).T.astype(BF16))
    vbt = _vt_block(v_ref[0].astype(F32).T.astype(BF16), CTX_LEN)
    m0 = jnp.full((1, 2 * CTX_LEN), -jnp.inf, F32)
    acc0 = jnp.zeros((V_DIM + SUM_ROWS, 2 * CTX_LEN), F32)
    _, acc = _softmax_block_t(q2t, k_ref[0], vbt, m0, acc0)
    o_ref[0] = _diff_combine_t(acc, lam_ref, gs_ref, lam_init)


def _key_block(n_tok):
    for mult in range(MAX_KEY_TILES, 0, -1):
        if n_tok % (mult * MXU_TILE) == 0:
            return mult * MXU_TILE
    raise ValueError(n_tok)


def _attn_call(q, k, v, lam_qk, g_subln, lam_init):
    bsz, n_tok, _ = q.shape
    n_lat_tok = n_tok - CTX_LEN
    tk = _key_block(n_lat_tok)
    tq_step = TQ
    n_col = 2 * TQ
    small = [pl.BlockSpec((4, HEAD_DIM), lambda *_: (0, 0)),
             pl.BlockSpec((V_DIM, 1), lambda *_: (0, 0))]
    gs = g_subln.reshape(V_DIM, 1)
    o = pl.pallas_call(
        functools.partial(_attn_kernel, n_lat_tok=n_lat_tok, tk=tk, lam_init=lam_init),
        grid=(bsz, N_HEADS, n_lat_tok // tq_step),
        in_specs=small + [
            pl.BlockSpec((1, tq_step, LANES), lambda b, h, t: (b, t, h)),
            pl.BlockSpec((1, n_tok, LANES), lambda b, h, t: (b, 0, h)),
            pl.BlockSpec((1, n_tok, LANES), lambda b, h, t: (b, 0, h)),
        ],
        out_specs=pl.BlockSpec((1, tq_step, V_DIM), lambda b, h, t: (b, t, h)),
        out_shape=jax.ShapeDtypeStruct((bsz, n_tok, V_WIDTH), BF16),
        scratch_shapes=[
            pltpu.VMEM((V_DIM, n_tok), BF16),
            pltpu.VMEM((LANES, n_col), BF16),
            pltpu.VMEM((1, n_col), F32),
            pltpu.VMEM((V_DIM + SUM_ROWS, n_col), F32),
            pltpu.VMEM((1, n_col), F32),
            pltpu.VMEM((tk, n_col), BF16),
        ],
        compiler_params=_cparams(3),
        name="diff_attn",
    )(lam_qk, gs, q, k, v)
    ctx_blk = n_lat_tok // CTX_LEN
    ctx_spec = pl.BlockSpec((1, CTX_LEN, LANES), lambda b, h: (b, ctx_blk, h))
    return pl.pallas_call(
        functools.partial(_attn_ctx_kernel, lam_init=lam_init),
        grid=(bsz, N_HEADS),
        in_specs=small + [ctx_spec, ctx_spec, ctx_spec, pl.BlockSpec(memory_space=pl.ANY)],
        out_specs=ctx_spec,
        out_shape=jax.ShapeDtypeStruct((bsz, n_tok, V_WIDTH), BF16),
        input_output_aliases={5: 0},
        compiler_params=_cparams(2),
        name="diff_attn_ctx",
    )(lam_qk, gs, q, k, v, o)


def _dft1_kernel(u_ref, c1_ref, s1_ref, twc_ref, tws_ref, o_ref):
    for j in range(DFT_GROUP):
        u = u_ref[0, :, j, :].astype(BF16)
        ar = jnp.dot(c1_ref[...], u, preferred_element_type=F32)
        ai = -jnp.dot(s1_ref[...], u, preferred_element_type=F32)
        twc = twc_ref[j]
        tws = tws_ref[j]
        for g in range(F_GROUPS):
            sl = slice(g * LANES, (g + 1) * LANES)
            o_ref[0, :, j, g * LANES:(g + 1) * LANES] = ar[:, sl] * twc + ai[:, sl] * tws
            o_ref[0, :, j, F_WIDTH + g * LANES:F_WIDTH + (g + 1) * LANES] = ai[:, sl] * twc - ar[:, sl] * tws


def _channel_dft(gr, gi, cc_ref, sc_ref, norm):
    grb = gr.astype(BF16)
    gib = gi.astype(BF16)
    out = []
    for g in range(F_GROUPS):
        sl = slice(g * LANES, (g + 1) * LANES)
        out.append((jnp.dot(grb[:, sl], cc_ref[...], preferred_element_type=F32)
                    + jnp.dot(gib[:, sl], sc_ref[...], preferred_element_type=F32)) * norm)
    return out


def _dft2_kernel(b_ref, c2_ref, s2_ref, cc_ref, sc_ref, o_ref, *, norm):
    for j in range(DFT_GROUP):
        bb = b_ref[0, j].astype(BF16)
        cb = jnp.dot(c2_ref[...], bb, preferred_element_type=F32)
        sb = jnp.dot(s2_ref[...], bb, preferred_element_type=F32)
        gr = cb[:, :F_WIDTH] + sb[:, F_WIDTH:]
        gi = cb[:, F_WIDTH:] - sb[:, :F_WIDTH]
        for g, z in enumerate(_channel_dft(gr, gi, cc_ref, sc_ref, norm)):
            o_ref[0, :, j, g * LANES:(g + 1) * LANES] = z


def _dft_ctx_kernel(f_ref, c_ref, s_ref, cc_ref, sc_ref, z_any, o_ref, *, norm):
    del z_any
    f = f_ref[0].astype(BF16)
    gr = jnp.dot(c_ref[...], f, preferred_element_type=F32)
    gi = -jnp.dot(s_ref[...], f, preferred_element_type=F32)
    for g, z in enumerate(_channel_dft(gr, gi, cc_ref, sc_ref, norm)):
        o_ref[0, :, g * LANES:(g + 1) * LANES] = z


def _dft_mats(n):
    idx = np.arange(n, dtype=np.float64)
    ang = 2.0 * np.pi * np.outer(idx, idx) / n
    return jnp.asarray(np.cos(ang), BF16), jnp.asarray(np.sin(ang), BF16)


def _fourier_call(f, n_lat_tok):
    bsz, n_tok, _ = f.shape
    n1 = n_lat_tok // DFT2
    full = lambda *shape: pl.BlockSpec(shape, lambda *_: (0,) * len(shape))
    c1, s1 = _dft_mats(n1)
    c2, s2 = _dft_mats(DFT2)
    cc, sc = _dft_mats(F_GROUP_DIM)
    cctx, sctx = _dft_mats(CTX_LEN)
    tw_ang = 2.0 * np.pi * np.outer(np.arange(DFT2, dtype=np.float64),
                                    np.arange(n1, dtype=np.float64)) / n_lat_tok
    twc = jnp.asarray(np.broadcast_to(np.cos(tw_ang)[:, :, None], (DFT2, n1, LANES)), F32)
    tws = jnp.asarray(np.broadcast_to(np.sin(tw_ang)[:, :, None], (DFT2, n1, LANES)), F32)

    fv = f.reshape(bsz, n_tok // DFT2, DFT2, F_WIDTH)
    mid = pl.pallas_call(
        _dft1_kernel,
        grid=(bsz, DFT2 // DFT_GROUP),
        in_specs=[
            pl.BlockSpec((1, n1, DFT_GROUP, F_WIDTH), lambda b, j: (b, 0, j, 0)),
            full(n1, n1), full(n1, n1),
            pl.BlockSpec((DFT_GROUP, n1, LANES), lambda b, j: (j, 0, 0)),
            pl.BlockSpec((DFT_GROUP, n1, LANES), lambda b, j: (j, 0, 0)),
        ],
        out_specs=pl.BlockSpec((1, n1, DFT_GROUP, 2 * F_WIDTH), lambda b, j: (b, 0, j, 0)),
        out_shape=jax.ShapeDtypeStruct((bsz, n1, DFT2, 2 * F_WIDTH), F32),
        compiler_params=_cparams(2),
        name="dft_outer",
    )(fv, c1, s1, twc, tws)

    z = pl.pallas_call(
        functools.partial(_dft2_kernel, norm=1.0 / math.sqrt(n_lat_tok * F_GROUP_DIM)),
        grid=(bsz, n1 // DFT_GROUP),
        in_specs=[
            pl.BlockSpec((1, DFT_GROUP, DFT2, 2 * F_WIDTH), lambda b, j: (b, j, 0, 0)),
            full(DFT2, DFT2), full(DFT2, DFT2),
            full(F_GROUP_DIM, F_GROUP_DIM), full(F_GROUP_DIM, F_GROUP_DIM),
        ],
        out_specs=pl.BlockSpec((1, DFT2, DFT_GROUP, F_WIDTH), lambda b, j: (b, 0, j, 0)),
        out_shape=jax.ShapeDtypeStruct((bsz, n_tok // n1, n1, F_WIDTH), F32),
        compiler_params=_cparams(2),
        name="dft_inner",
    )(mid, c2, s2, cc, sc)
    z = z.reshape(bsz, n_tok, F_WIDTH)

    ctx_blk = n_lat_tok // CTX_LEN
    z = pl.pallas_call(
        functools.partial(_dft_ctx_kernel, norm=1.0 / math.sqrt(CTX_LEN * F_GROUP_DIM)),
        grid=(bsz,),
        in_specs=[
            pl.BlockSpec((1, CTX_LEN, F_WIDTH), lambda b: (b, ctx_blk, 0)),
            full(CTX_LEN, CTX_LEN), full(CTX_LEN, CTX_LEN),
            full(F_GROUP_DIM, F_GROUP_DIM), full(F_GROUP_DIM, F_GROUP_DIM),
            pl.BlockSpec(memory_space=pl.ANY),
        ],
        out_specs=pl.BlockSpec((1, CTX_LEN, F_WIDTH), lambda b: (b, ctx_blk, 0)),
        out_shape=jax.ShapeDtypeStruct((bsz, n_tok, F_WIDTH), F32),
        input_output_aliases={5: 0},
        compiler_params=_cparams(1),
        name="dft_ctx",
    )(f, cctx, sctx, cc, sc, z)
    return z


def _merge_kernel(x_ref, o_ref, z_ref, mod_ref, g1_ref, g2_ref, wgate_ref, bgate_ref,
                  wao_ref, wfo_ref, wout_ref, wr_ref, br_ref, xo_ref, h2_ref, gates_ref):
    x = x_ref[0]
    mod = mod_ref[0]
    hb = _rms_mod(x, g1_ref[...], mod[0:1], mod[1:2]).astype(BF16)
    gates = jax.nn.sigmoid(jnp.dot(hb, wgate_ref[...], preferred_element_type=F32) + bgate_ref[...])
    ya = jnp.dot(o_ref[0], wao_ref[...], preferred_element_type=F32)
    yf = jnp.dot(z_ref[0].astype(BF16), wfo_ref[...], preferred_element_type=F32)
    merged = gates[:, :D_MODEL] * ya + gates[:, D_MODEL:] * yf
    out = jnp.dot(merged.astype(BF16), wout_ref[...], preferred_element_type=F32)
    xn = x + mod[2:3] * out
    xo_ref[0] = xn
    h2 = _rms_mod(xn, g2_ref[...], mod[3:4], mod[4:5])
    h2_ref[0] = h2.astype(BF16)

    logits = jnp.dot(h2, wr_ref[...], preferred_element_type=F32, precision=HIGHEST)
    aff = jax.nn.sigmoid(logits)
    lane = lax.broadcasted_iota(jnp.int32, (TM, LANES), 1)
    lanef = lane.astype(F32)
    neg = jnp.full((TM, LANES), -jnp.inf, F32)
    sel = jnp.where(lane < N_EXPERTS, aff + br_ref[...], neg)
    grp = lane // EXPERTS_PER_GROUP
    best = i1b = i2b = None
    for g in range(N_GROUPS):
        sg = jnp.where(grp == g, sel, neg)
        m1 = jnp.max(sg, axis=1, keepdims=True)
        i1 = jnp.min(jnp.where(sg == m1, lanef, float(LANES)), axis=1, keepdims=True)
        sg2 = jnp.where(lanef == i1, neg, sg)
        m2 = jnp.max(sg2, axis=1, keepdims=True)
        i2 = jnp.min(jnp.where(sg2 == m2, lanef, float(LANES)), axis=1, keepdims=True)
        score = m1 + m2
        if g == 0:
            best, i1b, i2b = score, i1, i2
        else:
            upd = score > best
            best = jnp.where(upd, score, best)
            i1b = jnp.where(upd, i1, i1b)
            i2b = jnp.where(upd, i2, i2b)
    hit1 = lanef == i1b
    hit2 = lanef == i2b
    a1 = jnp.sum(jnp.where(hit1, aff, 0.0), axis=1, keepdims=True)
    a2 = jnp.sum(jnp.where(hit2, aff, 0.0), axis=1, keepdims=True)
    den = a1 + a2
    gates_ref[0] = jnp.where(hit1, a1 / den, 0.0) + jnp.where(hit2, a2 / den, 0.0)


def _merge_call(x, o, z, mod, g1, g2, w_gate, b_gate, w_ao, w_fo, w_out, w_r, b_r, n_batch):
    bsz, n_tok, _ = x.shape
    nt = n_tok // TM
    tok = lambda w: pl.BlockSpec((1, TM, w), lambda b, t: (b, t, 0))
    full = lambda *shape: pl.BlockSpec(shape, lambda b, t: (0,) * len(shape))
    return pl.pallas_call(
        _merge_kernel,
        grid=(bsz, nt),
        in_specs=[
            tok(D_MODEL), tok(V_WIDTH), tok(F_WIDTH),
            pl.BlockSpec((1, N_MOD, D_MODEL),
                         lambda b, t: (jnp.where(t == nt - 1, n_batch, b), 0, 0)),
            full(1, D_MODEL), full(1, D_MODEL),
            full(D_MODEL, 2 * D_MODEL), full(1, 2 * D_MODEL),
            full(V_WIDTH, D_MODEL), full(F_WIDTH, D_MODEL), full(D_MODEL, D_MODEL),
            full(D_MODEL, LANES), full(1, LANES),
        ],
        out_specs=[tok(D_MODEL), tok(D_MODEL), tok(LANES)],
        out_shape=[
            jax.ShapeDtypeStruct((bsz, n_tok, D_MODEL), F32),
            jax.ShapeDtypeStruct((bsz, n_tok, D_MODEL), BF16),
            jax.ShapeDtypeStruct((bsz, n_tok, LANES), F32),
        ],
        compiler_params=_cparams(2),
        name="merge_route",
    )(x, o, z, mod, g1, g2, w_gate, b_gate, w_ao, w_fo, w_out, w_r, b_r)


def _moe_kernel(h_ref, g_ref, wg_ref, wu_ref, wd_ref, y_ref):
    e = pl.program_id(1)

    @pl.when(e == 0)
    def _():
        y_ref[...] = jnp.zeros(y_ref.shape, F32)

    h = h_ref[...]
    a = jnp.dot(h, wg_ref[0], preferred_element_type=F32)
    u = jnp.dot(h, wu_ref[0], preferred_element_type=F32)
    lane = lax.broadcasted_iota(jnp.int32, (TMOE, LANES), 1)
    ge = jnp.sum(jnp.where(lane == e, g_ref[...], 0.0), axis=1, keepdims=True)
    he = (a * jax.nn.sigmoid(a) * u * ge).astype(BF16)
    y_ref[...] += jnp.dot(he, wd_ref[0], preferred_element_type=F32)


def _moe_call(h2, gates, w_eg, w_eu, w_ed):
    n = h2.shape[0]
    return pl.pallas_call(
        _moe_kernel,
        grid=(n // TMOE, N_EXPERTS),
        in_specs=[
            pl.BlockSpec((TMOE, D_MODEL), lambda t, e: (t, 0)),
            pl.BlockSpec((TMOE, LANES), lambda t, e: (t, 0)),
            pl.BlockSpec((1, D_MODEL, D_EXPERT), lambda t, e: (e, 0, 0)),
            pl.BlockSpec((1, D_MODEL, D_EXPERT), lambda t, e: (e, 0, 0)),
            pl.BlockSpec((1, D_EXPERT, D_MODEL), lambda t, e: (e, 0, 0)),
        ],
        out_specs=pl.BlockSpec((TMOE, D_MODEL), lambda t, e: (t, 0)),
        out_shape=jax.ShapeDtypeStruct((n, D_MODEL), F32),
        compiler_params=_cparams(2),
        name="experts",
    )(h2, gates, w_eg, w_eu, w_ed)


def _final_kernel(x_ref, y_ref, mod_ref, g_ref, o_ref):
    x = x_ref[0] + mod_ref[0][5:6] * y_ref[0]
    o_ref[0] = x * lax.rsqrt(jnp.mean(x * x, axis=-1, keepdims=True) + EPS) * g_ref[...]


def _final_call(x, y, mod, g_final, n_lat_tok):
    bsz = x.shape[0]
    tok = pl.BlockSpec((1, TM, D_MODEL), lambda b, t: (b, t, 0))
    return pl.pallas_call(
        _final_kernel,
        grid=(bsz, n_lat_tok // TM),
        in_specs=[tok, tok,
                  pl.BlockSpec((1, N_MOD, D_MODEL), lambda b, t: (b, 0, 0)),
                  pl.BlockSpec((1, D_MODEL), lambda b, t: (0, 0))],
        out_specs=tok,
        out_shape=jax.ShapeDtypeStruct((bsz, n_lat_tok, D_MODEL), F32),
        compiler_params=_cparams(2),
        name="final_norm",
    )(x, y, mod, g_final)


def _rope_tables(n_lat_tok):
    t = np.arange(n_lat_tok)
    pos = np.stack([t // GRID_W, t % GRID_W], axis=0).astype(np.float64)
    inv = ROPE_THETA ** (-np.arange(ROPE_HALF, dtype=np.float64) / ROPE_HALF)
    d = np.arange(LANES) % HEAD_DIM
    axis = d // (2 * ROPE_HALF)
    freq = d % ROPE_HALF
    ang = pos[axis, :].T * inv[freq][None, :]
    sign = np.where((d % (2 * ROPE_HALF)) < ROPE_HALF, -1.0, 1.0)
    cos = np.concatenate([np.cos(ang), np.ones((CTX_LEN, LANES))], axis=0)
    sin = np.concatenate([np.sin(ang) * sign[None, :], np.zeros((CTX_LEN, LANES))], axis=0)
    return jnp.asarray(cos, F32), jnp.asarray(sin, F32)


def kernel(x, c, ctx, c_ctx, w_ada, b_ada, g_norm1, w_in, lam_qk, g_subln, w_attn_out, w_four_out,
           w_gate, b_gate, w_out, g_norm2, w_router, b_router, w_e_gate, w_e_up, w_e_down, g_final):
    bsz, n_lat_tok, d = x.shape
    depth = w_ada.shape[0]
    assert d == D_MODEL and ctx.shape[1] == CTX_LEN and bsz < MOD_ROWS
    assert n_lat_tok % TQ == 0 and n_lat_tok % (DFT2 * 8) == 0
    n_tok = n_lat_tok + CTX_LEN
    assert (bsz * n_tok) % TMOE == 0

    cond = jnp.zeros((MOD_ROWS, D_MODEL), F32).at[:bsz].set(c).at[bsz].set(c_ctx)
    mods = _ada_call(cond, w_ada, b_ada).reshape(depth, MOD_ROWS, N_MOD, D_MODEL)
    cos_t, sin_t = _rope_tables(n_lat_tok)
    w_r = jnp.zeros((D_MODEL, LANES), F32).at[:, :N_EXPERTS].set(w_router)
    b_r = jnp.zeros((1, LANES), F32).at[0, :N_EXPERTS].set(b_router)

    xs = jnp.concatenate([x, ctx], axis=1)
    y = None
    for i in range(depth):
        lam_init = 0.8 - 0.6 * math.exp(-0.3 * i)
        xs, q, k, v, f = _proj_call(
            xs, y, mods[i - 1] if i else None, mods[i], g_norm1[i].reshape(1, D_MODEL),
            w_in[i].astype(BF16), cos_t, sin_t, bsz)
        o = _attn_call(q, k, v, lam_qk[i], g_subln[i], lam_init)
        z = _fourier_call(f, n_lat_tok)
        xs, h2, gates = _merge_call(
            xs, o, z, mods[i], g_norm1[i].reshape(1, D_MODEL), g_norm2[i].reshape(1, D_MODEL),
            w_gate[i].astype(BF16), b_gate[i].reshape(1, 2 * D_MODEL), w_attn_out[i].astype(BF16),
            w_four_out[i].astype(BF16), w_out[i].astype(BF16), w_r, b_r, bsz)
        y = _moe_call(h2.reshape(bsz * n_tok, D_MODEL), gates.reshape(bsz * n_tok, LANES),
                      w_e_gate[i].astype(BF16), w_e_up[i].astype(BF16), w_e_down[i].astype(BF16))
        y = y.reshape(bsz, n_tok, D_MODEL)
    return _final_call(xs, y, mods[depth - 1], g_final.reshape(1, D_MODEL), n_lat_tok)
```

```python
import functools
import math

import numpy as np
import jax
import jax.numpy as jnp
from jax import lax
from jax.experimental import pallas as pl
from jax.experimental.pallas import tpu as pltpu

F32 = jnp.float32
BF16 = jnp.bfloat16
HIGHEST = lax.Precision.HIGHEST

D_MODEL = 1024
GRID_W = 64
CTX_LEN = 256
N_HEADS = 8
HEAD_DIM = 64
V_DIM = 128
QK_WIDTH = 1024
V_WIDTH = 1024
F_GROUPS = 4
F_WIDTH = 512
F_GROUP_DIM = 128
IN_WIDTH = 3584
ROPE_THETA = 10000.0
ROPE_HALF = 16
N_EXPERTS = 16
N_GROUPS = 4
EXPERTS_PER_GROUP = 4
D_EXPERT = 512
EPS = 1e-6
N_MOD = 6
MOD_ROWS = 8

LANES = 128
TM = 256
TQ = 512
MXU_TILE = 256
BF16_ROWS = 16
SUM_ROWS = 16
MAX_KEY_TILES = 8
QK_ROWS = 128
LAG_LIMIT = 100.0
TMR = 256
DMA_UNROLL = 8
ADA_TN = 512
DFT2 = 128
DFT_GROUP = 8
Q_SCALE = (HEAD_DIM ** -0.5) * math.log2(math.e)
VMEM_LIMIT = 56 * 1024 * 1024


def _cparams(n_axes):
    return pltpu.CompilerParams(dimension_semantics=("arbitrary",) * n_axes,
                                vmem_limit_bytes=VMEM_LIMIT)


def _rms_mod(x, g, shift, scale):
    y = x * lax.rsqrt(jnp.mean(x * x, axis=-1, keepdims=True) + EPS) * g
    return y * (1.0 + scale) + shift


def _ada_kernel(c_ref, w_ref, b_ref, o_ref):
    c = c_ref[...]
    s = c * jax.nn.sigmoid(c)
    o_ref[0] = jnp.dot(s, w_ref[0], preferred_element_type=F32, precision=HIGHEST) + b_ref[0]


def _ada_call(cond, w_ada, b_ada):
    depth = w_ada.shape[0]
    n_out = w_ada.shape[2]
    return pl.pallas_call(
        _ada_kernel,
        grid=(depth, n_out // ADA_TN),
        in_specs=[
            pl.BlockSpec((MOD_ROWS, D_MODEL), lambda l, j: (0, 0)),
            pl.BlockSpec((1, D_MODEL, ADA_TN), lambda l, j: (l, 0, j)),
            pl.BlockSpec((1, 1, ADA_TN), lambda l, j: (l, 0, j)),
        ],
        out_specs=pl.BlockSpec((1, MOD_ROWS, ADA_TN), lambda l, j: (l, 0, j)),
        out_shape=jax.ShapeDtypeStruct((depth, MOD_ROWS, n_out), F32),
        compiler_params=_cparams(2),
        name="ada",
    )(cond, w_ada, b_ada.reshape(depth, 1, n_out))


def _proj_kernel(*refs, has_y):
    if has_y:
        (x_ref, y_ref, pmod_ref, mod_ref, g1_ref, win_ref, cos_ref, sin_ref,
         xo_ref, q_ref, k_ref, v_ref, f_ref) = refs
        x = x_ref[0] + pmod_ref[0][5:6] * y_ref[0]
        xo_ref[0] = x
    else:
        (x_ref, mod_ref, g1_ref, win_ref, cos_ref, sin_ref,
         q_ref, k_ref, v_ref, f_ref) = refs
        x = x_ref[0]
    mod = mod_ref[0]
    hb = _rms_mod(x, g1_ref[...], mod[0:1], mod[1:2]).astype(BF16)
    cos = cos_ref[...]
    sin = sin_ref[...]
    lane = lax.broadcasted_iota(jnp.int32, (TM, LANES), 1)
    first = (lane % (2 * ROPE_HALF)) < ROPE_HALF

    def rope(xc):
        up = pltpu.roll(xc, LANES - ROPE_HALF, 1)
        dn = pltpu.roll(xc, ROPE_HALF, 1)
        return xc * cos + jnp.where(first, up, dn) * sin

    w2 = 2 * LANES
    for c in range(QK_WIDTH // w2):
        uq = jnp.dot(hb, win_ref[:, c * w2:(c + 1) * w2], preferred_element_type=F32)
        uk = jnp.dot(hb, win_ref[:, QK_WIDTH + c * w2:QK_WIDTH + (c + 1) * w2],
                     preferred_element_type=F32)
        for j in range(2):
            lo = c * w2 + j * LANES
            q_ref[0, :, lo:lo + LANES] = (rope(uq[:, j * LANES:(j + 1) * LANES]) * Q_SCALE).astype(BF16)
            k_ref[0, :, lo:lo + LANES] = rope(uk[:, j * LANES:(j + 1) * LANES]).astype(BF16)
    for c in range(V_WIDTH // w2):
        lo = 2 * QK_WIDTH + c * w2
        v_ref[0, :, c * w2:(c + 1) * w2] = jnp.dot(
            hb, win_ref[:, lo:lo + w2], preferred_element_type=F32).astype(BF16)
    for c in range(F_WIDTH // w2):
        lo = 2 * QK_WIDTH + V_WIDTH + c * w2
        f_ref[0, :, c * w2:(c + 1) * w2] = jnp.dot(
            hb, win_ref[:, lo:lo + w2], preferred_element_type=F32)


def _proj_call(x, y, pmod, mod, g1, w_in, cos_t, sin_t, n_batch):
    bsz, n_tok, _ = x.shape
    nt = n_tok // TM
    has_y = y is not None

    def mod_map(b, t):
        return (jnp.where(t == nt - 1, n_batch, b), 0, 0)

    tok = lambda w: pl.BlockSpec((1, TM, w), lambda b, t: (b, t, 0))
    modspec = pl.BlockSpec((1, N_MOD, D_MODEL), mod_map)
    in_specs = [tok(D_MODEL)]
    args = [x]
    if has_y:
        in_specs += [tok(D_MODEL), modspec]
        args += [y, pmod]
    in_specs += [
        modspec,
        pl.BlockSpec((1, D_MODEL), lambda b, t: (0, 0)),
        pl.BlockSpec((D_MODEL, IN_WIDTH), lambda b, t: (0, 0)),
        pl.BlockSpec((TM, LANES), lambda b, t: (t, 0)),
        pl.BlockSpec((TM, LANES), lambda b, t: (t, 0)),
    ]
    args += [mod, g1, w_in, cos_t, sin_t]
    out_specs = [tok(QK_WIDTH), tok(QK_WIDTH), tok(V_WIDTH), tok(F_WIDTH)]
    out_shape = [
        jax.ShapeDtypeStruct((bsz, n_tok, QK_WIDTH), BF16),
        jax.ShapeDtypeStruct((bsz, n_tok, QK_WIDTH), BF16),
        jax.ShapeDtypeStruct((bsz, n_tok, V_WIDTH), BF16),
        jax.ShapeDtypeStruct((bsz, n_tok, F_WIDTH), F32),
    ]
    if has_y:
        out_specs = [tok(D_MODEL)] + out_specs
        out_shape = [jax.ShapeDtypeStruct((bsz, n_tok, D_MODEL), F32)] + out_shape
    outs = pl.pallas_call(
        functools.partial(_proj_kernel, has_y=has_y),
        grid=(bsz, nt),
        in_specs=in_specs,
        out_specs=out_specs,
        out_shape=out_shape,
        compiler_params=_cparams(2),
        name="proj",
    )(*args)
    if has_y:
        return outs
    return [x] + list(outs)


def _split_maps_t(qt):
    row = lax.broadcasted_iota(jnp.int32, qt.shape, 0)
    zero = jnp.zeros_like(qt)
    return jnp.concatenate([jnp.where(row < HEAD_DIM, qt, zero),
                            jnp.where(row >= HEAD_DIM, qt, zero)], axis=1)


def _ones_rows(n_cols):
    row = lax.broadcasted_iota(jnp.int32, (SUM_ROWS, n_cols), 0)
    return jnp.where(row == 0, 1.0, 0.0).astype(BF16)


def _vt_block(vt, n_keys):
    return jnp.concatenate([vt, _ones_rows(n_keys)], axis=0)


def _softmax_block_t(q2t, kb, vbt_ext, m_prev, acc_prev):
    st = jnp.dot(kb, q2t, preferred_element_type=F32)
    m_new = jnp.maximum(m_prev, jnp.max(st, axis=0, keepdims=True))
    alpha = jnp.exp2(m_prev - m_new)
    pt = jnp.exp2(st - m_new).astype(BF16)
    acc = alpha * acc_prev + jnp.dot(vbt_ext, pt, preferred_element_type=F32)
    return m_new, acc


def _diff_combine_t(acc, lam_ref, gs_ref, lam_init):
    tq = acc.shape[1] // 2
    lp = lam_ref[...]
    lam = (jnp.exp(jnp.sum(lp[0:1] * lp[1:2], axis=1, keepdims=True))
           - jnp.exp(jnp.sum(lp[2:3] * lp[3:4], axis=1, keepdims=True)) + lam_init)
    pv = acc[:V_DIM]
    l = acc[V_DIM:V_DIM + 1]
    ot = pv[:, :tq] / l[:, :tq] - lam * (pv[:, tq:] / l[:, tq:])
    ot = ot * lax.rsqrt(jnp.mean(ot * ot, axis=0, keepdims=True) + EPS) * gs_ref[...]
    return (ot * (1.0 - lam_init)).T.astype(BF16)


def _attn_kernel(lam_ref, gs_ref, q_ref, k_ref, v_ref, o_ref, vt_ref, q2t_ref, m_ref, acc_ref,
                 lag_ref, pt_ref, *, n_lat_tok, tk, lam_init):
    n_tok = n_lat_tok + CTX_LEN

    @pl.when(pl.program_id(2) == 0)
    def _():
        for j in range(n_tok // MXU_TILE):
            sl = slice(j * MXU_TILE, (j + 1) * MXU_TILE)
            vt_ref[:, sl] = v_ref[0, sl, :].astype(F32).T.astype(BF16)

    q2t_ref[...] = _split_maps_t(q_ref[0].astype(F32).T.astype(BF16))

    def reset():
        m_ref[...] = jnp.full(m_ref.shape, -jnp.inf, F32)
        acc_ref[...] = jnp.zeros(acc_ref.shape, F32)

    def exact_chunk(off):
        m_new, acc = _softmax_block_t(
            q2t_ref[...], k_ref[0, pl.ds(off, MXU_TILE), :],
            _vt_block(vt_ref[:, pl.ds(off, MXU_TILE)], MXU_TILE), m_ref[...], acc_ref[...])
        m_ref[...] = m_new
        acc_ref[...] = acc

    def lagged_block(j, carry):
        off = pl.multiple_of(j * tk, tk)
        m_prev = m_ref[...]
        pmax = None
        for r in range(0, tk, QK_ROWS):
            st = jnp.dot(k_ref[0, pl.ds(off + r, QK_ROWS), :], q2t_ref[...],
                         preferred_element_type=F32)
            p = jnp.exp2(st - m_prev).astype(BF16)
            pt_ref[r:r + QK_ROWS, :] = p
            for t in range(0, QK_ROWS, BF16_ROWS):
                tile = p[t:t + BF16_ROWS, :]
                pmax = tile if pmax is None else jnp.maximum(pmax, tile)
        acc = acc_ref[...] + jnp.dot(_vt_block(vt_ref[:, pl.ds(off, tk)], tk), pt_ref[...],
                                     preferred_element_type=F32)
        excess = jnp.log2(jnp.max(pmax.astype(F32), axis=0, keepdims=True))
        m_new = m_prev + jnp.maximum(excess, 0.0)
        acc_ref[...] = acc * jnp.exp2(m_prev - m_new)
        m_ref[...] = m_new
        lag_ref[...] = jnp.maximum(lag_ref[...], excess)
        return carry

    reset()
    lag_ref[...] = jnp.full(lag_ref.shape, -jnp.inf, F32)
    exact_chunk(n_lat_tok)
    lax.fori_loop(0, n_lat_tok // tk, lagged_block, 0)

    @pl.when(jnp.max(lag_ref[...]) > LAG_LIMIT)
    def _():
        reset()
        lax.fori_loop(0, n_tok // MXU_TILE,
                      lambda c, carry: (exact_chunk(pl.multiple_of(c * MXU_TILE, MXU_TILE)), carry)[1], 0)

    o_ref[0] = _diff_combine_t(acc_ref[...], lam_ref, gs_ref, lam_init)


def _attn_ctx_kernel(lam_ref, gs_ref, q_ref, k_ref, v_ref, o_any, o_ref, *, lam_init):
    del o_any
    q2t = _split_maps_t(q_ref[0].astype(F32).T.astype(BF16))
    vbt = _vt_block(v_ref[0].astype(F32).T.astype(BF16), CTX_LEN)
    m0 = jnp.full((1, 2 * CTX_LEN), -jnp.inf, F32)
    acc0 = jnp.zeros((V_DIM + SUM_ROWS, 2 * CTX_LEN), F32)
    _, acc = _softmax_block_t(q2t, k_ref[0], vbt, m0, acc0)
    o_ref[0] = _diff_combine_t(acc, lam_ref, gs_ref, lam_init)


def _key_block(n_tok):
    for mult in range(MAX_KEY_TILES, 0, -1):
        if n_tok % (mult * MXU_TILE) == 0:
            return mult * MXU_TILE
    raise ValueError(n_tok)


def _attn_call(q, k, v, lam_qk, g_subln, lam_init):
    bsz, n_tok, _ = q.shape
    n_lat_tok = n_tok - CTX_LEN
    tk = _key_block(n_lat_tok)
    tq_step = TQ
    n_col = 2 * TQ
    small = [pl.BlockSpec((4, HEAD_DIM), lambda *_: (0, 0)),
             pl.BlockSpec((V_DIM, 1), lambda *_: (0, 0))]
    gs = g_subln.reshape(V_DIM, 1)
    o = pl.pallas_call(
        functools.partial(_attn_kernel, n_lat_tok=n_lat_tok, tk=tk, lam_init=lam_init),
        grid=(bsz, N_HEADS, n_lat_tok // tq_step),
        in_specs=small + [
            pl.BlockSpec((1, tq_step, LANES), lambda b, h, t: (b, t, h)),
            pl.BlockSpec((1, n_tok, LANES), lambda b, h, t: (b, 0, h)),
            pl.BlockSpec((1, n_tok, LANES), lambda b, h, t: (b, 0, h)),
        ],
        out_specs=pl.BlockSpec((1, tq_step, V_DIM), lambda b, h, t: (b, t, h)),
        out_shape=jax.ShapeDtypeStruct((bsz, n_tok, V_WIDTH), BF16),
        scratch_shapes=[
            pltpu.VMEM((V_DIM, n_tok), BF16),
            pltpu.VMEM((LANES, n_col), BF16),
            pltpu.VMEM((1, n_col), F32),
            pltpu.VMEM((V_DIM + SUM_ROWS, n_col), F32),
            pltpu.VMEM((1, n_col), F32),
            pltpu.VMEM((tk, n_col), BF16),
        ],
        compiler_params=_cparams(3),
        name="diff_attn",
    )(lam_qk, gs, q, k, v)
    ctx_blk = n_lat_tok // CTX_LEN
    ctx_spec = pl.BlockSpec((1, CTX_LEN, LANES), lambda b, h: (b, ctx_blk, h))
    return pl.pallas_call(
        functools.partial(_attn_ctx_kernel, lam_init=lam_init),
        grid=(bsz, N_HEADS),
        in_specs=small + [ctx_spec, ctx_spec, ctx_spec, pl.BlockSpec(memory_space=pl.ANY)],
        out_specs=ctx_spec,
        out_shape=jax.ShapeDtypeStruct((bsz, n_tok, V_WIDTH), BF16),
        input_output_aliases={5: 0},
        compiler_params=_cparams(2),
        name="diff_attn_ctx",
    )(lam_qk, gs, q, k, v, o)


def _dft1_kernel(u_ref, c1_ref, s1_ref, twc_ref, tws_ref, o_ref):
    for j in range(DFT_GROUP):
        u = u_ref[0, :, j, :].astype(BF16)
        ar = jnp.dot(c1_ref[...], u, preferred_element_type=F32)
        ai = -jnp.dot(s1_ref[...], u, preferred_element_type=F32)
        twc = twc_ref[j]
        tws = tws_ref[j]
        for g in range(F_GROUPS):
            sl = slice(g * LANES, (g + 1) * LANES)
            o_ref[0, :, j, g * LANES:(g + 1) * LANES] = ar[:, sl] * twc + ai[:, sl] * tws
            o_ref[0, :, j, F_WIDTH + g * LANES:F_WIDTH + (g + 1) * LANES] = ai[:, sl] * twc - ar[:, sl] * tws


def _channel_dft(gr, gi, cc_ref, sc_ref, norm):
    grb = gr.astype(BF16)
    gib = gi.astype(BF16)
    out = []
    for g in range(F_GROUPS):
        sl = slice(g * LANES, (g + 1) * LANES)
        out.append((jnp.dot(grb[:, sl], cc_ref[...], preferred_element_type=F32)
                    + jnp.dot(gib[:, sl], sc_ref[...], preferred_element_type=F32)) * norm)
    return out


def _dft2_kernel(b_ref, c2_ref, s2_ref, cc_ref, sc_ref, o_ref, *, norm):
    for j in range(DFT_GROUP):
        bb = b_ref[0, j].astype(BF16)
        cb = jnp.dot(c2_ref[...], bb, preferred_element_type=F32)
        sb = jnp.dot(s2_ref[...], bb, preferred_element_type=F32)
        gr = cb[:, :F_WIDTH] + sb[:, F_WIDTH:]
        gi = cb[:, F_WIDTH:] - sb[:, :F_WIDTH]
        for g, z in enumerate(_channel_dft(gr, gi, cc_ref, sc_ref, norm)):
            o_ref[0, :, j, g * LANES:(g + 1) * LANES] = z


def _dft_ctx_kernel(f_ref, c_ref, s_ref, cc_ref, sc_ref, z_any, o_ref, *, norm):
    del z_any
    f = f_ref[0].astype(BF16)
    gr = jnp.dot(c_ref[...], f, preferred_element_type=F32)
    gi = -jnp.dot(s_ref[...], f, preferred_element_type=F32)
    for g, z in enumerate(_channel_dft(gr, gi, cc_ref, sc_ref, norm)):
        o_ref[0, :, g * LANES:(g + 1) * LANES] = z


def _dft_mats(n):
    idx = np.arange(n, dtype=np.float64)
    ang = 2.0 * np.pi * np.outer(idx, idx) / n
    return jnp.asarray(np.cos(ang), BF16), jnp.asarray(np.sin(ang), BF16)


def _fourier_call(f, n_lat_tok):
    bsz, n_tok, _ = f.shape
    n1 = n_lat_tok // DFT2
    full = lambda *shape: pl.BlockSpec(shape, lambda *_: (0,) * len(shape))
    c1, s1 = _dft_mats(n1)
    c2, s2 = _dft_mats(DFT2)
    cc, sc = _dft_mats(F_GROUP_DIM)
    cctx, sctx = _dft_mats(CTX_LEN)
    tw_ang = 2.0 * np.pi * np.outer(np.arange(DFT2, dtype=np.float64),
                                    np.arange(n1, dtype=np.float64)) / n_lat_tok
    twc = jnp.asarray(np.broadcast_to(np.cos(tw_ang)[:, :, None], (DFT2, n1, LANES)), F32)
    tws = jnp.asarray(np.broadcast_to(np.sin(tw_ang)[:, :, None], (DFT2, n1, LANES)), F32)

    fv = f.reshape(bsz, n_tok // DFT2, DFT2, F_WIDTH)
    mid = pl.pallas_call(
        _dft1_kernel,
        grid=(bsz, DFT2 // DFT_GROUP),
        in_specs=[
            pl.BlockSpec((1, n1, DFT_GROUP, F_WIDTH), lambda b, j: (b, 0, j, 0)),
            full(n1, n1), full(n1, n1),
            pl.BlockSpec((DFT_GROUP, n1, LANES), lambda b, j: (j, 0, 0)),
            pl.BlockSpec((DFT_GROUP, n1, LANES), lambda b, j: (j, 0, 0)),
        ],
        out_specs=pl.BlockSpec((1, n1, DFT_GROUP, 2 * F_WIDTH), lambda b, j: (b, 0, j, 0)),
        out_shape=jax.ShapeDtypeStruct((bsz, n1, DFT2, 2 * F_WIDTH), F32),
        compiler_params=_cparams(2),
        name="dft_outer",
    )(fv, c1, s1, twc, tws)

    z = pl.pallas_call(
        functools.partial(_dft2_kernel, norm=1.0 / math.sqrt(n_lat_tok * F_GROUP_DIM)),
        grid=(bsz, n1 // DFT_GROUP),
        in_specs=[
            pl.BlockSpec((1, DFT_GROUP, DFT2, 2 * F_WIDTH), lambda b, j: (b, j, 0, 0)),
            full(DFT2, DFT2), full(DFT2, DFT2),
            full(F_GROUP_DIM, F_GROUP_DIM), full(F_GROUP_DIM, F_GROUP_DIM),
        ],
        out_specs=pl.BlockSpec((1, DFT2, DFT_GROUP, F_WIDTH), lambda b, j: (b, 0, j, 0)),
        out_shape=jax.ShapeDtypeStruct((bsz, n_tok // n1, n1, F_WIDTH), F32),
        compiler_params=_cparams(2),
        name="dft_inner",
    )(mid, c2, s2, cc, sc)
    z = z.reshape(bsz, n_tok, F_WIDTH)

    ctx_blk = n_lat_tok // CTX_LEN
    z = pl.pallas_call(
        functools.partial(_dft_ctx_kernel, norm=1.0 / math.sqrt(CTX_LEN * F_GROUP_DIM)),
        grid=(bsz,),
        in_specs=[
            pl.BlockSpec((1, CTX_LEN, F_WIDTH), lambda b: (b, ctx_blk, 0)),
            full(CTX_LEN, CTX_LEN), full(CTX_LEN, CTX_LEN),
            full(F_GROUP_DIM, F_GROUP_DIM), full(F_GROUP_DIM, F_GROUP_DIM),
            pl.BlockSpec(memory_space=pl.ANY),
        ],
        out_specs=pl.BlockSpec((1, CTX_LEN, F_WIDTH), lambda b: (b, ctx_blk, 0)),
        out_shape=jax.ShapeDtypeStruct((bsz, n_tok, F_WIDTH), F32),
        input_output_aliases={5: 0},
        compiler_params=_cparams(1),
        name="dft_ctx",
    )(f, cctx, sctx, cc, sc, z)
    return z


def _merge_kernel(x_ref, o_ref, z_ref, mod_ref, g1_ref, g2_ref, wgate_ref, bgate_ref,
                  wao_ref, wfo_ref, wout_ref, wr_ref, br_ref, xo_ref, h2_ref, route_ref):
    x = x_ref[0]
    mod = mod_ref[0]
    hb = _rms_mod(x, g1_ref[...], mod[0:1], mod[1:2]).astype(BF16)
    gates = jax.nn.sigmoid(jnp.dot(hb, wgate_ref[...], preferred_element_type=F32) + bgate_ref[...])
    ya = jnp.dot(o_ref[0], wao_ref[...], preferred_element_type=F32)
    yf = jnp.dot(z_ref[0].astype(BF16), wfo_ref[...], preferred_element_type=F32)
    merged = gates[:, :D_MODEL] * ya + gates[:, D_MODEL:] * yf
    out = jnp.dot(merged.astype(BF16), wout_ref[...], preferred_element_type=F32)
    xn = x + mod[2:3] * out
    xo_ref[0] = xn
    h2 = _rms_mod(xn, g2_ref[...], mod[3:4], mod[4:5])
    h2_ref[0] = h2

    logits = jnp.dot(h2, wr_ref[...], preferred_element_type=F32, precision=HIGHEST)
    aff = jax.nn.sigmoid(logits)
    lane = lax.broadcasted_iota(jnp.int32, (TM, LANES), 1)
    lanef = lane.astype(F32)
    neg = jnp.full((TM, LANES), -jnp.inf, F32)
    sel = jnp.where(lane < N_EXPERTS, aff + br_ref[...], neg)
    grp = lane // EXPERTS_PER_GROUP
    best = i1b = i2b = None
    for g in range(N_GROUPS):
        sg = jnp.where(grp == g, sel, neg)
        m1 = jnp.max(sg, axis=1, keepdims=True)
        i1 = jnp.min(jnp.where(sg == m1, lanef, float(LANES)), axis=1, keepdims=True)
        sg2 = jnp.where(lanef == i1, neg, sg)
        m2 = jnp.max(sg2, axis=1, keepdims=True)
        i2 = jnp.min(jnp.where(sg2 == m2, lanef, float(LANES)), axis=1, keepdims=True)
        score = m1 + m2
        if g == 0:
            best, i1b, i2b = score, i1, i2
        else:
            upd = score > best
            best = jnp.where(upd, score, best)
            i1b = jnp.where(upd, i1, i1b)
            i2b = jnp.where(upd, i2, i2b)
    hit1 = lanef == i1b
    hit2 = lanef == i2b
    a1 = jnp.sum(jnp.where(hit1, aff, 0.0), axis=1, keepdims=True)
    a2 = jnp.sum(jnp.where(hit2, aff, 0.0), axis=1, keepdims=True)
    den = a1 + a2
    route_ref[0] = (jnp.where(lane == 0, i1b, 0.0) + jnp.where(lane == 1, i2b, 0.0)
                    + jnp.where(lane == 2, a1 / den, 0.0) + jnp.where(lane == 3, a2 / den, 0.0))


def _merge_call(x, o, z, mod, g1, g2, w_gate, b_gate, w_ao, w_fo, w_out, w_r, b_r, n_batch):
    bsz, n_tok, _ = x.shape
    nt = n_tok // TM
    tok = lambda w: pl.BlockSpec((1, TM, w), lambda b, t: (b, t, 0))
    full = lambda *shape: pl.BlockSpec(shape, lambda b, t: (0,) * len(shape))
    return pl.pallas_call(
        _merge_kernel,
        grid=(bsz, nt),
        in_specs=[
            tok(D_MODEL), tok(V_WIDTH), tok(F_WIDTH),
            pl.BlockSpec((1, N_MOD, D_MODEL),
                         lambda b, t: (jnp.where(t == nt - 1, n_batch, b), 0, 0)),
            full(1, D_MODEL), full(1, D_MODEL),
            full(D_MODEL, 2 * D_MODEL), full(1, 2 * D_MODEL),
            full(V_WIDTH, D_MODEL), full(F_WIDTH, D_MODEL), full(D_MODEL, D_MODEL),
            full(D_MODEL, LANES), full(1, LANES),
        ],
        out_specs=[tok(D_MODEL), tok(D_MODEL), tok(LANES)],
        out_shape=[
            jax.ShapeDtypeStruct((bsz, n_tok, D_MODEL), F32),
            jax.ShapeDtypeStruct((bsz, n_tok, D_MODEL), F32),
            jax.ShapeDtypeStruct((bsz, n_tok, LANES), F32),
        ],
        compiler_params=_cparams(2),
        name="merge_route",
    )(x, o, z, mod, g1, g2, w_gate, b_gate, w_ao, w_fo, w_out, w_r, b_r)


def _slot_copy(src_ref, src_row, dst_ref, dst_row, sem):
    return pltpu.make_async_copy(src_ref.at[pl.ds(src_row, 1)], dst_ref.at[pl.ds(dst_row, 1)], sem)


def _dispatch_kernel(pos_ref, h_ref, xs_in, xs_ref, sem):
    del xs_in

    def copy(r, k):
        return _slot_copy(h_ref, r, xs_ref, pos_ref[0, 0, 2 * r + k], sem)

    def issue(r, carry):
        copy(r, 0).start()
        copy(r, 1).start()
        return carry

    def drain(r, carry):
        copy(r, 0).wait()
        copy(r, 1).wait()
        return carry

    lax.fori_loop(0, TM, issue, 0, unroll=DMA_UNROLL)
    lax.fori_loop(0, TM, drain, 0, unroll=DMA_UNROLL)


def _experts_kernel(te_ref, nv_ref, xs_ref, wg_ref, wu_ref, wd_ref, ys_ref):
    del te_ref
    t = pl.program_id(0)

    @pl.when(t < nv_ref[0])
    def _():
        h = xs_ref[...].astype(BF16)
        a = jnp.dot(h, wg_ref[0], preferred_element_type=F32)
        u = jnp.dot(h, wu_ref[0], preferred_element_type=F32)
        he = (a * jax.nn.sigmoid(a) * u).astype(BF16)
        ys_ref[...] = jnp.dot(he, wd_ref[0], preferred_element_type=F32)

    @pl.when(t >= nv_ref[0])
    def _():
        ys_ref[...] = jnp.zeros(ys_ref.shape, F32)


def _combine_kernel(pos_ref, route_ref, ys_ref, y_ref, buf_ref, sem):
    def copy(r, k):
        return _slot_copy(ys_ref, pos_ref[0, 0, 2 * r + k], buf_ref.at[k], r, sem)

    def issue(r, carry):
        copy(r, 0).start()
        copy(r, 1).start()
        return carry

    def drain(r, carry):
        copy(r, 0).wait()
        copy(r, 1).wait()
        return carry

    lax.fori_loop(0, TM, issue, 0, unroll=DMA_UNROLL)
    lax.fori_loop(0, TM, drain, 0, unroll=DMA_UNROLL)
    route = route_ref[...]
    y_ref[...] = route[:, 2:3] * buf_ref[0] + route[:, 3:4] * buf_ref[1]


def _moe_call(h2, route, w_eg, w_eu, w_ed):
    n = h2.shape[0]
    n_slots = 2 * n
    n_rows = n_slots + N_EXPERTS * TMR
    n_tiles = n_rows // TMR

    e_slot = route[:, :2].astype(jnp.int32).reshape(n_slots)
    onehot = (e_slot[:, None] == jnp.arange(N_EXPERTS, dtype=jnp.int32)[None, :]).astype(jnp.int32)
    csum = jnp.cumsum(onehot, axis=0)
    rank = jnp.sum(csum * onehot, axis=1) - 1
    counts = csum[-1]
    padded = ((counts + TMR - 1) // TMR) * TMR
    ends = jnp.cumsum(padded)
    pos = jnp.sum(onehot * (ends - padded)[None, :], axis=1) + rank
    tile_expert = jnp.minimum(
        jnp.sum((jnp.arange(n_tiles, dtype=jnp.int32) * TMR)[:, None] >= ends[None, :], axis=1),
        N_EXPERTS - 1).astype(jnp.int32)
    n_valid = (ends[-1] // TMR).astype(jnp.int32).reshape(1)
    pos_tiles = pos.astype(jnp.int32).reshape(n // TM, 1, 2 * TM)

    pos_spec = pl.BlockSpec((1, 1, 2 * TM), lambda t: (t, 0, 0), memory_space=pltpu.SMEM)
    xs = pl.pallas_call(
        _dispatch_kernel,
        grid=(n // TM,),
        in_specs=[pos_spec,
                  pl.BlockSpec((TM, D_MODEL), lambda t: (t, 0)),
                  pl.BlockSpec(memory_space=pl.ANY)],
        out_specs=pl.BlockSpec(memory_space=pl.ANY),
        out_shape=jax.ShapeDtypeStruct((n_rows, D_MODEL), F32),
        scratch_shapes=[pltpu.SemaphoreType.DMA],
        input_output_aliases={2: 0},
        compiler_params=_cparams(1),
        name="dispatch",
    )(pos_tiles, h2, jnp.zeros((n_rows, D_MODEL), F32))

    w_map = lambda t, te, nv: (te[t], 0, 0)
    ys = pl.pallas_call(
        _experts_kernel,
        grid_spec=pltpu.PrefetchScalarGridSpec(
            num_scalar_prefetch=2,
            grid=(n_tiles,),
            in_specs=[
                pl.BlockSpec((TMR, D_MODEL), lambda t, te, nv: (t, 0)),
                pl.BlockSpec((1, D_MODEL, D_EXPERT), w_map),
                pl.BlockSpec((1, D_MODEL, D_EXPERT), w_map),
                pl.BlockSpec((1, D_EXPERT, D_MODEL), w_map),
            ],
            out_specs=pl.BlockSpec((TMR, D_MODEL), lambda t, te, nv: (t, 0)),
        ),
        out_shape=jax.ShapeDtypeStruct((n_rows, D_MODEL), F32),
        compiler_params=_cparams(1),
        name="experts",
    )(tile_expert, n_valid, xs, w_eg, w_eu, w_ed)

    return pl.pallas_call(
        _combine_kernel,
        grid=(n // TM,),
        in_specs=[pos_spec,
                  pl.BlockSpec((TM, LANES), lambda t: (t, 0)),
                  pl.BlockSpec(memory_space=pl.ANY)],
        out_specs=pl.BlockSpec((TM, D_MODEL), lambda t: (t, 0)),
        out_shape=jax.ShapeDtypeStruct((n, D_MODEL), F32),
        scratch_shapes=[pltpu.VMEM((2, TM, D_MODEL), F32), pltpu.SemaphoreType.DMA],
        compiler_params=_cparams(1),
        name="combine",
    )(pos_tiles, route, ys)


def _final_kernel(x_ref, y_ref, mod_ref, g_ref, o_ref):
    x = x_ref[0] + mod_ref[0][5:6] * y_ref[0]
    o_ref[0] = x * lax.rsqrt(jnp.mean(x * x, axis=-1, keepdims=True) + EPS) * g_ref[...]


def _final_call(x, y, mod, g_final, n_lat_tok):
    bsz = x.shape[0]
    tok = pl.BlockSpec((1, TM, D_MODEL), lambda b, t: (b, t, 0))
    return pl.pallas_call(
        _final_kernel,
        grid=(bsz, n_lat_tok // TM),
        in_specs=[tok, tok,
                  pl.BlockSpec((1, N_MOD, D_MODEL), lambda b, t: (b, 0, 0)),
                  pl.BlockSpec((1, D_MODEL), lambda b, t: (0, 0))],
        out_specs=tok,
        out_shape=jax.ShapeDtypeStruct((bsz, n_lat_tok, D_MODEL), F32),
        compiler_params=_cparams(2),
        name="final_norm",
    )(x, y, mod, g_final)


def _rope_tables(n_lat_tok):
    t = np.arange(n_lat_tok)
    pos = np.stack([t // GRID_W, t % GRID_W], axis=0).astype(np.float64)
    inv = ROPE_THETA ** (-np.arange(ROPE_HALF, dtype=np.float64) / ROPE_HALF)
    d = np.arange(LANES) % HEAD_DIM
    axis = d // (2 * ROPE_HALF)
    freq = d % ROPE_HALF
    ang = pos[axis, :].T * inv[freq][None, :]
    sign = np.where((d % (2 * ROPE_HALF)) < ROPE_HALF, -1.0, 1.0)
    cos = np.concatenate([np.cos(ang), np.ones((CTX_LEN, LANES))], axis=0)
    sin = np.concatenate([np.sin(ang) * sign[None, :], np.zeros((CTX_LEN, LANES))], axis=0)
    return jnp.asarray(cos, F32), jnp.asarray(sin, F32)


def kernel(x, c, ctx, c_ctx, w_ada, b_ada, g_norm1, w_in, lam_qk, g_subln, w_attn_out, w_four_out,
           w_gate, b_gate, w_out, g_norm2, w_router, b_router, w_e_gate, w_e_up, w_e_down, g_final):
    bsz, n_lat_tok, d = x.shape
    depth = w_ada.shape[0]
    assert d == D_MODEL and ctx.shape[1] == CTX_LEN and bsz < MOD_ROWS
    assert n_lat_tok % TQ == 0 and n_lat_tok % (DFT2 * 8) == 0
    n_tok = n_lat_tok + CTX_LEN

    cond = jnp.zeros((MOD_ROWS, D_MODEL), F32).at[:bsz].set(c).at[bsz].set(c_ctx)
    mods = _ada_call(cond, w_ada, b_ada).reshape(depth, MOD_ROWS, N_MOD, D_MODEL)
    cos_t, sin_t = _rope_tables(n_lat_tok)
    w_r = jnp.zeros((D_MODEL, LANES), F32).at[:, :N_EXPERTS].set(w_router)
    b_r = jnp.zeros((1, LANES), F32).at[0, :N_EXPERTS].set(b_router)

    xs = jnp.concatenate([x, ctx], axis=1)
    y = None
    for i in range(depth):
        lam_init = 0.8 - 0.6 * math.exp(-0.3 * i)
        xs, q, k, v, f = _proj_call(
            xs, y, mods[i - 1] if i else None, mods[i], g_norm1[i].reshape(1, D_MODEL),
            w_in[i].astype(BF16), cos_t, sin_t, bsz)
        o = _attn_call(q, k, v, lam_qk[i], g_subln[i], lam_init)
        z = _fourier_call(f, n_lat_tok)
        xs, h2, route = _merge_call(
            xs, o, z, mods[i], g_norm1[i].reshape(1, D_MODEL), g_norm2[i].reshape(1, D_MODEL),
            w_gate[i].astype(BF16), b_gate[i].reshape(1, 2 * D_MODEL), w_attn_out[i].astype(BF16),
            w_four_out[i].astype(BF16), w_out[i].astype(BF16), w_r, b_r, bsz)
        y = _moe_call(h2.reshape(bsz * n_tok, D_MODEL), route.reshape(bsz * n_tok, LANES),
                      w_e_gate[i].astype(BF16), w_e_up[i].astype(BF16), w_e_down[i].astype(BF16))
        y = y.reshape(bsz, n_tok, D_MODEL)
    return _final_call(xs, y, mods[depth - 1], g_final.reshape(1, D_MODEL), n_lat_tok)
```

```python
import functools
import math

import numpy as np
import jax
import jax.numpy as jnp
from jax import lax
from jax.experimental import pallas as pl
from jax.experimental.pallas import tpu as pltpu

F32 = jnp.float32
BF16 = jnp.bfloat16
HIGHEST = lax.Precision.HIGHEST

D_MODEL = 1024
GRID_W = 64
CTX_LEN = 256
N_HEADS = 8
HEAD_DIM = 64
V_DIM = 128
QK_WIDTH = 1024
V_WIDTH = 1024
F_GROUPS = 4
F_WIDTH = 512
F_GROUP_DIM = 128
IN_WIDTH = 3584
ROPE_THETA = 10000.0
ROPE_HALF = 16
N_EXPERTS = 16
N_GROUPS = 4
EXPERTS_PER_GROUP = 4
D_EXPERT = 512
EPS = 1e-6
N_MOD = 6
MOD_ROWS = 8

LANES = 128
TM = 256
TQ = 512
MXU_TILE = 256
BF16_ROWS = 16
SUM_ROWS = 16
MAX_KEY_TILES = 16
QK_ROWS = 128
LAG_LIMIT = 100.0
TMR = 256
DMA_UNROLL = 8
ADA_TN = 512
DFT2 = 128
DFT_GROUP = 8
Q_SCALE = (HEAD_DIM ** -0.5) * math.log2(math.e)
VMEM_LIMIT = 56 * 1024 * 1024


def _cparams(n_axes):
    return pltpu.CompilerParams(dimension_semantics=("arbitrary",) * n_axes,
                                vmem_limit_bytes=VMEM_LIMIT)


def _rms_mod(x, g, shift, scale):
    y = x * lax.rsqrt(jnp.mean(x * x, axis=-1, keepdims=True) + EPS) * g
    return y * (1.0 + scale) + shift


def _ada_kernel(c_ref, w_ref, b_ref, o_ref):
    c = c_ref[...]
    s = c * jax.nn.sigmoid(c)
    o_ref[0] = jnp.dot(s, w_ref[0], preferred_element_type=F32, precision=HIGHEST) + b_ref[0]


def _ada_call(cond, w_ada, b_ada):
    depth = w_ada.shape[0]
    n_out = w_ada.shape[2]
    return pl.pallas_call(
        _ada_kernel,
        grid=(depth, n_out // ADA_TN),
        in_specs=[
            pl.BlockSpec((MOD_ROWS, D_MODEL), lambda l, j: (0, 0)),
            pl.BlockSpec((1, D_MODEL, ADA_TN), lambda l, j: (l, 0, j)),
            pl.BlockSpec((1, 1, ADA_TN), lambda l, j: (l, 0, j)),
        ],
        out_specs=pl.BlockSpec((1, MOD_ROWS, ADA_TN), lambda l, j: (l, 0, j)),
        out_shape=jax.ShapeDtypeStruct((depth, MOD_ROWS, n_out), F32),
        compiler_params=_cparams(2),
        name="ada",
    )(cond, w_ada, b_ada.reshape(depth, 1, n_out))


def _proj_kernel(*refs, has_y):
    if has_y:
        (x_ref, y_ref, pmod_ref, mod_ref, g1_ref, win_ref, cos_ref, sin_ref,
         xo_ref, q_ref, k_ref, v_ref, f_ref) = refs
        x = x_ref[0] + pmod_ref[0][5:6] * y_ref[0]
        xo_ref[0] = x
    else:
        (x_ref, mod_ref, g1_ref, win_ref, cos_ref, sin_ref,
         q_ref, k_ref, v_ref, f_ref) = refs
        x = x_ref[0]
    mod = mod_ref[0]
    hb = _rms_mod(x, g1_ref[...], mod[0:1], mod[1:2]).astype(BF16)
    cos = cos_ref[...]
    sin = sin_ref[...]
    lane = lax.broadcasted_iota(jnp.int32, (TM, LANES), 1)
    first = (lane % (2 * ROPE_HALF)) < ROPE_HALF

    def rope(xc):
        up = pltpu.roll(xc, LANES - ROPE_HALF, 1)
        dn = pltpu.roll(xc, ROPE_HALF, 1)
        return xc * cos + jnp.where(first, up, dn) * sin

    w2 = 2 * LANES
    for c in range(QK_WIDTH // w2):
        uq = jnp.dot(hb, win_ref[:, c * w2:(c + 1) * w2], preferred_element_type=F32)
        uk = jnp.dot(hb, win_ref[:, QK_WIDTH + c * w2:QK_WIDTH + (c + 1) * w2],
                     preferred_element_type=F32)
        for j in range(2):
            lo = c * w2 + j * LANES
            q_ref[0, :, lo:lo + LANES] = (rope(uq[:, j * LANES:(j + 1) * LANES]) * Q_SCALE).astype(BF16)
            k_ref[0, :, lo:lo + LANES] = rope(uk[:, j * LANES:(j + 1) * LANES]).astype(BF16)
    for c in range(V_WIDTH // w2):
        lo = 2 * QK_WIDTH + c * w2
        v_ref[0, :, c * w2:(c + 1) * w2] = jnp.dot(
            hb, win_ref[:, lo:lo + w2], preferred_element_type=F32).astype(BF16)
    for c in range(F_WIDTH // w2):
        lo = 2 * QK_WIDTH + V_WIDTH + c * w2
        f_ref[0, :, c * w2:(c + 1) * w2] = jnp.dot(
            hb, win_ref[:, lo:lo + w2], preferred_element_type=F32)


def _proj_call(x, y, pmod, mod, g1, w_in, cos_t, sin_t, n_batch):
    bsz, n_tok, _ = x.shape
    nt = n_tok // TM
    has_y = y is not None

    def mod_map(b, t):
        return (jnp.where(t == nt - 1, n_batch, b), 0, 0)

    tok = lambda w: pl.BlockSpec((1, TM, w), lambda b, t: (b, t, 0))
    modspec = pl.BlockSpec((1, N_MOD, D_MODEL), mod_map)
    in_specs = [tok(D_MODEL)]
    args = [x]
    if has_y:
        in_specs += [tok(D_MODEL), modspec]
        args += [y, pmod]
    in_specs += [
        modspec,
        pl.BlockSpec((1, D_MODEL), lambda b, t: (0, 0)),
        pl.BlockSpec((D_MODEL, IN_WIDTH), lambda b, t: (0, 0)),
        pl.BlockSpec((TM, LANES), lambda b, t: (t, 0)),
        pl.BlockSpec((TM, LANES), lambda b, t: (t, 0)),
    ]
    args += [mod, g1, w_in, cos_t, sin_t]
    out_specs = [tok(QK_WIDTH), tok(QK_WIDTH), tok(V_WIDTH), tok(F_WIDTH)]
    out_shape = [
        jax.ShapeDtypeStruct((bsz, n_tok, QK_WIDTH), BF16),
        jax.ShapeDtypeStruct((bsz, n_tok, QK_WIDTH), BF16),
        jax.ShapeDtypeStruct((bsz, n_tok, V_WIDTH), BF16),
        jax.ShapeDtypeStruct((bsz, n_tok, F_WIDTH), F32),
    ]
    if has_y:
        out_specs = [tok(D_MODEL)] + out_specs
        out_shape = [jax.ShapeDtypeStruct((bsz, n_tok, D_MODEL), F32)] + out_shape
    outs = pl.pallas_call(
        functools.partial(_proj_kernel, has_y=has_y),
        grid=(bsz, nt),
        in_specs=in_specs,
        out_specs=out_specs,
        out_shape=out_shape,
        compiler_params=_cparams(2),
        name="proj",
    )(*args)
    if has_y:
        return outs
    return [x] + list(outs)


def _split_maps_t(qt):
    row = lax.broadcasted_iota(jnp.int32, qt.shape, 0)
    zero = jnp.zeros_like(qt)
    return jnp.concatenate([jnp.where(row < HEAD_DIM, qt, zero),
                            jnp.where(row >= HEAD_DIM, qt, zero)], axis=1)


def _ones_rows(n_cols):
    row = lax.broadcasted_iota(jnp.int32, (SUM_ROWS, n_cols), 0)
    return jnp.where(row == 0, 1.0, 0.0).astype(BF16)


def _vt_block(vt, n_keys):
    return jnp.concatenate([vt, _ones_rows(n_keys)], axis=0)


def _softmax_block_t(q2t, kb, vbt_ext, m_prev, acc_prev):
    st = jnp.dot(kb, q2t, preferred_element_type=F32)
    m_new = jnp.maximum(m_prev, jnp.max(st, axis=0, keepdims=True))
    alpha = jnp.exp2(m_prev - m_new)
    pt = jnp.exp2(st - m_new).astype(BF16)
    acc = alpha * acc_prev + jnp.dot(vbt_ext, pt, preferred_element_type=F32)
    return m_new, acc


def _diff_combine_t(acc, lam_ref, gs_ref, lam_init):
    tq = acc.shape[1] // 2
    lp = lam_ref[...]
    lam = (jnp.exp(jnp.sum(lp[0:1] * lp[1:2], axis=1, keepdims=True))
           - jnp.exp(jnp.sum(lp[2:3] * lp[3:4], axis=1, keepdims=True)) + lam_init)
    pv = acc[:V_DIM]
    l = acc[V_DIM:V_DIM + 1]
    ot = pv[:, :tq] / l[:, :tq] - lam * (pv[:, tq:] / l[:, tq:])
    ot = ot * lax.rsqrt(jnp.mean(ot * ot, axis=0, keepdims=True) + EPS) * gs_ref[...]
    return (ot * (1.0 - lam_init)).T.astype(BF16)


def _attn_kernel(lam_ref, gs_ref, q_ref, k_ref, v_ref, o_ref, vt_ref, q2t_ref, m_ref, acc_ref,
                 lag_ref, pt_ref, *, n_lat_tok, tk, lam_init):
    n_tok = n_lat_tok + CTX_LEN

    @pl.when(pl.program_id(2) == 0)
    def _():
        for j in range(n_tok // MXU_TILE):
            sl = slice(j * MXU_TILE, (j + 1) * MXU_TILE)
            vt_ref[:, sl] = v_ref[0, sl, :].astype(F32).T.astype(BF16)

    q2t_ref[...] = _split_maps_t(q_ref[0].astype(F32).T.astype(BF16))

    def reset():
        m_ref[...] = jnp.full(m_ref.shape, -jnp.inf, F32)
        acc_ref[...] = jnp.zeros(acc_ref.shape, F32)

    def exact_chunk(off):
        m_new, acc = _softmax_block_t(
            q2t_ref[...], k_ref[0, pl.ds(off, MXU_TILE), :],
            _vt_block(vt_ref[:, pl.ds(off, MXU_TILE)], MXU_TILE), m_ref[...], acc_ref[...])
        m_ref[...] = m_new
        acc_ref[...] = acc

    def lagged_block(j, carry):
        off = pl.multiple_of(j * tk, tk)
        m_prev = m_ref[...]
        pmax = None
        for r in range(0, tk, QK_ROWS):
            st = jnp.dot(k_ref[0, pl.ds(off + r, QK_ROWS), :], q2t_ref[...],
                         preferred_element_type=F32)
            p = jnp.exp2(st - m_prev).astype(BF16)
            pt_ref[r:r + QK_ROWS, :] = p
            for t in range(0, QK_ROWS, BF16_ROWS):
                tile = p[t:t + BF16_ROWS, :]
                pmax = tile if pmax is None else jnp.maximum(pmax, tile)
        acc = acc_ref[...] + jnp.dot(_vt_block(vt_ref[:, pl.ds(off, tk)], tk), pt_ref[...],
                                     preferred_element_type=F32)
        excess = jnp.log2(jnp.max(pmax.astype(F32), axis=0, keepdims=True))
        m_new = m_prev + jnp.maximum(excess, 0.0)
        acc_ref[...] = acc * jnp.exp2(m_prev - m_new)
        m_ref[...] = m_new
        lag_ref[...] = jnp.maximum(lag_ref[...], excess)
        return carry

    reset()
    lag_ref[...] = jnp.full(lag_ref.shape, -jnp.inf, F32)
    exact_chunk(n_lat_tok)
    lax.fori_loop(0, n_lat_tok // tk, lagged_block, 0)

    @pl.when(jnp.max(lag_ref[...]) > LAG_LIMIT)
    def _():
        reset()
        lax.fori_loop(0, n_tok // MXU_TILE,
                      lambda c, carry: (exact_chunk(pl.multiple_of(c * MXU_TILE, MXU_TILE)), carry)[1], 0)

    o_ref[0] = _diff_combine_t(acc_ref[...], lam_ref, gs_ref, lam_init)


def _attn_ctx_kernel(lam_ref, gs_ref, q_ref, k_ref, v_ref, o_any, o_ref, *, lam_init):
    del o_any
    q2t = _split_maps_t(q_ref[0].astype(F32).T.astype(BF16))
    vbt = _vt_block(v_ref[0].astype(F32).T.astype(BF16), CTX_LEN)
    m0 = jnp.full((1, 2 * CTX_LEN), -jnp.inf, F32)
    acc0 = jnp.zeros((V_DIM + SUM_ROWS, 2 * CTX_LEN), F32)
    _, acc = _softmax_block_t(q2t, k_ref[0], vbt, m0, acc0)
    o_ref[0] = _diff_combine_t(acc, lam_ref, gs_ref, lam_init)


def _key_block(n_tok):
    for mult in range(MAX_KEY_TILES, 0, -1):
        if n_tok % (mult * MXU_TILE) == 0:
            return mult * MXU_TILE
    raise ValueError(n_tok)


def _attn_call(q, k, v, lam_qk, g_subln, lam_init):
    bsz, n_tok, _ = q.shape
    n_lat_tok = n_tok - CTX_LEN
    tk = _key_block(n_lat_tok)
    tq_step = TQ
    n_col = 2 * TQ
    small = [pl.BlockSpec((4, HEAD_DIM), lambda *_: (0, 0)),
             pl.BlockSpec((V_DIM, 1), lambda *_: (0, 0))]
    gs = g_subln.reshape(V_DIM, 1)
    o = pl.pallas_call(
        functools.partial(_attn_kernel, n_lat_tok=n_lat_tok, tk=tk, lam_init=lam_init),
        grid=(bsz, N_HEADS, n_lat_tok // tq_step),
        in_specs=small + [
            pl.BlockSpec((1, tq_step, LANES), lambda b, h, t: (b, t, h)),
            pl.BlockSpec((1, n_tok, LANES), lambda b, h, t: (b, 0, h)),
            pl.BlockSpec((1, n_tok, LANES), lambda b, h, t: (b, 0, h)),
        ],
        out_specs=pl.BlockSpec((1, tq_step, V_DIM), lambda b, h, t: (b, t, h)),
        out_shape=jax.ShapeDtypeStruct((bsz, n_tok, V_WIDTH), BF16),
        scratch_shapes=[
            pltpu.VMEM((V_DIM, n_tok), BF16),
            pltpu.VMEM((LANES, n_col), BF16),
            pltpu.VMEM((1, n_col), F32),
            pltpu.VMEM((V_DIM + SUM_ROWS, n_col), F32),
            pltpu.VMEM((1, n_col), F32),
            pltpu.VMEM((tk, n_col), BF16),
        ],
        compiler_params=_cparams(3),
        name="diff_attn",
    )(lam_qk, gs, q, k, v)
    ctx_blk = n_lat_tok // CTX_LEN
    ctx_spec = pl.BlockSpec((1, CTX_LEN, LANES), lambda b, h: (b, ctx_blk, h))
    return pl.pallas_call(
        functools.partial(_attn_ctx_kernel, lam_init=lam_init),
        grid=(bsz, N_HEADS),
        in_specs=small + [ctx_spec, ctx_spec, ctx_spec, pl.BlockSpec(memory_space=pl.ANY)],
        out_specs=ctx_spec,
        out_shape=jax.ShapeDtypeStruct((bsz, n_tok, V_WIDTH), BF16),
        input_output_aliases={5: 0},
        compiler_params=_cparams(2),
        name="diff_attn_ctx",
    )(lam_qk, gs, q, k, v, o)


def _dft1_kernel(u_ref, c1_ref, s1_ref, twc_ref, tws_ref, o_ref):
    for j in range(DFT_GROUP):
        u = u_ref[0, :, j, :].astype(BF16)
        ar = jnp.dot(c1_ref[...], u, preferred_element_type=F32)
        ai = -jnp.dot(s1_ref[...], u, preferred_element_type=F32)
        twc = twc_ref[j]
        tws = tws_ref[j]
        for g in range(F_GROUPS):
            sl = slice(g * LANES, (g + 1) * LANES)
            o_ref[0, :, j, g * LANES:(g + 1) * LANES] = ar[:, sl] * twc + ai[:, sl] * tws
            o_ref[0, :, j, F_WIDTH + g * LANES:F_WIDTH + (g + 1) * LANES] = ai[:, sl] * twc - ar[:, sl] * tws


def _channel_dft(gr, gi, cc_ref, sc_ref, norm):
    grb = gr.astype(BF16)
    gib = gi.astype(BF16)
    out = []
    for g in range(F_GROUPS):
        sl = slice(g * LANES, (g + 1) * LANES)
        out.append((jnp.dot(grb[:, sl], cc_ref[...], preferred_element_type=F32)
                    + jnp.dot(gib[:, sl], sc_ref[...], preferred_element_type=F32)) * norm)
    return out


def _dft2_kernel(b_ref, c2_ref, s2_ref, cc_ref, sc_ref, o_ref, *, norm):
    for j in range(DFT_GROUP):
        bb = b_ref[0, j].astype(BF16)
        cb = jnp.dot(c2_ref[...], bb, preferred_element_type=F32)
        sb = jnp.dot(s2_ref[...], bb, preferred_element_type=F32)
        gr = cb[:, :F_WIDTH] + sb[:, F_WIDTH:]
        gi = cb[:, F_WIDTH:] - sb[:, :F_WIDTH]
        for g, z in enumerate(_channel_dft(gr, gi, cc_ref, sc_ref, norm)):
            o_ref[0, :, j, g * LANES:(g + 1) * LANES] = z


def _dft_ctx_kernel(f_ref, c_ref, s_ref, cc_ref, sc_ref, z_any, o_ref, *, norm):
    del z_any
    f = f_ref[0].astype(BF16)
    gr = jnp.dot(c_ref[...], f, preferred_element_type=F32)
    gi = -jnp.dot(s_ref[...], f, preferred_element_type=F32)
    for g, z in enumerate(_channel_dft(gr, gi, cc_ref, sc_ref, norm)):
        o_ref[0, :, g * LANES:(g + 1) * LANES] = z


def _dft_mats(n):
    idx = np.arange(n, dtype=np.float64)
    ang = 2.0 * np.pi * np.outer(idx, idx) / n
    return jnp.asarray(np.cos(ang), BF16), jnp.asarray(np.sin(ang), BF16)


def _fourier_call(f, n_lat_tok):
    bsz, n_tok, _ = f.shape
    n1 = n_lat_tok // DFT2
    full = lambda *shape: pl.BlockSpec(shape, lambda *_: (0,) * len(shape))
    c1, s1 = _dft_mats(n1)
    c2, s2 = _dft_mats(DFT2)
    cc, sc = _dft_mats(F_GROUP_DIM)
    cctx, sctx = _dft_mats(CTX_LEN)
    tw_ang = 2.0 * np.pi * np.outer(np.arange(DFT2, dtype=np.float64),
                                    np.arange(n1, dtype=np.float64)) / n_lat_tok
    twc = jnp.asarray(np.broadcast_to(np.cos(tw_ang)[:, :, None], (DFT2, n1, LANES)), F32)
    tws = jnp.asarray(np.broadcast_to(np.sin(tw_ang)[:, :, None], (DFT2, n1, LANES)), F32)

    fv = f.reshape(bsz, n_tok // DFT2, DFT2, F_WIDTH)
    mid = pl.pallas_call(
        _dft1_kernel,
        grid=(bsz, DFT2 // DFT_GROUP),
        in_specs=[
            pl.BlockSpec((1, n1, DFT_GROUP, F_WIDTH), lambda b, j: (b, 0, j, 0)),
            full(n1, n1), full(n1, n1),
            pl.BlockSpec((DFT_GROUP, n1, LANES), lambda b, j: (j, 0, 0)),
            pl.BlockSpec((DFT_GROUP, n1, LANES), lambda b, j: (j, 0, 0)),
        ],
        out_specs=pl.BlockSpec((1, n1, DFT_GROUP, 2 * F_WIDTH), lambda b, j: (b, 0, j, 0)),
        out_shape=jax.ShapeDtypeStruct((bsz, n1, DFT2, 2 * F_WIDTH), F32),
        compiler_params=_cparams(2),
        name="dft_outer",
    )(fv, c1, s1, twc, tws)

    z = pl.pallas_call(
        functools.partial(_dft2_kernel, norm=1.0 / math.sqrt(n_lat_tok * F_GROUP_DIM)),
        grid=(bsz, n1 // DFT_GROUP),
        in_specs=[
            pl.BlockSpec((1, DFT_GROUP, DFT2, 2 * F_WIDTH), lambda b, j: (b, j, 0, 0)),
            full(DFT2, DFT2), full(DFT2, DFT2),
            full(F_GROUP_DIM, F_GROUP_DIM), full(F_GROUP_DIM, F_GROUP_DIM),
        ],
        out_specs=pl.BlockSpec((1, DFT2, DFT_GROUP, F_WIDTH), lambda b, j: (b, 0, j, 0)),
        out_shape=jax.ShapeDtypeStruct((bsz, n_tok // n1, n1, F_WIDTH), F32),
        compiler_params=_cparams(2),
        name="dft_inner",
    )(mid, c2, s2, cc, sc)
    z = z.reshape(bsz, n_tok, F_WIDTH)

    ctx_blk = n_lat_tok // CTX_LEN
    z = pl.pallas_call(
        functools.partial(_dft_ctx_kernel, norm=1.0 / math.sqrt(CTX_LEN * F_GROUP_DIM)),
        grid=(bsz,),
        in_specs=[
            pl.BlockSpec((1, CTX_LEN, F_WIDTH), lambda b: (b, ctx_blk, 0)),
            full(CTX_LEN, CTX_LEN), full(CTX_LEN, CTX_LEN),
            full(F_GROUP_DIM, F_GROUP_DIM), full(F_GROUP_DIM, F_GROUP_DIM),
            pl.BlockSpec(memory_space=pl.ANY),
        ],
        out_specs=pl.BlockSpec((1, CTX_LEN, F_WIDTH), lambda b: (b, ctx_blk, 0)),
        out_shape=jax.ShapeDtypeStruct((bsz, n_tok, F_WIDTH), F32),
        input_output_aliases={5: 0},
        compiler_params=_cparams(1),
        name="dft_ctx",
    )(f, cctx, sctx, cc, sc, z)
    return z


def _merge_kernel(x_ref, o_ref, z_ref, mod_ref, g1_ref, g2_ref, wgate_ref, bgate_ref,
                  wao_ref, wfo_ref, wout_ref, wr_ref, br_ref, xo_ref, h2_ref, route_ref):
    x = x_ref[0]
    mod = mod_ref[0]
    hb = _rms_mod(x, g1_ref[...], mod[0:1], mod[1:2]).astype(BF16)
    gates = jax.nn.sigmoid(jnp.dot(hb, wgate_ref[...], preferred_element_type=F32) + bgate_ref[...])
    ya = jnp.dot(o_ref[0], wao_ref[...], preferred_element_type=F32)
    yf = jnp.dot(z_ref[0].astype(BF16), wfo_ref[...], preferred_element_type=F32)
    merged = gates[:, :D_MODEL] * ya + gates[:, D_MODEL:] * yf
    out = jnp.dot(merged.astype(BF16), wout_ref[...], preferred_element_type=F32)
    xn = x + mod[2:3] * out
    xo_ref[0] = xn
    h2 = _rms_mod(xn, g2_ref[...], mod[3:4], mod[4:5])
    h2_ref[0] = h2

    logits = jnp.dot(h2, wr_ref[...], preferred_element_type=F32, precision=HIGHEST)
    aff = jax.nn.sigmoid(logits)
    lane = lax.broadcasted_iota(jnp.int32, (TM, LANES), 1)
    lanef = lane.astype(F32)
    neg = jnp.full((TM, LANES), -jnp.inf, F32)
    sel = jnp.where(lane < N_EXPERTS, aff + br_ref[...], neg)
    grp = lane // EXPERTS_PER_GROUP
    best = i1b = i2b = None
    for g in range(N_GROUPS):
        sg = jnp.where(grp == g, sel, neg)
        m1 = jnp.max(sg, axis=1, keepdims=True)
        i1 = jnp.min(jnp.where(sg == m1, lanef, float(LANES)), axis=1, keepdims=True)
        sg2 = jnp.where(lanef == i1, neg, sg)
        m2 = jnp.max(sg2, axis=1, keepdims=True)
        i2 = jnp.min(jnp.where(sg2 == m2, lanef, float(LANES)), axis=1, keepdims=True)
        score = m1 + m2
        if g == 0:
            best, i1b, i2b = score, i1, i2
        else:
            upd = score > best
            best = jnp.where(upd, score, best)
            i1b = jnp.where(upd, i1, i1b)
            i2b = jnp.where(upd, i2, i2b)
    hit1 = lanef == i1b
    hit2 = lanef == i2b
    a1 = jnp.sum(jnp.where(hit1, aff, 0.0), axis=1, keepdims=True)
    a2 = jnp.sum(jnp.where(hit2, aff, 0.0), axis=1, keepdims=True)
    den = a1 + a2
    route_ref[0] = (jnp.where(lane == 0, i1b, 0.0) + jnp.where(lane == 1, i2b, 0.0)
                    + jnp.where(lane == 2, a1 / den, 0.0) + jnp.where(lane == 3, a2 / den, 0.0))


def _merge_call(x, o, z, mod, g1, g2, w_gate, b_gate, w_ao, w_fo, w_out, w_r, b_r, n_batch):
    bsz, n_tok, _ = x.shape
    nt = n_tok // TM
    tok = lambda w: pl.BlockSpec((1, TM, w), lambda b, t: (b, t, 0))
    full = lambda *shape: pl.BlockSpec(shape, lambda b, t: (0,) * len(shape))
    return pl.pallas_call(
        _merge_kernel,
        grid=(bsz, nt),
        in_specs=[
            tok(D_MODEL), tok(V_WIDTH), tok(F_WIDTH),
            pl.BlockSpec((1, N_MOD, D_MODEL),
                         lambda b, t: (jnp.where(t == nt - 1, n_batch, b), 0, 0)),
            full(1, D_MODEL), full(1, D_MODEL),
            full(D_MODEL, 2 * D_MODEL), full(1, 2 * D_MODEL),
            full(V_WIDTH, D_MODEL), full(F_WIDTH, D_MODEL), full(D_MODEL, D_MODEL),
            full(D_MODEL, LANES), full(1, LANES),
        ],
        out_specs=[tok(D_MODEL), tok(D_MODEL), tok(LANES)],
        out_shape=[
            jax.ShapeDtypeStruct((bsz, n_tok, D_MODEL), F32),
            jax.ShapeDtypeStruct((bsz, n_tok, D_MODEL), F32),
            jax.ShapeDtypeStruct((bsz, n_tok, LANES), F32),
        ],
        compiler_params=_cparams(2),
        name="merge_route",
    )(x, o, z, mod, g1, g2, w_gate, b_gate, w_ao, w_fo, w_out, w_r, b_r)


def _slot_copy(src_ref, src_row, dst_ref, dst_row, sem):
    return pltpu.make_async_copy(src_ref.at[pl.ds(src_row, 1)], dst_ref.at[pl.ds(dst_row, 1)], sem)


def _dispatch_kernel(pos_ref, h_ref, xs_in, xs_ref, sem):
    del xs_in

    def copy(r, k):
        return _slot_copy(h_ref, r, xs_ref, pos_ref[0, 0, 2 * r + k], sem)

    def issue(r, carry):
        copy(r, 0).start(priority=0)
        copy(r, 1).start(priority=1)
        return carry

    def drain(r, carry):
        copy(r, 0).wait()
        copy(r, 1).wait()
        return carry

    lax.fori_loop(0, TM, issue, 0, unroll=DMA_UNROLL)
    lax.fori_loop(0, TM, drain, 0, unroll=DMA_UNROLL)


def _experts_kernel(te_ref, nv_ref, xs_ref, wg_ref, wu_ref, wd_ref, ys_ref):
    del te_ref
    t = pl.program_id(0)

    @pl.when(t < nv_ref[0])
    def _():
        h = xs_ref[...].astype(BF16)
        a = jnp.dot(h, wg_ref[0], preferred_element_type=F32)
        u = jnp.dot(h, wu_ref[0], preferred_element_type=F32)
        he = (a * jax.nn.sigmoid(a) * u).astype(BF16)
        ys_ref[...] = jnp.dot(he, wd_ref[0], preferred_element_type=F32)

    @pl.when(t >= nv_ref[0])
    def _():
        ys_ref[...] = jnp.zeros(ys_ref.shape, F32)


def _combine_kernel(pos_ref, route_ref, ys_ref, y_ref, buf_ref, sem):
    def copy(r, k):
        return _slot_copy(ys_ref, pos_ref[0, 0, 2 * r + k], buf_ref.at[k], r, sem)

    def issue(r, carry):
        copy(r, 0).start(priority=0)
        copy(r, 1).start(priority=1)
        return carry

    def drain(r, carry):
        copy(r, 0).wait()
        copy(r, 1).wait()
        return carry

    lax.fori_loop(0, TM, issue, 0, unroll=DMA_UNROLL)
    lax.fori_loop(0, TM, drain, 0, unroll=DMA_UNROLL)
    route = route_ref[...]
    y_ref[...] = route[:, 2:3] * buf_ref[0] + route[:, 3:4] * buf_ref[1]


def _moe_call(h2, route, w_eg, w_eu, w_ed):
    n = h2.shape[0]
    n_slots = 2 * n
    n_rows = n_slots + N_EXPERTS * TMR
    n_tiles = n_rows // TMR

    e_slot = route[:, :2].astype(jnp.int32).reshape(n_slots)
    onehot = (e_slot[:, None] == jnp.arange(N_EXPERTS, dtype=jnp.int32)[None, :]).astype(jnp.int32)
    csum = jnp.cumsum(onehot, axis=0)
    rank = jnp.sum(csum * onehot, axis=1) - 1
    counts = csum[-1]
    padded = ((counts + TMR - 1) // TMR) * TMR
    ends = jnp.cumsum(padded)
    pos = jnp.sum(onehot * (ends - padded)[None, :], axis=1) + rank
    tile_expert = jnp.minimum(
        jnp.sum((jnp.arange(n_tiles, dtype=jnp.int32) * TMR)[:, None] >= ends[None, :], axis=1),
        N_EXPERTS - 1).astype(jnp.int32)
    n_valid = (ends[-1] // TMR).astype(jnp.int32).reshape(1)
    pos_tiles = pos.astype(jnp.int32).reshape(n // TM, 1, 2 * TM)

    pos_spec = pl.BlockSpec((1, 1, 2 * TM), lambda t: (t, 0, 0), memory_space=pltpu.SMEM)
    xs = pl.pallas_call(
        _dispatch_kernel,
        grid=(n // TM,),
        in_specs=[pos_spec,
                  pl.BlockSpec((TM, D_MODEL), lambda t: (t, 0)),
                  pl.BlockSpec(memory_space=pl.ANY)],
        out_specs=pl.BlockSpec(memory_space=pl.ANY),
        out_shape=jax.ShapeDtypeStruct((n_rows, D_MODEL), F32),
        scratch_shapes=[pltpu.SemaphoreType.DMA],
        input_output_aliases={2: 0},
        compiler_params=_cparams(1),
        name="dispatch",
    )(pos_tiles, h2, jnp.zeros((n_rows, D_MODEL), F32))

    w_map = lambda t, te, nv: (te[t], 0, 0)
    ys = pl.pallas_call(
        _experts_kernel,
        grid_spec=pltpu.PrefetchScalarGridSpec(
            num_scalar_prefetch=2,
            grid=(n_tiles,),
            in_specs=[
                pl.BlockSpec((TMR, D_MODEL), lambda t, te, nv: (t, 0)),
                pl.BlockSpec((1, D_MODEL, D_EXPERT), w_map),
                pl.BlockSpec((1, D_MODEL, D_EXPERT), w_map),
                pl.BlockSpec((1, D_EXPERT, D_MODEL), w_map),
            ],
            out_specs=pl.BlockSpec((TMR, D_MODEL), lambda t, te, nv: (t, 0)),
        ),
        out_shape=jax.ShapeDtypeStruct((n_rows, D_MODEL), F32),
        compiler_params=_cparams(1),
        name="experts",
    )(tile_expert, n_valid, xs, w_eg, w_eu, w_ed)

    return pl.pallas_call(
        _combine_kernel,
        grid=(n // TM,),
        in_specs=[pos_spec,
                  pl.BlockSpec((TM, LANES), lambda t: (t, 0)),
                  pl.BlockSpec(memory_space=pl.ANY)],
        out_specs=pl.BlockSpec((TM, D_MODEL), lambda t: (t, 0)),
        out_shape=jax.ShapeDtypeStruct((n, D_MODEL), F32),
        scratch_shapes=[pltpu.VMEM((2, TM, D_MODEL), F32), pltpu.SemaphoreType.DMA],
        compiler_params=_cparams(1),
        name="combine",
    )(pos_tiles, route, ys)


def _final_kernel(x_ref, y_ref, mod_ref, g_ref, o_ref):
    x = x_ref[0] + mod_ref[0][5:6] * y_ref[0]
    o_ref[0] = x * lax.rsqrt(jnp.mean(x * x, axis=-1, keepdims=True) + EPS) * g_ref[...]


def _final_call(x, y, mod, g_final, n_lat_tok):
    bsz = x.shape[0]
    tok = pl.BlockSpec((1, TM, D_MODEL), lambda b, t: (b, t, 0))
    return pl.pallas_call(
        _final_kernel,
        grid=(bsz, n_lat_tok // TM),
        in_specs=[tok, tok,
                  pl.BlockSpec((1, N_MOD, D_MODEL), lambda b, t: (b, 0, 0)),
                  pl.BlockSpec((1, D_MODEL), lambda b, t: (0, 0))],
        out_specs=tok,
        out_shape=jax.ShapeDtypeStruct((bsz, n_lat_tok, D_MODEL), F32),
        compiler_params=_cparams(2),
        name="final_norm",
    )(x, y, mod, g_final)


def _rope_tables(n_lat_tok):
    t = np.arange(n_lat_tok)
    pos = np.stack([t // GRID_W, t % GRID_W], axis=0).astype(np.float64)
    inv = ROPE_THETA ** (-np.arange(ROPE_HALF, dtype=np.float64) / ROPE_HALF)
    d = np.arange(LANES) % HEAD_DIM
    axis = d // (2 * ROPE_HALF)
    freq = d % ROPE_HALF
    ang = pos[axis, :].T * inv[freq][None, :]
    sign = np.where((d % (2 * ROPE_HALF)) < ROPE_HALF, -1.0, 1.0)
    cos = np.concatenate([np.cos(ang), np.ones((CTX_LEN, LANES))], axis=0)
    sin = np.concatenate([np.sin(ang) * sign[None, :], np.zeros((CTX_LEN, LANES))], axis=0)
    return jnp.asarray(cos, F32), jnp.asarray(sin, F32)


def kernel(x, c, ctx, c_ctx, w_ada, b_ada, g_norm1, w_in, lam_qk, g_subln, w_attn_out, w_four_out,
           w_gate, b_gate, w_out, g_norm2, w_router, b_router, w_e_gate, w_e_up, w_e_down, g_final):
    bsz, n_lat_tok, d = x.shape
    depth = w_ada.shape[0]
    assert d == D_MODEL and ctx.shape[1] == CTX_LEN and bsz < MOD_ROWS
    assert n_lat_tok % TQ == 0 and n_lat_tok % (DFT2 * 8) == 0
    n_tok = n_lat_tok + CTX_LEN

    cond = jnp.zeros((MOD_ROWS, D_MODEL), F32).at[:bsz].set(c).at[bsz].set(c_ctx)
    mods = _ada_call(cond, w_ada, b_ada).reshape(depth, MOD_ROWS, N_MOD, D_MODEL)
    cos_t, sin_t = _rope_tables(n_lat_tok)
    w_r = jnp.zeros((D_MODEL, LANES), F32).at[:, :N_EXPERTS].set(w_router)
    b_r = jnp.zeros((1, LANES), F32).at[0, :N_EXPERTS].set(b_router)

    xs = jnp.concatenate([x, ctx], axis=1)
    y = None
    for i in range(depth):
        lam_init = 0.8 - 0.6 * math.exp(-0.3 * i)
        xs, q, k, v, f = _proj_call(
            xs, y, mods[i - 1] if i else None, mods[i], g_norm1[i].reshape(1, D_MODEL),
            w_in[i].astype(BF16), cos_t, sin_t, bsz)
        o = _attn_call(q, k, v, lam_qk[i], g_subln[i], lam_init)
        z = _fourier_call(f, n_lat_tok)
        xs, h2, route = _merge_call(
            xs, o, z, mods[i], g_norm1[i].reshape(1, D_MODEL), g_norm2[i].reshape(1, D_MODEL),
            w_gate[i].astype(BF16), b_gate[i].reshape(1, 2 * D_MODEL), w_attn_out[i].astype(BF16),
            w_four_out[i].astype(BF16), w_out[i].astype(BF16), w_r, b_r, bsz)
        y = _moe_call(h2.reshape(bsz * n_tok, D_MODEL), route.reshape(bsz * n_tok, LANES),
                      w_e_gate[i].astype(BF16), w_e_up[i].astype(BF16), w_e_down[i].astype(BF16))
        y = y.reshape(bsz, n_tok, D_MODEL)
    return _final_call(xs, y, mods[depth - 1], g_final.reshape(1, D_MODEL), n_lat_tok)
```

```python
import functools
import math

import numpy as np
import jax
import jax.numpy as jnp
from jax import lax
from jax.experimental import pallas as pl
from jax.experimental.pallas import tpu as pltpu

F32 = jnp.float32
BF16 = jnp.bfloat16
HIGHEST = lax.Precision.HIGHEST

D_MODEL = 1024
GRID_W = 64
CTX_LEN = 256
N_HEADS = 8
HEAD_DIM = 64
V_DIM = 128
QK_WIDTH = 1024
V_WIDTH = 1024
F_GROUPS = 4
F_WIDTH = 512
F_GROUP_DIM = 128
IN_WIDTH = 3584
ROPE_THETA = 10000.0
ROPE_HALF = 16
N_EXPERTS = 16
N_GROUPS = 4
EXPERTS_PER_GROUP = 4
D_EXPERT = 512
EPS = 1e-6
N_MOD = 6
MOD_ROWS = 8

LANES = 128
TM = 256
TQ = 1024
MXU_TILE = 256
BF16_ROWS = 16
SUM_ROWS = 16
MAX_KEY_TILES = 16
QK_ROWS = 512
LAG_LIMIT = 100.0
TMR = 256
DMA_UNROLL = 8
ADA_TN = 512
DFT2 = 128
DFT_GROUP = 8
Q_SCALE = (HEAD_DIM ** -0.5) * math.log2(math.e)
VMEM_LIMIT = 56 * 1024 * 1024


def _cparams(n_axes):
    return pltpu.CompilerParams(dimension_semantics=("arbitrary",) * n_axes,
                                vmem_limit_bytes=VMEM_LIMIT)


def _rms_mod(x, g, shift, scale):
    y = x * lax.rsqrt(jnp.mean(x * x, axis=-1, keepdims=True) + EPS) * g
    return y * (1.0 + scale) + shift


def _ada_kernel(c_ref, w_ref, b_ref, o_ref):
    c = c_ref[...]
    s = c * jax.nn.sigmoid(c)
    o_ref[0] = jnp.dot(s, w_ref[0], preferred_element_type=F32, precision=HIGHEST) + b_ref[0]


def _ada_call(cond, w_ada, b_ada):
    depth = w_ada.shape[0]
    n_out = w_ada.shape[2]
    return pl.pallas_call(
        _ada_kernel,
        grid=(depth, n_out // ADA_TN),
        in_specs=[
            pl.BlockSpec((MOD_ROWS, D_MODEL), lambda l, j: (0, 0)),
            pl.BlockSpec((1, D_MODEL, ADA_TN), lambda l, j: (l, 0, j)),
            pl.BlockSpec((1, 1, ADA_TN), lambda l, j: (l, 0, j)),
        ],
        out_specs=pl.BlockSpec((1, MOD_ROWS, ADA_TN), lambda l, j: (l, 0, j)),
        out_shape=jax.ShapeDtypeStruct((depth, MOD_ROWS, n_out), F32),
        compiler_params=_cparams(2),
        name="ada",
    )(cond, w_ada, b_ada.reshape(depth, 1, n_out))


def _proj_kernel(*refs, has_y):
    if has_y:
        (x_ref, y_ref, pmod_ref, mod_ref, g1_ref, win_ref, cos_ref, sin_ref,
         xo_ref, q_ref, k_ref, v_ref, f_ref) = refs
        x = x_ref[0] + pmod_ref[0][5:6] * y_ref[0]
        xo_ref[0] = x
    else:
        (x_ref, mod_ref, g1_ref, win_ref, cos_ref, sin_ref,
         q_ref, k_ref, v_ref, f_ref) = refs
        x = x_ref[0]
    mod = mod_ref[0]
    hb = _rms_mod(x, g1_ref[...], mod[0:1], mod[1:2]).astype(BF16)
    cos = cos_ref[...]
    sin = sin_ref[...]
    lane = lax.broadcasted_iota(jnp.int32, (TM, LANES), 1)
    first = (lane % (2 * ROPE_HALF)) < ROPE_HALF

    def rope(xc):
        up = pltpu.roll(xc, LANES - ROPE_HALF, 1)
        dn = pltpu.roll(xc, ROPE_HALF, 1)
        return xc * cos + jnp.where(first, up, dn) * sin

    w2 = 2 * LANES
    for c in range(QK_WIDTH // w2):
        uq = jnp.dot(hb, win_ref[:, c * w2:(c + 1) * w2], preferred_element_type=F32)
        uk = jnp.dot(hb, win_ref[:, QK_WIDTH + c * w2:QK_WIDTH + (c + 1) * w2],
                     preferred_element_type=F32)
        for j in range(2):
            lo = c * w2 + j * LANES
            q_ref[0, :, lo:lo + LANES] = (rope(uq[:, j * LANES:(j + 1) * LANES]) * Q_SCALE).astype(BF16)
            k_ref[0, :, lo:lo + LANES] = rope(uk[:, j * LANES:(j + 1) * LANES]).astype(BF16)
    for c in range(V_WIDTH // w2):
        lo = 2 * QK_WIDTH + c * w2
        v_ref[0, :, c * w2:(c + 1) * w2] = jnp.dot(
            hb, win_ref[:, lo:lo + w2], preferred_element_type=F32).astype(BF16)
    for c in range(F_WIDTH // w2):
        lo = 2 * QK_WIDTH + V_WIDTH + c * w2
        f_ref[0, :, c * w2:(c + 1) * w2] = jnp.dot(
            hb, win_ref[:, lo:lo + w2], preferred_element_type=F32)


def _proj_call(x, y, pmod, mod, g1, w_in, cos_t, sin_t, n_batch):
    bsz, n_tok, _ = x.shape
    nt = n_tok // TM
    has_y = y is not None

    def mod_map(b, t):
        return (jnp.where(t == nt - 1, n_batch, b), 0, 0)

    tok = lambda w: pl.BlockSpec((1, TM, w), lambda b, t: (b, t, 0))
    modspec = pl.BlockSpec((1, N_MOD, D_MODEL), mod_map)
    in_specs = [tok(D_MODEL)]
    args = [x]
    if has_y:
        in_specs += [tok(D_MODEL), modspec]
        args += [y, pmod]
    in_specs += [
        modspec,
        pl.BlockSpec((1, D_MODEL), lambda b, t: (0, 0)),
        pl.BlockSpec((D_MODEL, IN_WIDTH), lambda b, t: (0, 0)),
        pl.BlockSpec((TM, LANES), lambda b, t: (t, 0)),
        pl.BlockSpec((TM, LANES), lambda b, t: (t, 0)),
    ]
    args += [mod, g1, w_in, cos_t, sin_t]
    out_specs = [tok(QK_WIDTH), tok(QK_WIDTH), tok(V_WIDTH), tok(F_WIDTH)]
    out_shape = [
        jax.ShapeDtypeStruct((bsz, n_tok, QK_WIDTH), BF16),
        jax.ShapeDtypeStruct((bsz, n_tok, QK_WIDTH), BF16),
        jax.ShapeDtypeStruct((bsz, n_tok, V_WIDTH), BF16),
        jax.ShapeDtypeStruct((bsz, n_tok, F_WIDTH), F32),
    ]
    if has_y:
        out_specs = [tok(D_MODEL)] + out_specs
        out_shape = [jax.ShapeDtypeStruct((bsz, n_tok, D_MODEL), F32)] + out_shape
    outs = pl.pallas_call(
        functools.partial(_proj_kernel, has_y=has_y),
        grid=(bsz, nt),
        in_specs=in_specs,
        out_specs=out_specs,
        out_shape=out_shape,
        compiler_params=_cparams(2),
        name="proj",
    )(*args)
    if has_y:
        return outs
    return [x] + list(outs)


def _split_maps_t(qt):
    row = lax.broadcasted_iota(jnp.int32, qt.shape, 0)
    zero = jnp.zeros_like(qt)
    return jnp.concatenate([jnp.where(row < HEAD_DIM, qt, zero),
                            jnp.where(row >= HEAD_DIM, qt, zero)], axis=1)


def _ones_rows(n_cols):
    row = lax.broadcasted_iota(jnp.int32, (SUM_ROWS, n_cols), 0)
    return jnp.where(row == 0, 1.0, 0.0).astype(BF16)


def _vt_block(vt, n_keys):
    return jnp.concatenate([vt, _ones_rows(n_keys)], axis=0)


def _softmax_block_t(q2t, kb, vbt_ext, m_prev, acc_prev):
    st = jnp.dot(kb, q2t, preferred_element_type=F32)
    m_new = jnp.maximum(m_prev, jnp.max(st, axis=0, keepdims=True))
    alpha = jnp.exp2(m_prev - m_new)
    pt = jnp.exp2(st - m_new).astype(BF16)
    acc = alpha * acc_prev + jnp.dot(vbt_ext, pt, preferred_element_type=F32)
    return m_new, acc


def _diff_combine_t(acc, lam_ref, gs_ref, lam_init):
    tq = acc.shape[1] // 2
    lp = lam_ref[...]
    lam = (jnp.exp(jnp.sum(lp[0:1] * lp[1:2], axis=1, keepdims=True))
           - jnp.exp(jnp.sum(lp[2:3] * lp[3:4], axis=1, keepdims=True)) + lam_init)
    pv = acc[:V_DIM]
    l = acc[V_DIM:V_DIM + 1]
    ot = pv[:, :tq] / l[:, :tq] - lam * (pv[:, tq:] / l[:, tq:])
    ot = ot * lax.rsqrt(jnp.mean(ot * ot, axis=0, keepdims=True) + EPS) * gs_ref[...]
    return (ot * (1.0 - lam_init)).T.astype(BF16)


def _attn_kernel(lam_ref, gs_ref, q_ref, k_ref, v_ref, o_ref, vt_ref, q2t_ref, m_ref, acc_ref,
                 lag_ref, pt_ref, *, n_lat_tok, tk, lam_init):
    n_tok = n_lat_tok + CTX_LEN

    @pl.when(pl.program_id(2) == 0)
    def _():
        for j in range(n_tok // MXU_TILE):
            sl = slice(j * MXU_TILE, (j + 1) * MXU_TILE)
            vt_ref[:, sl] = v_ref[0, sl, :].astype(F32).T.astype(BF16)

    q2t_ref[...] = _split_maps_t(q_ref[0].astype(F32).T.astype(BF16))

    def reset():
        m_ref[...] = jnp.full(m_ref.shape, -jnp.inf, F32)
        acc_ref[...] = jnp.zeros(acc_ref.shape, F32)

    def exact_chunk(off):
        m_new, acc = _softmax_block_t(
            q2t_ref[...], k_ref[0, pl.ds(off, MXU_TILE), :],
            _vt_block(vt_ref[:, pl.ds(off, MXU_TILE)], MXU_TILE), m_ref[...], acc_ref[...])
        m_ref[...] = m_new
        acc_ref[...] = acc

    def lagged_block(j, carry):
        off = pl.multiple_of(j * tk, tk)
        m_prev = m_ref[...]
        pmax = None
        for r in range(0, tk, QK_ROWS):
            st = jnp.dot(k_ref[0, pl.ds(off + r, QK_ROWS), :], q2t_ref[...],
                         preferred_element_type=F32)
            p = jnp.exp2(st - m_prev).astype(BF16)
            pt_ref[r:r + QK_ROWS, :] = p
            for t in range(0, QK_ROWS, BF16_ROWS):
                tile = p[t:t + BF16_ROWS, :]
                pmax = tile if pmax is None else jnp.maximum(pmax, tile)
        acc = acc_ref[...] + jnp.dot(_vt_block(vt_ref[:, pl.ds(off, tk)], tk), pt_ref[...],
                                     preferred_element_type=F32)
        excess = jnp.log2(jnp.max(pmax.astype(F32), axis=0, keepdims=True))
        m_new = m_prev + jnp.maximum(excess, 0.0)
        acc_ref[...] = acc * jnp.exp2(m_prev - m_new)
        m_ref[...] = m_new
        lag_ref[...] = jnp.maximum(lag_ref[...], excess)
        return carry

    reset()
    lag_ref[...] = jnp.full(lag_ref.shape, -jnp.inf, F32)
    exact_chunk(n_lat_tok)
    lax.fori_loop(0, n_lat_tok // tk, lagged_block, 0)

    @pl.when(jnp.max(lag_ref[...]) > LAG_LIMIT)
    def _():
        reset()
        lax.fori_loop(0, n_tok // MXU_TILE,
                      lambda c, carry: (exact_chunk(pl.multiple_of(c * MXU_TILE, MXU_TILE)), carry)[1], 0)

    o_ref[0] = _diff_combine_t(acc_ref[...], lam_ref, gs_ref, lam_init)


def _attn_ctx_kernel(lam_ref, gs_ref, q_ref, k_ref, v_ref, o_any, o_ref, *, lam_init):
    del o_any
    q2t = _split_maps_t(q_ref[0].astype(F32).T.astype(BF16))
    vbt = _vt_block(v_ref[0].astype(F32).T.astype(BF16), CTX_LEN)
    m0 = jnp.full((1, 2 * CTX_LEN), -jnp.inf, F32)
    acc0 = jnp.zeros((V_DIM + SUM_ROWS, 2 * CTX_LEN), F32)
    _, acc = _softmax_block_t(q2t, k_ref[0], vbt, m0, acc0)
    o_ref[0] = _diff_combine_t(acc, lam_ref, gs_ref, lam_init)


def _key_block(n_tok):
    for mult in range(MAX_KEY_TILES, 0, -1):
        if n_tok % (mult * MXU_TILE) == 0:
            return mult * MXU_TILE
    raise ValueError(n_tok)


def _attn_call(q, k, v, lam_qk, g_subln, lam_init):
    bsz, n_tok, _ = q.shape
    n_lat_tok = n_tok - CTX_LEN
    tk = _key_block(n_lat_tok)
    tq_step = TQ
    n_col = 2 * TQ
    small = [pl.BlockSpec((4, HEAD_DIM), lambda *_: (0, 0)),
             pl.BlockSpec((V_DIM, 1), lambda *_: (0, 0))]
    gs = g_subln.reshape(V_DIM, 1)
    o = pl.pallas_call(
        functools.partial(_attn_kernel, n_lat_tok=n_lat_tok, tk=tk, lam_init=lam_init),
        grid=(bsz, N_HEADS, n_lat_tok // tq_step),
        in_specs=small + [
            pl.BlockSpec((1, tq_step, LANES), lambda b, h, t: (b, t, h)),
            pl.BlockSpec((1, n_tok, LANES), lambda b, h, t: (b, 0, h)),
            pl.BlockSpec((1, n_tok, LANES), lambda b, h, t: (b, 0, h)),
        ],
        out_specs=pl.BlockSpec((1, tq_step, V_DIM), lambda b, h, t: (b, t, h)),
        out_shape=jax.ShapeDtypeStruct((bsz, n_tok, V_WIDTH), BF16),
        scratch_shapes=[
            pltpu.VMEM((V_DIM, n_tok), BF16),
            pltpu.VMEM((LANES, n_col), BF16),
            pltpu.VMEM((1, n_col), F32),
            pltpu.VMEM((V_DIM + SUM_ROWS, n_col), F32),
            pltpu.VMEM((1, n_col), F32),
            pltpu.VMEM((tk, n_col), BF16),
        ],
        compiler_params=_cparams(3),
        name="diff_attn",
    )(lam_qk, gs, q, k, v)
    ctx_blk = n_lat_tok // CTX_LEN
    ctx_spec = pl.BlockSpec((1, CTX_LEN, LANES), lambda b, h: (b, ctx_blk, h))
    return pl.pallas_call(
        functools.partial(_attn_ctx_kernel, lam_init=lam_init),
        grid=(bsz, N_HEADS),
        in_specs=small + [ctx_spec, ctx_spec, ctx_spec, pl.BlockSpec(memory_space=pl.ANY)],
        out_specs=ctx_spec,
        out_shape=jax.ShapeDtypeStruct((bsz, n_tok, V_WIDTH), BF16),
        input_output_aliases={5: 0},
        compiler_params=_cparams(2),
        name="diff_attn_ctx",
    )(lam_qk, gs, q, k, v, o)


def _dft1_kernel(u_ref, c1_ref, s1_ref, twc_ref, tws_ref, o_ref):
    for j in range(DFT_GROUP):
        u = u_ref[0, :, j, :].astype(BF16)
        ar = jnp.dot(c1_ref[...], u, preferred_element_type=F32)
        ai = -jnp.dot(s1_ref[...], u, preferred_element_type=F32)
        twc = twc_ref[j]
        tws = tws_ref[j]
        for g in range(F_GROUPS):
            sl = slice(g * LANES, (g + 1) * LANES)
            o_ref[0, :, j, g * LANES:(g + 1) * LANES] = ar[:, sl] * twc + ai[:, sl] * tws
            o_ref[0, :, j, F_WIDTH + g * LANES:F_WIDTH + (g + 1) * LANES] = ai[:, sl] * twc - ar[:, sl] * tws


def _channel_dft(gr, gi, cc_ref, sc_ref, norm):
    grb = gr.astype(BF16)
    gib = gi.astype(BF16)
    out = []
    for g in range(F_GROUPS):
        sl = slice(g * LANES, (g + 1) * LANES)
        out.append((jnp.dot(grb[:, sl], cc_ref[...], preferred_element_type=F32)
                    + jnp.dot(gib[:, sl], sc_ref[...], preferred_element_type=F32)) * norm)
    return out


def _dft2_kernel(b_ref, c2_ref, s2_ref, cc_ref, sc_ref, o_ref, *, norm):
    for j in range(DFT_GROUP):
        bb = b_ref[0, j].astype(BF16)
        cb = jnp.dot(c2_ref[...], bb, preferred_element_type=F32)
        sb = jnp.dot(s2_ref[...], bb, preferred_element_type=F32)
        gr = cb[:, :F_WIDTH] + sb[:, F_WIDTH:]
        gi = cb[:, F_WIDTH:] - sb[:, :F_WIDTH]
        for g, z in enumerate(_channel_dft(gr, gi, cc_ref, sc_ref, norm)):
            o_ref[0, :, j, g * LANES:(g + 1) * LANES] = z


def _dft_ctx_kernel(f_ref, c_ref, s_ref, cc_ref, sc_ref, z_any, o_ref, *, norm):
    del z_any
    f = f_ref[0].astype(BF16)
    gr = jnp.dot(c_ref[...], f, preferred_element_type=F32)
    gi = -jnp.dot(s_ref[...], f, preferred_element_type=F32)
    for g, z in enumerate(_channel_dft(gr, gi, cc_ref, sc_ref, norm)):
        o_ref[0, :, g * LANES:(g + 1) * LANES] = z


def _dft_mats(n):
    idx = np.arange(n, dtype=np.float64)
    ang = 2.0 * np.pi * np.outer(idx, idx) / n
    return jnp.asarray(np.cos(ang), BF16), jnp.asarray(np.sin(ang), BF16)


def _fourier_call(f, n_lat_tok):
    bsz, n_tok, _ = f.shape
    n1 = n_lat_tok // DFT2
    full = lambda *shape: pl.BlockSpec(shape, lambda *_: (0,) * len(shape))
    c1, s1 = _dft_mats(n1)
    c2, s2 = _dft_mats(DFT2)
    cc, sc = _dft_mats(F_GROUP_DIM)
    cctx, sctx = _dft_mats(CTX_LEN)
    tw_ang = 2.0 * np.pi * np.outer(np.arange(DFT2, dtype=np.float64),
                                    np.arange(n1, dtype=np.float64)) / n_lat_tok
    twc = jnp.asarray(np.broadcast_to(np.cos(tw_ang)[:, :, None], (DFT2, n1, LANES)), F32)
    tws = jnp.asarray(np.broadcast_to(np.sin(tw_ang)[:, :, None], (DFT2, n1, LANES)), F32)

    fv = f.reshape(bsz, n_tok // DFT2, DFT2, F_WIDTH)
    mid = pl.pallas_call(
        _dft1_kernel,
        grid=(bsz, DFT2 // DFT_GROUP),
        in_specs=[
            pl.BlockSpec((1, n1, DFT_GROUP, F_WIDTH), lambda b, j: (b, 0, j, 0)),
            full(n1, n1), full(n1, n1),
            pl.BlockSpec((DFT_GROUP, n1, LANES), lambda b, j: (j, 0, 0)),
            pl.BlockSpec((DFT_GROUP, n1, LANES), lambda b, j: (j, 0, 0)),
        ],
        out_specs=pl.BlockSpec((1, n1, DFT_GROUP, 2 * F_WIDTH), lambda b, j: (b, 0, j, 0)),
        out_shape=jax.ShapeDtypeStruct((bsz, n1, DFT2, 2 * F_WIDTH), F32),
        compiler_params=_cparams(2),
        name="dft_outer",
    )(fv, c1, s1, twc, tws)

    z = pl.pallas_call(
        functools.partial(_dft2_kernel, norm=1.0 / math.sqrt(n_lat_tok * F_GROUP_DIM)),
        grid=(bsz, n1 // DFT_GROUP),
        in_specs=[
            pl.BlockSpec((1, DFT_GROUP, DFT2, 2 * F_WIDTH), lambda b, j: (b, j, 0, 0)),
            full(DFT2, DFT2), full(DFT2, DFT2),
            full(F_GROUP_DIM, F_GROUP_DIM), full(F_GROUP_DIM, F_GROUP_DIM),
        ],
        out_specs=pl.BlockSpec((1, DFT2, DFT_GROUP, F_WIDTH), lambda b, j: (b, 0, j, 0)),
        out_shape=jax.ShapeDtypeStruct((bsz, n_tok // n1, n1, F_WIDTH), F32),
        compiler_params=_cparams(2),
        name="dft_inner",
    )(mid, c2, s2, cc, sc)
    z = z.reshape(bsz, n_tok, F_WIDTH)

    ctx_blk = n_lat_tok // CTX_LEN
    z = pl.pallas_call(
        functools.partial(_dft_ctx_kernel, norm=1.0 / math.sqrt(CTX_LEN * F_GROUP_DIM)),
        grid=(bsz,),
        in_specs=[
            pl.BlockSpec((1, CTX_LEN, F_WIDTH), lambda b: (b, ctx_blk, 0)),
            full(CTX_LEN, CTX_LEN), full(CTX_LEN, CTX_LEN),
            full(F_GROUP_DIM, F_GROUP_DIM), full(F_GROUP_DIM, F_GROUP_DIM),
            pl.BlockSpec(memory_space=pl.ANY),
        ],
        out_specs=pl.BlockSpec((1, CTX_LEN, F_WIDTH), lambda b: (b, ctx_blk, 0)),
        out_shape=jax.ShapeDtypeStruct((bsz, n_tok, F_WIDTH), F32),
        input_output_aliases={5: 0},
        compiler_params=_cparams(1),
        name="dft_ctx",
    )(f, cctx, sctx, cc, sc, z)
    return z


def _merge_kernel(x_ref, o_ref, z_ref, mod_ref, g1_ref, g2_ref, wgate_ref, bgate_ref,
                  wao_ref, wfo_ref, wout_ref, wr_ref, br_ref, xo_ref, h2_ref, route_ref):
    x = x_ref[0]
    mod = mod_ref[0]
    hb = _rms_mod(x, g1_ref[...], mod[0:1], mod[1:2]).astype(BF16)
    gates = jax.nn.sigmoid(jnp.dot(hb, wgate_ref[...], preferred_element_type=F32) + bgate_ref[...])
    ya = jnp.dot(o_ref[0], wao_ref[...], preferred_element_type=F32)
    yf = jnp.dot(z_ref[0].astype(BF16), wfo_ref[...], preferred_element_type=F32)
    merged = gates[:, :D_MODEL] * ya + gates[:, D_MODEL:] * yf
    out = jnp.dot(merged.astype(BF16), wout_ref[...], preferred_element_type=F32)
    xn = x + mod[2:3] * out
    xo_ref[0] = xn
    h2 = _rms_mod(xn, g2_ref[...], mod[3:4], mod[4:5])
    h2_ref[0] = h2

    h2_hi = h2.astype(BF16)
    h2_lo = (h2 - h2_hi.astype(F32)).astype(BF16)
    part = jnp.dot(h2_hi, wr_ref[...], preferred_element_type=F32)
    logits = (part[:, :LANES] + part[:, LANES:]
              + jnp.dot(h2_lo, wr_ref[:, :LANES], preferred_element_type=F32))
    aff = jax.nn.sigmoid(logits)
    lane = lax.broadcasted_iota(jnp.int32, (TM, LANES), 1)
    lanef = lane.astype(F32)
    neg = jnp.full((TM, LANES), -jnp.inf, F32)
    sel = jnp.where(lane < N_EXPERTS, aff + br_ref[...], neg)
    grp = lane // EXPERTS_PER_GROUP
    best = i1b = i2b = None
    for g in range(N_GROUPS):
        sg = jnp.where(grp == g, sel, neg)
        m1 = jnp.max(sg, axis=1, keepdims=True)
        i1 = jnp.min(jnp.where(sg == m1, lanef, float(LANES)), axis=1, keepdims=True)
        sg2 = jnp.where(lanef == i1, neg, sg)
        m2 = jnp.max(sg2, axis=1, keepdims=True)
        i2 = jnp.min(jnp.where(sg2 == m2, lanef, float(LANES)), axis=1, keepdims=True)
        score = m1 + m2
        if g == 0:
            best, i1b, i2b = score, i1, i2
        else:
            upd = score > best
            best = jnp.where(upd, score, best)
            i1b = jnp.where(upd, i1, i1b)
            i2b = jnp.where(upd, i2, i2b)
    hit1 = lanef == i1b
    hit2 = lanef == i2b
    a1 = jnp.sum(jnp.where(hit1, aff, 0.0), axis=1, keepdims=True)
    a2 = jnp.sum(jnp.where(hit2, aff, 0.0), axis=1, keepdims=True)
    den = a1 + a2
    route_ref[0] = (jnp.where(lane == 0, i1b, 0.0) + jnp.where(lane == 1, i2b, 0.0)
                    + jnp.where(lane == 2, a1 / den, 0.0) + jnp.where(lane == 3, a2 / den, 0.0))


def _merge_call(x, o, z, mod, g1, g2, w_gate, b_gate, w_ao, w_fo, w_out, w_r, b_r, n_batch):
    bsz, n_tok, _ = x.shape
    nt = n_tok // TM
    tok = lambda w: pl.BlockSpec((1, TM, w), lambda b, t: (b, t, 0))
    full = lambda *shape: pl.BlockSpec(shape, lambda b, t: (0,) * len(shape))
    return pl.pallas_call(
        _merge_kernel,
        grid=(bsz, nt),
        in_specs=[
            tok(D_MODEL), tok(V_WIDTH), tok(F_WIDTH),
            pl.BlockSpec((1, N_MOD, D_MODEL),
                         lambda b, t: (jnp.where(t == nt - 1, n_batch, b), 0, 0)),
            full(1, D_MODEL), full(1, D_MODEL),
            full(D_MODEL, 2 * D_MODEL), full(1, 2 * D_MODEL),
            full(V_WIDTH, D_MODEL), full(F_WIDTH, D_MODEL), full(D_MODEL, D_MODEL),
            full(D_MODEL, 2 * LANES), full(1, LANES),
        ],
        out_specs=[tok(D_MODEL), tok(D_MODEL), tok(LANES)],
        out_shape=[
            jax.ShapeDtypeStruct((bsz, n_tok, D_MODEL), F32),
            jax.ShapeDtypeStruct((bsz, n_tok, D_MODEL), F32),
            jax.ShapeDtypeStruct((bsz, n_tok, LANES), F32),
        ],
        compiler_params=_cparams(2),
        name="merge_route",
    )(x, o, z, mod, g1, g2, w_gate, b_gate, w_ao, w_fo, w_out, w_r, b_r)


def _slot_copy(src_ref, src_row, dst_ref, dst_row, sem):
    return pltpu.make_async_copy(src_ref.at[pl.ds(src_row, 1)], dst_ref.at[pl.ds(dst_row, 1)], sem)


def _dispatch_kernel(pos_ref, h_ref, xs_in, xs_ref, sem):
    del xs_in

    def copy(r, k):
        return _slot_copy(h_ref, r, xs_ref, pos_ref[0, 0, 2 * r + k], sem)

    def issue(r, carry):
        copy(r, 0).start(priority=0)
        copy(r, 1).start(priority=1)
        return carry

    def drain(r, carry):
        copy(r, 0).wait()
        copy(r, 1).wait()
        return carry

    lax.fori_loop(0, TM, issue, 0, unroll=DMA_UNROLL)
    lax.fori_loop(0, TM, drain, 0, unroll=DMA_UNROLL)


def _experts_kernel(te_ref, nv_ref, xs_ref, wg_ref, wu_ref, wd_ref, ys_ref):
    del te_ref
    t = pl.program_id(0)

    @pl.when(t < nv_ref[0])
    def _():
        h = xs_ref[...].astype(BF16)
        a = jnp.dot(h, wg_ref[0], preferred_element_type=F32)
        u = jnp.dot(h, wu_ref[0], preferred_element_type=F32)
        he = (a * jax.nn.sigmoid(a) * u).astype(BF16)
        ys_ref[...] = jnp.dot(he, wd_ref[0], preferred_element_type=F32)

    @pl.when(t >= nv_ref[0])
    def _():
        ys_ref[...] = jnp.zeros(ys_ref.shape, F32)


def _combine_kernel(pos_ref, route_ref, ys_ref, y_ref, buf_ref, sem):
    def copy(r, k):
        return _slot_copy(ys_ref, pos_ref[0, 0, 2 * r + k], buf_ref.at[k], r, sem)

    def issue(r, carry):
        copy(r, 0).start(priority=0)
        copy(r, 1).start(priority=1)
        return carry

    def drain(r, carry):
        copy(r, 0).wait()
        copy(r, 1).wait()
        return carry

    lax.fori_loop(0, TM, issue, 0, unroll=DMA_UNROLL)
    lax.fori_loop(0, TM, drain, 0, unroll=DMA_UNROLL)
    route = route_ref[...]
    y_ref[...] = route[:, 2:3] * buf_ref[0] + route[:, 3:4] * buf_ref[1]


def _moe_call(h2, route, w_eg, w_eu, w_ed):
    n = h2.shape[0]
    n_slots = 2 * n
    n_rows = n_slots + N_EXPERTS * TMR
    n_tiles = n_rows // TMR

    e_slot = route[:, :2].astype(jnp.int32).reshape(n_slots)
    onehot = (e_slot[:, None] == jnp.arange(N_EXPERTS, dtype=jnp.int32)[None, :]).astype(jnp.int32)
    csum = jnp.cumsum(onehot, axis=0)
    rank = jnp.sum(csum * onehot, axis=1) - 1
    counts = csum[-1]
    padded = ((counts + TMR - 1) // TMR) * TMR
    ends = jnp.cumsum(padded)
    pos = jnp.sum(onehot * (ends - padded)[None, :], axis=1) + rank
    tile_expert = jnp.minimum(
        jnp.sum((jnp.arange(n_tiles, dtype=jnp.int32) * TMR)[:, None] >= ends[None, :], axis=1),
        N_EXPERTS - 1).astype(jnp.int32)
    n_valid = (ends[-1] // TMR).astype(jnp.int32).reshape(1)
    pos_tiles = pos.astype(jnp.int32).reshape(n // TM, 1, 2 * TM)

    pos_spec = pl.BlockSpec((1, 1, 2 * TM), lambda t: (t, 0, 0), memory_space=pltpu.SMEM)
    xs = pl.pallas_call(
        _dispatch_kernel,
        grid=(n // TM,),
        in_specs=[pos_spec,
                  pl.BlockSpec((TM, D_MODEL), lambda t: (t, 0)),
                  pl.BlockSpec(memory_space=pl.ANY)],
        out_specs=pl.BlockSpec(memory_space=pl.ANY),
        out_shape=jax.ShapeDtypeStruct((n_rows, D_MODEL), F32),
        scratch_shapes=[pltpu.SemaphoreType.DMA],
        input_output_aliases={2: 0},
        compiler_params=_cparams(1),
        name="dispatch",
    )(pos_tiles, h2, jnp.zeros((n_rows, D_MODEL), F32))

    w_map = lambda t, te, nv: (te[t], 0, 0)
    ys = pl.pallas_call(
        _experts_kernel,
        grid_spec=pltpu.PrefetchScalarGridSpec(
            num_scalar_prefetch=2,
            grid=(n_tiles,),
            in_specs=[
                pl.BlockSpec((TMR, D_MODEL), lambda t, te, nv: (t, 0)),
                pl.BlockSpec((1, D_MODEL, D_EXPERT), w_map),
                pl.BlockSpec((1, D_MODEL, D_EXPERT), w_map),
                pl.BlockSpec((1, D_EXPERT, D_MODEL), w_map),
            ],
            out_specs=pl.BlockSpec((TMR, D_MODEL), lambda t, te, nv: (t, 0)),
        ),
        out_shape=jax.ShapeDtypeStruct((n_rows, D_MODEL), F32),
        compiler_params=_cparams(1),
        name="experts",
    )(tile_expert, n_valid, xs, w_eg, w_eu, w_ed)

    return pl.pallas_call(
        _combine_kernel,
        grid=(n // TM,),
        in_specs=[pos_spec,
                  pl.BlockSpec((TM, LANES), lambda t: (t, 0)),
                  pl.BlockSpec(memory_space=pl.ANY)],
        out_specs=pl.BlockSpec((TM, D_MODEL), lambda t: (t, 0)),
        out_shape=jax.ShapeDtypeStruct((n, D_MODEL), F32),
        scratch_shapes=[pltpu.VMEM((2, TM, D_MODEL), F32), pltpu.SemaphoreType.DMA],
        compiler_params=_cparams(1),
        name="combine",
    )(pos_tiles, route, ys)


def _final_kernel(x_ref, y_ref, mod_ref, g_ref, o_ref):
    x = x_ref[0] + mod_ref[0][5:6] * y_ref[0]
    o_ref[0] = x * lax.rsqrt(jnp.mean(x * x, axis=-1, keepdims=True) + EPS) * g_ref[...]


def _final_call(x, y, mod, g_final, n_lat_tok):
    bsz = x.shape[0]
    tok = pl.BlockSpec((1, TM, D_MODEL), lambda b, t: (b, t, 0))
    return pl.pallas_call(
        _final_kernel,
        grid=(bsz, n_lat_tok // TM),
        in_specs=[tok, tok,
                  pl.BlockSpec((1, N_MOD, D_MODEL), lambda b, t: (b, 0, 0)),
                  pl.BlockSpec((1, D_MODEL), lambda b, t: (0, 0))],
        out_specs=tok,
        out_shape=jax.ShapeDtypeStruct((bsz, n_lat_tok, D_MODEL), F32),
        compiler_params=_cparams(2),
        name="final_norm",
    )(x, y, mod, g_final)


def _rope_tables(n_lat_tok):
    t = np.arange(n_lat_tok)
    pos = np.stack([t // GRID_W, t % GRID_W], axis=0).astype(np.float64)
    inv = ROPE_THETA ** (-np.arange(ROPE_HALF, dtype=np.float64) / ROPE_HALF)
    d = np.arange(LANES) % HEAD_DIM
    axis = d // (2 * ROPE_HALF)
    freq = d % ROPE_HALF
    ang = pos[axis, :].T * inv[freq][None, :]
    sign = np.where((d % (2 * ROPE_HALF)) < ROPE_HALF, -1.0, 1.0)
    cos = np.concatenate([np.cos(ang), np.ones((CTX_LEN, LANES))], axis=0)
    sin = np.concatenate([np.sin(ang) * sign[None, :], np.zeros((CTX_LEN, LANES))], axis=0)
    return jnp.asarray(cos, F32), jnp.asarray(sin, F32)


def kernel(x, c, ctx, c_ctx, w_ada, b_ada, g_norm1, w_in, lam_qk, g_subln, w_attn_out, w_four_out,
           w_gate, b_gate, w_out, g_norm2, w_router, b_router, w_e_gate, w_e_up, w_e_down, g_final):
    bsz, n_lat_tok, d = x.shape
    depth = w_ada.shape[0]
    assert d == D_MODEL and ctx.shape[1] == CTX_LEN and bsz < MOD_ROWS
    assert n_lat_tok % TQ == 0 and n_lat_tok % (DFT2 * 8) == 0
    n_tok = n_lat_tok + CTX_LEN

    cond = jnp.zeros((MOD_ROWS, D_MODEL), F32).at[:bsz].set(c).at[bsz].set(c_ctx)
    mods = _ada_call(cond, w_ada, b_ada).reshape(depth, MOD_ROWS, N_MOD, D_MODEL)
    cos_t, sin_t = _rope_tables(n_lat_tok)
    w_r = jnp.zeros((D_MODEL, LANES), F32).at[:, :N_EXPERTS].set(w_router)
    w_r_hi = w_r.astype(BF16)
    w_r = jnp.concatenate([w_r_hi, (w_r - w_r_hi.astype(F32)).astype(BF16)], axis=1)
    b_r = jnp.zeros((1, LANES), F32).at[0, :N_EXPERTS].set(b_router)

    xs = jnp.concatenate([x, ctx], axis=1)
    y = None
    for i in range(depth):
        lam_init = 0.8 - 0.6 * math.exp(-0.3 * i)
        xs, q, k, v, f = _proj_call(
            xs, y, mods[i - 1] if i else None, mods[i], g_norm1[i].reshape(1, D_MODEL),
            w_in[i].astype(BF16), cos_t, sin_t, bsz)
        o = _attn_call(q, k, v, lam_qk[i], g_subln[i], lam_init)
        z = _fourier_call(f, n_lat_tok)
        xs, h2, route = _merge_call(
            xs, o, z, mods[i], g_norm1[i].reshape(1, D_MODEL), g_norm2[i].reshape(1, D_MODEL),
            w_gate[i].astype(BF16), b_gate[i].reshape(1, 2 * D_MODEL), w_attn_out[i].astype(BF16),
            w_four_out[i].astype(BF16), w_out[i].astype(BF16), w_r, b_r, bsz)
        y = _moe_call(h2.reshape(bsz * n_tok, D_MODEL), route.reshape(bsz * n_tok, LANES),
                      w_e_gate[i].astype(BF16), w_e_up[i].astype(BF16), w_e_down[i].astype(BF16))
        y = y.reshape(bsz, n_tok, D_MODEL)
    return _final_call(xs, y, mods[depth - 1], g_final.reshape(1, D_MODEL), n_lat_tok)
```

```python
import functools
import math

import numpy as np
import jax
import jax.numpy as jnp
from jax import lax
from jax.experimental import pallas as pl
from jax.experimental.pallas import tpu as pltpu

F32 = jnp.float32
BF16 = jnp.bfloat16
HIGHEST = lax.Precision.HIGHEST

D_MODEL = 1024
GRID_W = 64
CTX_LEN = 256
N_HEADS = 8
HEAD_DIM = 64
V_DIM = 128
QK_WIDTH = 1024
V_WIDTH = 1024
F_GROUPS = 4
F_WIDTH = 512
F_GROUP_DIM = 128
IN_WIDTH = 3584
ROPE_THETA = 10000.0
ROPE_HALF = 16
N_EXPERTS = 16
N_GROUPS = 4
EXPERTS_PER_GROUP = 4
D_EXPERT = 512
EPS = 1e-6
N_MOD = 6
MOD_ROWS = 8

LANES = 128
TM = 256
TQ = 1024
MXU_TILE = 256
BF16_ROWS = 16
SUM_ROWS = 16
MAX_KEY_TILES = 16
QK_ROWS = 512
LAG_LIMIT = 100.0
TMR = 256
DMA_UNROLL = 8
ADA_TN = 512
DFT2 = 128
DFT_GROUP = 8
Q_SCALE = (HEAD_DIM ** -0.5) * math.log2(math.e)
VMEM_LIMIT = 56 * 1024 * 1024


def _cparams(n_axes):
    return pltpu.CompilerParams(dimension_semantics=("arbitrary",) * n_axes,
                                vmem_limit_bytes=VMEM_LIMIT)


def _rms_mod(x, g, shift, scale):
    y = x * lax.rsqrt(jnp.mean(x * x, axis=-1, keepdims=True) + EPS) * g
    return y * (1.0 + scale) + shift


def _ada_kernel(c_ref, w_ref, b_ref, o_ref):
    c = c_ref[...]
    s = c * jax.nn.sigmoid(c)
    o_ref[0] = jnp.dot(s, w_ref[0], preferred_element_type=F32, precision=HIGHEST) + b_ref[0]


def _ada_call(cond, w_ada, b_ada):
    depth = w_ada.shape[0]
    n_out = w_ada.shape[2]
    return pl.pallas_call(
        _ada_kernel,
        grid=(depth, n_out // ADA_TN),
        in_specs=[
            pl.BlockSpec((MOD_ROWS, D_MODEL), lambda l, j: (0, 0)),
            pl.BlockSpec((1, D_MODEL, ADA_TN), lambda l, j: (l, 0, j)),
            pl.BlockSpec((1, 1, ADA_TN), lambda l, j: (l, 0, j)),
        ],
        out_specs=pl.BlockSpec((1, MOD_ROWS, ADA_TN), lambda l, j: (l, 0, j)),
        out_shape=jax.ShapeDtypeStruct((depth, MOD_ROWS, n_out), F32),
        compiler_params=_cparams(2),
        name="ada",
    )(cond, w_ada, b_ada.reshape(depth, 1, n_out))


def _proj_kernel(*refs, has_y):
    if has_y:
        (x_ref, y_ref, pmod_ref, mod_ref, g1_ref, win_ref, cos_ref, sin_ref,
         xo_ref, q_ref, k_ref, v_ref, f_ref) = refs
        x = x_ref[0] + pmod_ref[0][5:6] * y_ref[0]
        xo_ref[0] = x
    else:
        (x_ref, mod_ref, g1_ref, win_ref, cos_ref, sin_ref,
         q_ref, k_ref, v_ref, f_ref) = refs
        x = x_ref[0]
    mod = mod_ref[0]
    hb = _rms_mod(x, g1_ref[...], mod[0:1], mod[1:2]).astype(BF16)
    cos = cos_ref[...]
    sin = sin_ref[...]
    lane = lax.broadcasted_iota(jnp.int32, (TM, LANES), 1)
    first = (lane % (2 * ROPE_HALF)) < ROPE_HALF

    def rope(xc):
        up = pltpu.roll(xc, LANES - ROPE_HALF, 1)
        dn = pltpu.roll(xc, ROPE_HALF, 1)
        return xc * cos + jnp.where(first, up, dn) * sin

    w2 = 2 * LANES
    for c in range(QK_WIDTH // w2):
        uq = jnp.dot(hb, win_ref[:, c * w2:(c + 1) * w2], preferred_element_type=F32)
        uk = jnp.dot(hb, win_ref[:, QK_WIDTH + c * w2:QK_WIDTH + (c + 1) * w2],
                     preferred_element_type=F32)
        for j in range(2):
            lo = c * w2 + j * LANES
            q_ref[0, :, lo:lo + LANES] = (rope(uq[:, j * LANES:(j + 1) * LANES]) * Q_SCALE).astype(BF16)
            k_ref[0, :, lo:lo + LANES] = rope(uk[:, j * LANES:(j + 1) * LANES]).astype(BF16)
    for c in range(V_WIDTH // w2):
        lo = 2 * QK_WIDTH + c * w2
        v_ref[0, :, c * w2:(c + 1) * w2] = jnp.dot(
            hb, win_ref[:, lo:lo + w2], preferred_element_type=F32).astype(BF16)
    for c in range(F_WIDTH // w2):
        lo = 2 * QK_WIDTH + V_WIDTH + c * w2
        f_ref[0, :, c * w2:(c + 1) * w2] = jnp.dot(
            hb, win_ref[:, lo:lo + w2], preferred_element_type=F32)


def _proj_call(x, y, pmod, mod, g1, w_in, cos_t, sin_t, n_batch):
    bsz, n_tok, _ = x.shape
    nt = n_tok // TM
    has_y = y is not None

    def mod_map(b, t):
        return (jnp.where(t == nt - 1, n_batch, b), 0, 0)

    tok = lambda w: pl.BlockSpec((1, TM, w), lambda b, t: (b, t, 0))
    modspec = pl.BlockSpec((1, N_MOD, D_MODEL), mod_map)
    in_specs = [tok(D_MODEL)]
    args = [x]
    if has_y:
        in_specs += [tok(D_MODEL), modspec]
        args += [y, pmod]
    in_specs += [
        modspec,
        pl.BlockSpec((1, D_MODEL), lambda b, t: (0, 0)),
        pl.BlockSpec((D_MODEL, IN_WIDTH), lambda b, t: (0, 0)),
        pl.BlockSpec((TM, LANES), lambda b, t: (t, 0)),
        pl.BlockSpec((TM, LANES), lambda b, t: (t, 0)),
    ]
    args += [mod, g1, w_in, cos_t, sin_t]
    out_specs = [tok(QK_WIDTH), tok(QK_WIDTH), tok(V_WIDTH), tok(F_WIDTH)]
    out_shape = [
        jax.ShapeDtypeStruct((bsz, n_tok, QK_WIDTH), BF16),
        jax.ShapeDtypeStruct((bsz, n_tok, QK_WIDTH), BF16),
        jax.ShapeDtypeStruct((bsz, n_tok, V_WIDTH), BF16),
        jax.ShapeDtypeStruct((bsz, n_tok, F_WIDTH), F32),
    ]
    if has_y:
        out_specs = [tok(D_MODEL)] + out_specs
        out_shape = [jax.ShapeDtypeStruct((bsz, n_tok, D_MODEL), F32)] + out_shape
    outs = pl.pallas_call(
        functools.partial(_proj_kernel, has_y=has_y),
        grid=(bsz, nt),
        in_specs=in_specs,
        out_specs=out_specs,
        out_shape=out_shape,
        compiler_params=_cparams(2),
        name="proj",
    )(*args)
    if has_y:
        return outs
    return [x] + list(outs)


def _split_maps_t(qt):
    row = lax.broadcasted_iota(jnp.int32, qt.shape, 0)
    zero = jnp.zeros_like(qt)
    return jnp.concatenate([jnp.where(row < HEAD_DIM, qt, zero),
                            jnp.where(row >= HEAD_DIM, qt, zero)], axis=1)


def _ones_rows(n_cols):
    row = lax.broadcasted_iota(jnp.int32, (SUM_ROWS, n_cols), 0)
    return jnp.where(row == 0, 1.0, 0.0).astype(BF16)


def _vt_block(vt, n_keys):
    return jnp.concatenate([vt, _ones_rows(n_keys)], axis=0)


def _softmax_block_t(q2t, kb, vbt_ext, m_prev, acc_prev):
    st = jnp.dot(kb, q2t, preferred_element_type=F32)
    m_new = jnp.maximum(m_prev, jnp.max(st, axis=0, keepdims=True))
    alpha = jnp.exp2(m_prev - m_new)
    pt = jnp.exp2(st - m_new).astype(BF16)
    acc = alpha * acc_prev + jnp.dot(vbt_ext, pt, preferred_element_type=F32)
    return m_new, acc


def _diff_combine_t(acc, lam_ref, gs_ref, lam_init):
    tq = acc.shape[1] // 2
    lp = lam_ref[...]
    lam = (jnp.exp(jnp.sum(lp[0:1] * lp[1:2], axis=1, keepdims=True))
           - jnp.exp(jnp.sum(lp[2:3] * lp[3:4], axis=1, keepdims=True)) + lam_init)
    pv = acc[:V_DIM]
    l = acc[V_DIM:V_DIM + 1]
    ot = pv[:, :tq] / l[:, :tq] - lam * (pv[:, tq:] / l[:, tq:])
    ot = ot * lax.rsqrt(jnp.mean(ot * ot, axis=0, keepdims=True) + EPS) * gs_ref[...]
    return (ot * (1.0 - lam_init)).T.astype(BF16)


def _attn_kernel(lam_ref, gs_ref, q_ref, k_ref, v_ref, o_ref, vt_ref, q2t_ref, m_ref, acc_ref,
                 lag_ref, pt_ref, *, n_lat_tok, tk, lam_init):
    n_tok = n_lat_tok + CTX_LEN

    @pl.when(pl.program_id(2) == 0)
    def _():
        for j in range(n_tok // MXU_TILE):
            sl = slice(j * MXU_TILE, (j + 1) * MXU_TILE)
            vt_ref[:, sl] = v_ref[0, sl, :].astype(F32).T.astype(BF16)

    q2t_ref[...] = _split_maps_t(q_ref[0].astype(F32).T.astype(BF16))

    def reset():
        m_ref[...] = jnp.full(m_ref.shape, -jnp.inf, F32)
        acc_ref[...] = jnp.zeros(acc_ref.shape, F32)

    def exact_chunk(off):
        m_new, acc = _softmax_block_t(
            q2t_ref[...], k_ref[0, pl.ds(off, MXU_TILE), :],
            _vt_block(vt_ref[:, pl.ds(off, MXU_TILE)], MXU_TILE), m_ref[...], acc_ref[...])
        m_ref[...] = m_new
        acc_ref[...] = acc

    def lagged_block(j, carry):
        off = pl.multiple_of(j * tk, tk)
        m_prev = m_ref[...]
        pmax = None
        for r in range(0, tk, QK_ROWS):
            st = jnp.dot(k_ref[0, pl.ds(off + r, QK_ROWS), :], q2t_ref[...],
                         preferred_element_type=F32)
            p = jnp.exp2(st - m_prev).astype(BF16)
            pt_ref[r:r + QK_ROWS, :] = p
            for t in range(0, QK_ROWS, BF16_ROWS):
                tile = p[t:t + BF16_ROWS, :]
                pmax = tile if pmax is None else jnp.maximum(pmax, tile)
        acc = acc_ref[...] + jnp.dot(_vt_block(vt_ref[:, pl.ds(off, tk)], tk), pt_ref[...],
                                     preferred_element_type=F32)
        excess = jnp.log2(jnp.max(pmax.astype(F32), axis=0, keepdims=True))
        m_new = m_prev + jnp.maximum(excess, 0.0)
        acc_ref[...] = acc * jnp.exp2(m_prev - m_new)
        m_ref[...] = m_new
        lag_ref[...] = jnp.maximum(lag_ref[...], excess)
        return carry

    reset()
    lag_ref[...] = jnp.full(lag_ref.shape, -jnp.inf, F32)
    exact_chunk(n_lat_tok)
    lax.fori_loop(0, n_lat_tok // tk, lagged_block, 0)

    @pl.when(jnp.max(lag_ref[...]) > LAG_LIMIT)
    def _():
        reset()
        lax.fori_loop(0, n_tok // MXU_TILE,
                      lambda c, carry: (exact_chunk(pl.multiple_of(c * MXU_TILE, MXU_TILE)), carry)[1], 0)

    o_ref[0] = _diff_combine_t(acc_ref[...], lam_ref, gs_ref, lam_init)


def _attn_ctx_kernel(lam_ref, gs_ref, q_ref, k_ref, v_ref, o_any, o_ref, *, lam_init):
    del o_any
    q2t = _split_maps_t(q_ref[0].astype(F32).T.astype(BF16))
    vbt = _vt_block(v_ref[0].astype(F32).T.astype(BF16), CTX_LEN)
    m0 = jnp.full((1, 2 * CTX_LEN), -jnp.inf, F32)
    acc0 = jnp.zeros((V_DIM + SUM_ROWS, 2 * CTX_LEN), F32)
    _, acc = _softmax_block_t(q2t, k_ref[0], vbt, m0, acc0)
    o_ref[0] = _diff_combine_t(acc, lam_ref, gs_ref, lam_init)


def _key_block(n_tok):
    for mult in range(MAX_KEY_TILES, 0, -1):
        if n_tok % (mult * MXU_TILE) == 0:
            return mult * MXU_TILE
    raise ValueError(n_tok)


def _attn_call(q, k, v, lam_qk, g_subln, lam_init):
    bsz, n_tok, _ = q.shape
    n_lat_tok = n_tok - CTX_LEN
    tk = _key_block(n_lat_tok)
    tq_step = TQ
    n_col = 2 * TQ
    small = [pl.BlockSpec((4, HEAD_DIM), lambda *_: (0, 0)),
             pl.BlockSpec((V_DIM, 1), lambda *_: (0, 0))]
    gs = g_subln.reshape(V_DIM, 1)
    o = pl.pallas_call(
        functools.partial(_attn_kernel, n_lat_tok=n_lat_tok, tk=tk, lam_init=lam_init),
        grid=(bsz, N_HEADS, n_lat_tok // tq_step),
        in_specs=small + [
            pl.BlockSpec((1, tq_step, LANES), lambda b, h, t: (b, t, h)),
            pl.BlockSpec((1, n_tok, LANES), lambda b, h, t: (b, 0, h)),
            pl.BlockSpec((1, n_tok, LANES), lambda b, h, t: (b, 0, h)),
        ],
        out_specs=pl.BlockSpec((1, tq_step, V_DIM), lambda b, h, t: (b, t, h)),
        out_shape=jax.ShapeDtypeStruct((bsz, n_tok, V_WIDTH), BF16),
        scratch_shapes=[
            pltpu.VMEM((V_DIM, n_tok), BF16),
            pltpu.VMEM((LANES, n_col), BF16),
            pltpu.VMEM((1, n_col), F32),
            pltpu.VMEM((V_DIM + SUM_ROWS, n_col), F32),
            pltpu.VMEM((1, n_col), F32),
            pltpu.VMEM((tk, n_col), BF16),
        ],
        compiler_params=_cparams(3),
        name="diff_attn",
    )(lam_qk, gs, q, k, v)
    ctx_blk = n_lat_tok // CTX_LEN
    ctx_spec = pl.BlockSpec((1, CTX_LEN, LANES), lambda b, h: (b, ctx_blk, h))
    return pl.pallas_call(
        functools.partial(_attn_ctx_kernel, lam_init=lam_init),
        grid=(bsz, N_HEADS),
        in_specs=small + [ctx_spec, ctx_spec, ctx_spec, pl.BlockSpec(memory_space=pl.ANY)],
        out_specs=ctx_spec,
        out_shape=jax.ShapeDtypeStruct((bsz, n_tok, V_WIDTH), BF16),
        input_output_aliases={5: 0},
        compiler_params=_cparams(2),
        name="diff_attn_ctx",
    )(lam_qk, gs, q, k, v, o)


def _dft1_kernel(u_ref, c1_ref, s1_ref, twc_ref, tws_ref, o_ref):
    n1 = u_ref.shape[1]
    rows = n1 * DFT_GROUP
    u = u_ref[0].reshape(rows, F_WIDTH).astype(BF16)
    ar = jnp.dot(c1_ref[...], u, preferred_element_type=F32)
    ai = -jnp.dot(s1_ref[...], u, preferred_element_type=F32)
    twc = twc_ref[0]
    tws = tws_ref[0]
    out = []
    for part in (lambda a_r, a_i: a_r * twc + a_i * tws, lambda a_r, a_i: a_i * twc - a_r * tws):
        for g in range(F_GROUPS):
            sl = slice(g * LANES, (g + 1) * LANES)
            out.append(part(ar[:, sl], ai[:, sl]))
    o_ref[0] = jnp.concatenate(out, axis=1).reshape(n1, DFT_GROUP, 2 * F_WIDTH)


def _channel_dft(gr, gi, cc_ref, sc_ref, norm):
    grb = gr.astype(BF16)
    gib = gi.astype(BF16)
    out = []
    for g in range(F_GROUPS):
        sl = slice(g * LANES, (g + 1) * LANES)
        out.append((jnp.dot(grb[:, sl], cc_ref[...], preferred_element_type=F32)
                    + jnp.dot(gib[:, sl], sc_ref[...], preferred_element_type=F32)) * norm)
    return out


def _dft2_kernel(b_ref, c2_ref, s2_ref, cc_ref, sc_ref, o_ref, *, norm):
    for j in range(DFT_GROUP):
        bb = b_ref[0, j].astype(BF16)
        cb = jnp.dot(c2_ref[...], bb, preferred_element_type=F32)
        sb = jnp.dot(s2_ref[...], bb, preferred_element_type=F32)
        gr = cb[:, :F_WIDTH] + sb[:, F_WIDTH:]
        gi = cb[:, F_WIDTH:] - sb[:, :F_WIDTH]
        for g, z in enumerate(_channel_dft(gr, gi, cc_ref, sc_ref, norm)):
            o_ref[0, :, j, g * LANES:(g + 1) * LANES] = z


def _dft_ctx_kernel(f_ref, c_ref, s_ref, cc_ref, sc_ref, z_any, o_ref, *, norm):
    del z_any
    f = f_ref[0].astype(BF16)
    gr = jnp.dot(c_ref[...], f, preferred_element_type=F32)
    gi = -jnp.dot(s_ref[...], f, preferred_element_type=F32)
    for g, z in enumerate(_channel_dft(gr, gi, cc_ref, sc_ref, norm)):
        o_ref[0, :, g * LANES:(g + 1) * LANES] = z


def _dft_mats(n, repeat=1):
    idx = np.arange(n, dtype=np.float64)
    ang = 2.0 * np.pi * np.outer(idx, idx) / n
    eye = np.eye(repeat)
    return (jnp.asarray(np.kron(np.cos(ang), eye), BF16), jnp.asarray(np.kron(np.sin(ang), eye), BF16))


def _fourier_call(f, n_lat_tok):
    bsz, n_tok, _ = f.shape
    n1 = n_lat_tok // DFT2
    full = lambda *shape: pl.BlockSpec(shape, lambda *_: (0,) * len(shape))
    c1, s1 = _dft_mats(n1, DFT_GROUP)
    c2, s2 = _dft_mats(DFT2)
    cc, sc = _dft_mats(F_GROUP_DIM)
    cctx, sctx = _dft_mats(CTX_LEN)
    tw_ang = (2.0 * np.pi / n_lat_tok) * np.einsum(
        "sj,k->skj", np.arange(DFT2, dtype=np.float64).reshape(DFT2 // DFT_GROUP, DFT_GROUP),
        np.arange(n1, dtype=np.float64)).reshape(DFT2 // DFT_GROUP, n1 * DFT_GROUP)
    twc = jnp.asarray(np.broadcast_to(np.cos(tw_ang)[:, :, None], tw_ang.shape + (LANES,)), F32)
    tws = jnp.asarray(np.broadcast_to(np.sin(tw_ang)[:, :, None], tw_ang.shape + (LANES,)), F32)

    fv = f.reshape(bsz, n_tok // DFT2, DFT2, F_WIDTH)
    mid = pl.pallas_call(
        _dft1_kernel,
        grid=(bsz, DFT2 // DFT_GROUP),
        in_specs=[
            pl.BlockSpec((1, n1, DFT_GROUP, F_WIDTH), lambda b, j: (b, 0, j, 0)),
            full(n1 * DFT_GROUP, n1 * DFT_GROUP), full(n1 * DFT_GROUP, n1 * DFT_GROUP),
            pl.BlockSpec((1, n1 * DFT_GROUP, LANES), lambda b, j: (j, 0, 0)),
            pl.BlockSpec((1, n1 * DFT_GROUP, LANES), lambda b, j: (j, 0, 0)),
        ],
        out_specs=pl.BlockSpec((1, n1, DFT_GROUP, 2 * F_WIDTH), lambda b, j: (b, 0, j, 0)),
        out_shape=jax.ShapeDtypeStruct((bsz, n1, DFT2, 2 * F_WIDTH), F32),
        compiler_params=_cparams(2),
        name="dft_outer",
    )(fv, c1, s1, twc, tws)

    z = pl.pallas_call(
        functools.partial(_dft2_kernel, norm=1.0 / math.sqrt(n_lat_tok * F_GROUP_DIM)),
        grid=(bsz, n1 // DFT_GROUP),
        in_specs=[
            pl.BlockSpec((1, DFT_GROUP, DFT2, 2 * F_WIDTH), lambda b, j: (b, j, 0, 0)),
            full(DFT2, DFT2), full(DFT2, DFT2),
            full(F_GROUP_DIM, F_GROUP_DIM), full(F_GROUP_DIM, F_GROUP_DIM),
        ],
        out_specs=pl.BlockSpec((1, DFT2, DFT_GROUP, F_WIDTH), lambda b, j: (b, 0, j, 0)),
        out_shape=jax.ShapeDtypeStruct((bsz, n_tok // n1, n1, F_WIDTH), F32),
        compiler_params=_cparams(2),
        name="dft_inner",
    )(mid, c2, s2, cc, sc)
    z = z.reshape(bsz, n_tok, F_WIDTH)

    ctx_blk = n_lat_tok // CTX_LEN
    z = pl.pallas_call(
        functools.partial(_dft_ctx_kernel, norm=1.0 / math.sqrt(CTX_LEN * F_GROUP_DIM)),
        grid=(bsz,),
        in_specs=[
            pl.BlockSpec((1, CTX_LEN, F_WIDTH), lambda b: (b, ctx_blk, 0)),
            full(CTX_LEN, CTX_LEN), full(CTX_LEN, CTX_LEN),
            full(F_GROUP_DIM, F_GROUP_DIM), full(F_GROUP_DIM, F_GROUP_DIM),
            pl.BlockSpec(memory_space=pl.ANY),
        ],
        out_specs=pl.BlockSpec((1, CTX_LEN, F_WIDTH), lambda b: (b, ctx_blk, 0)),
        out_shape=jax.ShapeDtypeStruct((bsz, n_tok, F_WIDTH), F32),
        input_output_aliases={5: 0},
        compiler_params=_cparams(1),
        name="dft_ctx",
    )(f, cctx, sctx, cc, sc, z)
    return z


def _merge_kernel(x_ref, o_ref, z_ref, mod_ref, g1_ref, g2_ref, wgate_ref, bgate_ref,
                  wao_ref, wfo_ref, wout_ref, wr_ref, br_ref, xo_ref, h2_ref, route_ref):
    x = x_ref[0]
    mod = mod_ref[0]
    hb = _rms_mod(x, g1_ref[...], mod[0:1], mod[1:2]).astype(BF16)
    gates = jax.nn.sigmoid(jnp.dot(hb, wgate_ref[...], preferred_element_type=F32) + bgate_ref[...])
    ya = jnp.dot(o_ref[0], wao_ref[...], preferred_element_type=F32)
    yf = jnp.dot(z_ref[0].astype(BF16), wfo_ref[...], preferred_element_type=F32)
    merged = gates[:, :D_MODEL] * ya + gates[:, D_MODEL:] * yf
    out = jnp.dot(merged.astype(BF16), wout_ref[...], preferred_element_type=F32)
    xn = x + mod[2:3] * out
    xo_ref[0] = xn
    h2 = _rms_mod(xn, g2_ref[...], mod[3:4], mod[4:5])
    h2_ref[0] = h2

    h2_hi = h2.astype(BF16)
    h2_lo = (h2 - h2_hi.astype(F32)).astype(BF16)
    part = jnp.dot(h2_hi, wr_ref[...], preferred_element_type=F32)
    logits = (part[:, :LANES] + part[:, LANES:]
              + jnp.dot(h2_lo, wr_ref[:, :LANES], preferred_element_type=F32))
    aff = jax.nn.sigmoid(logits)
    lane = lax.broadcasted_iota(jnp.int32, (TM, LANES), 1)
    lanef = lane.astype(F32)
    neg = jnp.full((TM, LANES), -jnp.inf, F32)
    sel = jnp.where(lane < N_EXPERTS, aff + br_ref[...], neg)
    grp = lane // EXPERTS_PER_GROUP
    best = i1b = i2b = None
    for g in range(N_GROUPS):
        sg = jnp.where(grp == g, sel, neg)
        m1 = jnp.max(sg, axis=1, keepdims=True)
        i1 = jnp.min(jnp.where(sg == m1, lanef, float(LANES)), axis=1, keepdims=True)
        sg2 = jnp.where(lanef == i1, neg, sg)
        m2 = jnp.max(sg2, axis=1, keepdims=True)
        i2 = jnp.min(jnp.where(sg2 == m2, lanef, float(LANES)), axis=1, keepdims=True)
        score = m1 + m2
        if g == 0:
            best, i1b, i2b = score, i1, i2
        else:
            upd = score > best
            best = jnp.where(upd, score, best)
            i1b = jnp.where(upd, i1, i1b)
            i2b = jnp.where(upd, i2, i2b)
    hit1 = lanef == i1b
    hit2 = lanef == i2b
    a1 = jnp.sum(jnp.where(hit1, aff, 0.0), axis=1, keepdims=True)
    a2 = jnp.sum(jnp.where(hit2, aff, 0.0), axis=1, keepdims=True)
    den = a1 + a2
    route_ref[0] = (jnp.where(lane == 0, i1b, 0.0) + jnp.where(lane == 1, i2b, 0.0)
                    + jnp.where(lane == 2, a1 / den, 0.0) + jnp.where(lane == 3, a2 / den, 0.0))


def _merge_call(x, o, z, mod, g1, g2, w_gate, b_gate, w_ao, w_fo, w_out, w_r, b_r, n_batch):
    bsz, n_tok, _ = x.shape
    nt = n_tok // TM
    tok = lambda w: pl.BlockSpec((1, TM, w), lambda b, t: (b, t, 0))
    full = lambda *shape: pl.BlockSpec(shape, lambda b, t: (0,) * len(shape))
    return pl.pallas_call(
        _merge_kernel,
        grid=(bsz, nt),
        in_specs=[
            tok(D_MODEL), tok(V_WIDTH), tok(F_WIDTH),
            pl.BlockSpec((1, N_MOD, D_MODEL),
                         lambda b, t: (jnp.where(t == nt - 1, n_batch, b), 0, 0)),
            full(1, D_MODEL), full(1, D_MODEL),
            full(D_MODEL, 2 * D_MODEL), full(1, 2 * D_MODEL),
            full(V_WIDTH, D_MODEL), full(F_WIDTH, D_MODEL), full(D_MODEL, D_MODEL),
            full(D_MODEL, 2 * LANES), full(1, LANES),
        ],
        out_specs=[tok(D_MODEL), tok(D_MODEL), tok(LANES)],
        out_shape=[
            jax.ShapeDtypeStruct((bsz, n_tok, D_MODEL), F32),
            jax.ShapeDtypeStruct((bsz, n_tok, D_MODEL), F32),
            jax.ShapeDtypeStruct((bsz, n_tok, LANES), F32),
        ],
        compiler_params=_cparams(2),
        name="merge_route",
    )(x, o, z, mod, g1, g2, w_gate, b_gate, w_ao, w_fo, w_out, w_r, b_r)


def _slot_copy(src_ref, src_row, dst_ref, dst_row, sem):
    return pltpu.make_async_copy(src_ref.at[pl.ds(src_row, 1)], dst_ref.at[pl.ds(dst_row, 1)], sem)


def _dispatch_kernel(pos_ref, h_ref, xs_in, xs_ref, sem):
    del xs_in

    def copy(r, k):
        return _slot_copy(h_ref, r, xs_ref, pos_ref[0, 0, 2 * r + k], sem)

    def issue(r, carry):
        copy(r, 0).start(priority=0)
        copy(r, 1).start(priority=1)
        return carry

    def drain(r, carry):
        copy(r, 0).wait()
        copy(r, 1).wait()
        return carry

    lax.fori_loop(0, TM, issue, 0, unroll=DMA_UNROLL)
    lax.fori_loop(0, TM, drain, 0, unroll=DMA_UNROLL)


def _experts_kernel(te_ref, nv_ref, xs_ref, wg_ref, wu_ref, wd_ref, ys_ref):
    del te_ref
    t = pl.program_id(0)

    @pl.when(t < nv_ref[0])
    def _():
        h = xs_ref[...].astype(BF16)
        a = jnp.dot(h, wg_ref[0], preferred_element_type=F32)
        u = jnp.dot(h, wu_ref[0], preferred_element_type=F32)
        he = (a * jax.nn.sigmoid(a) * u).astype(BF16)
        ys_ref[...] = jnp.dot(he, wd_ref[0], preferred_element_type=F32)

    @pl.when(t >= nv_ref[0])
    def _():
        ys_ref[...] = jnp.zeros(ys_ref.shape, F32)


def _combine_kernel(pos_ref, route_ref, ys_ref, y_ref, buf_ref, sem):
    def copy(r, k):
        return _slot_copy(ys_ref, pos_ref[0, 0, 2 * r + k], buf_ref.at[k], r, sem)

    def issue(r, carry):
        copy(r, 0).start(priority=0)
        copy(r, 1).start(priority=1)
        return carry

    def drain(r, carry):
        copy(r, 0).wait()
        copy(r, 1).wait()
        return carry

    lax.fori_loop(0, TM, issue, 0, unroll=DMA_UNROLL)
    lax.fori_loop(0, TM, drain, 0, unroll=DMA_UNROLL)
    route = route_ref[...]
    y_ref[...] = route[:, 2:3] * buf_ref[0] + route[:, 3:4] * buf_ref[1]


def _moe_call(h2, route, w_eg, w_eu, w_ed):
    n = h2.shape[0]
    n_slots = 2 * n
    n_rows = n_slots + N_EXPERTS * TMR
    n_tiles = n_rows // TMR

    e_slot = route[:, :2].astype(jnp.int32).reshape(n_slots)
    onehot = (e_slot[:, None] == jnp.arange(N_EXPERTS, dtype=jnp.int32)[None, :]).astype(jnp.int32)
    csum = jnp.cumsum(onehot, axis=0)
    rank = jnp.sum(csum * onehot, axis=1) - 1
    counts = csum[-1]
    padded = ((counts + TMR - 1) // TMR) * TMR
    ends = jnp.cumsum(padded)
    pos = jnp.sum(onehot * (ends - padded)[None, :], axis=1) + rank
    tile_expert = jnp.minimum(
        jnp.sum((jnp.arange(n_tiles, dtype=jnp.int32) * TMR)[:, None] >= ends[None, :], axis=1),
        N_EXPERTS - 1).astype(jnp.int32)
    n_valid = (ends[-1] // TMR).astype(jnp.int32).reshape(1)
    pos_tiles = pos.astype(jnp.int32).reshape(n // TM, 1, 2 * TM)

    pos_spec = pl.BlockSpec((1, 1, 2 * TM), lambda t: (t, 0, 0), memory_space=pltpu.SMEM)
    xs = pl.pallas_call(
        _dispatch_kernel,
        grid=(n // TM,),
        in_specs=[pos_spec,
                  pl.BlockSpec((TM, D_MODEL), lambda t: (t, 0)),
                  pl.BlockSpec(memory_space=pl.ANY)],
        out_specs=pl.BlockSpec(memory_space=pl.ANY),
        out_shape=jax.ShapeDtypeStruct((n_rows, D_MODEL), F32),
        scratch_shapes=[pltpu.SemaphoreType.DMA],
        input_output_aliases={2: 0},
        compiler_params=_cparams(1),
        name="dispatch",
    )(pos_tiles, h2, jnp.zeros((n_rows, D_MODEL), F32))

    w_map = lambda t, te, nv: (te[t], 0, 0)
    ys = pl.pallas_call(
        _experts_kernel,
        grid_spec=pltpu.PrefetchScalarGridSpec(
            num_scalar_prefetch=2,
            grid=(n_tiles,),
            in_specs=[
                pl.BlockSpec((TMR, D_MODEL), lambda t, te, nv: (t, 0)),
                pl.BlockSpec((1, D_MODEL, D_EXPERT), w_map),
                pl.BlockSpec((1, D_MODEL, D_EXPERT), w_map),
                pl.BlockSpec((1, D_EXPERT, D_MODEL), w_map),
            ],
            out_specs=pl.BlockSpec((TMR, D_MODEL), lambda t, te, nv: (t, 0)),
        ),
        out_shape=jax.ShapeDtypeStruct((n_rows, D_MODEL), F32),
        compiler_params=_cparams(1),
        name="experts",
    )(tile_expert, n_valid, xs, w_eg, w_eu, w_ed)

    return pl.pallas_call(
        _combine_kernel,
        grid=(n // TM,),
        in_specs=[pos_spec,
                  pl.BlockSpec((TM, LANES), lambda t: (t, 0)),
                  pl.BlockSpec(memory_space=pl.ANY)],
        out_specs=pl.BlockSpec((TM, D_MODEL), lambda t: (t, 0)),
        out_shape=jax.ShapeDtypeStruct((n, D_MODEL), F32),
        scratch_shapes=[pltpu.VMEM((2, TM, D_MODEL), F32), pltpu.SemaphoreType.DMA],
        compiler_params=_cparams(1),
        name="combine",
    )(pos_tiles, route, ys)


def _final_kernel(x_ref, y_ref, mod_ref, g_ref, o_ref):
    x = x_ref[0] + mod_ref[0][5:6] * y_ref[0]
    o_ref[0] = x * lax.rsqrt(jnp.mean(x * x, axis=-1, keepdims=True) + EPS) * g_ref[...]


def _final_call(x, y, mod, g_final, n_lat_tok):
    bsz = x.shape[0]
    tok = pl.BlockSpec((1, TM, D_MODEL), lambda b, t: (b, t, 0))
    return pl.pallas_call(
        _final_kernel,
        grid=(bsz, n_lat_tok // TM),
        in_specs=[tok, tok,
                  pl.BlockSpec((1, N_MOD, D_MODEL), lambda b, t: (b, 0, 0)),
                  pl.BlockSpec((1, D_MODEL), lambda b, t: (0, 0))],
        out_specs=tok,
        out_shape=jax.ShapeDtypeStruct((bsz, n_lat_tok, D_MODEL), F32),
        compiler_params=_cparams(2),
        name="final_norm",
    )(x, y, mod, g_final)


def _rope_tables(n_lat_tok):
    t = np.arange(n_lat_tok)
    pos = np.stack([t // GRID_W, t % GRID_W], axis=0).astype(np.float64)
    inv = ROPE_THETA ** (-np.arange(ROPE_HALF, dtype=np.float64) / ROPE_HALF)
    d = np.arange(LANES) % HEAD_DIM
    axis = d // (2 * ROPE_HALF)
    freq = d % ROPE_HALF
    ang = pos[axis, :].T * inv[freq][None, :]
    sign = np.where((d % (2 * ROPE_HALF)) < ROPE_HALF, -1.0, 1.0)
    cos = np.concatenate([np.cos(ang), np.ones((CTX_LEN, LANES))], axis=0)
    sin = np.concatenate([np.sin(ang) * sign[None, :], np.zeros((CTX_LEN, LANES))], axis=0)
    return jnp.asarray(cos, F32), jnp.asarray(sin, F32)


def kernel(x, c, ctx, c_ctx, w_ada, b_ada, g_norm1, w_in, lam_qk, g_subln, w_attn_out, w_four_out,
           w_gate, b_gate, w_out, g_norm2, w_router, b_router, w_e_gate, w_e_up, w_e_down, g_final):
    bsz, n_lat_tok, d = x.shape
    depth = w_ada.shape[0]
    assert d == D_MODEL and ctx.shape[1] == CTX_LEN and bsz < MOD_ROWS
    assert n_lat_tok % TQ == 0 and n_lat_tok % (DFT2 * 8) == 0
    n_tok = n_lat_tok + CTX_LEN

    cond = jnp.zeros((MOD_ROWS, D_MODEL), F32).at[:bsz].set(c).at[bsz].set(c_ctx)
    mods = _ada_call(cond, w_ada, b_ada).reshape(depth, MOD_ROWS, N_MOD, D_MODEL)
    cos_t, sin_t = _rope_tables(n_lat_tok)
    w_r = jnp.zeros((D_MODEL, LANES), F32).at[:, :N_EXPERTS].set(w_router)
    w_r_hi = w_r.astype(BF16)
    w_r = jnp.concatenate([w_r_hi, (w_r - w_r_hi.astype(F32)).astype(BF16)], axis=1)
    b_r = jnp.zeros((1, LANES), F32).at[0, :N_EXPERTS].set(b_router)

    xs = jnp.concatenate([x, ctx], axis=1)
    y = None
    for i in range(depth):
        lam_init = 0.8 - 0.6 * math.exp(-0.3 * i)
        xs, q, k, v, f = _proj_call(
            xs, y, mods[i - 1] if i else None, mods[i], g_norm1[i].reshape(1, D_MODEL),
            w_in[i].astype(BF16), cos_t, sin_t, bsz)
        o = _attn_call(q, k, v, lam_qk[i], g_subln[i], lam_init)
        z = _fourier_call(f, n_lat_tok)
        xs, h2, route = _merge_call(
            xs, o, z, mods[i], g_norm1[i].reshape(1, D_MODEL), g_norm2[i].reshape(1, D_MODEL),
            w_gate[i].astype(BF16), b_gate[i].reshape(1, 2 * D_MODEL), w_attn_out[i].astype(BF16),
            w_four_out[i].astype(BF16), w_out[i].astype(BF16), w_r, b_r, bsz)
        y = _moe_call(h2.reshape(bsz * n_tok, D_MODEL), route.reshape(bsz * n_tok, LANES),
                      w_e_gate[i].astype(BF16), w_e_up[i].astype(BF16), w_e_down[i].astype(BF16))
        y = y.reshape(bsz, n_tok, D_MODEL)
    return _final_call(xs, y, mods[depth - 1], g_final.reshape(1, D_MODEL), n_lat_tok)
```

```python
import functools
import math

import numpy as np
import jax
import jax.numpy as jnp
from jax import lax
from jax.experimental import pallas as pl
from jax.experimental.pallas import tpu as pltpu

F32 = jnp.float32
BF16 = jnp.bfloat16
HIGHEST = lax.Precision.HIGHEST

D_MODEL = 1024
GRID_W = 64
CTX_LEN = 256
N_HEADS = 8
HEAD_DIM = 64
V_DIM = 128
QK_WIDTH = 1024
V_WIDTH = 1024
F_GROUPS = 4
F_WIDTH = 512
F_GROUP_DIM = 128
IN_WIDTH = 3584
ROPE_THETA = 10000.0
ROPE_HALF = 16
N_EXPERTS = 16
N_GROUPS = 4
EXPERTS_PER_GROUP = 4
D_EXPERT = 512
EPS = 1e-6
N_MOD = 6
MOD_ROWS = 8

LANES = 128
TM = 256
TQ = 1024
MXU_TILE = 256
BF16_ROWS = 16
SUM_ROWS = 16
MAX_KEY_TILES = 16
QK_ROWS = 512
LAG_LIMIT = 100.0
TMR = 256
TSLOT = 512
DMA_UNROLL = 8
ADA_TN = 512
DFT2 = 128
DFT_GROUP = 8
Q_SCALE = (HEAD_DIM ** -0.5) * math.log2(math.e)
VMEM_LIMIT = 56 * 1024 * 1024


def _cparams(n_axes):
    return pltpu.CompilerParams(dimension_semantics=("arbitrary",) * n_axes,
                                vmem_limit_bytes=VMEM_LIMIT)


def _rms_mod(x, g, shift, scale):
    y = x * lax.rsqrt(jnp.mean(x * x, axis=-1, keepdims=True) + EPS) * g
    return y * (1.0 + scale) + shift


def _ada_kernel(c_ref, w_ref, b_ref, o_ref):
    c = c_ref[...]
    s = c * jax.nn.sigmoid(c)
    o_ref[0] = jnp.dot(s, w_ref[0], preferred_element_type=F32, precision=HIGHEST) + b_ref[0]


def _ada_call(cond, w_ada, b_ada):
    depth = w_ada.shape[0]
    n_out = w_ada.shape[2]
    return pl.pallas_call(
        _ada_kernel,
        grid=(depth, n_out // ADA_TN),
        in_specs=[
            pl.BlockSpec((MOD_ROWS, D_MODEL), lambda l, j: (0, 0)),
            pl.BlockSpec((1, D_MODEL, ADA_TN), lambda l, j: (l, 0, j)),
            pl.BlockSpec((1, 1, ADA_TN), lambda l, j: (l, 0, j)),
        ],
        out_specs=pl.BlockSpec((1, MOD_ROWS, ADA_TN), lambda l, j: (l, 0, j)),
        out_shape=jax.ShapeDtypeStruct((depth, MOD_ROWS, n_out), F32),
        compiler_params=_cparams(2),
        name="ada",
    )(cond, w_ada, b_ada.reshape(depth, 1, n_out))


def _proj_kernel(*refs, has_y):
    if has_y:
        (x_ref, y_ref, pmod_ref, mod_ref, g1_ref, win_ref, cos_ref, sin_ref,
         xo_ref, q_ref, k_ref, v_ref, f_ref) = refs
        x = x_ref[0] + pmod_ref[0][5:6] * y_ref[0]
        xo_ref[0] = x
    else:
        (x_ref, mod_ref, g1_ref, win_ref, cos_ref, sin_ref,
         q_ref, k_ref, v_ref, f_ref) = refs
        x = x_ref[0]
    mod = mod_ref[0]
    hb = _rms_mod(x, g1_ref[...], mod[0:1], mod[1:2]).astype(BF16)
    cos = cos_ref[...]
    sin = sin_ref[...]
    lane = lax.broadcasted_iota(jnp.int32, (TM, LANES), 1)
    first = (lane % (2 * ROPE_HALF)) < ROPE_HALF

    def rope(xc):
        up = pltpu.roll(xc, LANES - ROPE_HALF, 1)
        dn = pltpu.roll(xc, ROPE_HALF, 1)
        return xc * cos + jnp.where(first, up, dn) * sin

    w2 = 2 * LANES
    for c in range(QK_WIDTH // w2):
        uq = jnp.dot(hb, win_ref[:, c * w2:(c + 1) * w2], preferred_element_type=F32)
        uk = jnp.dot(hb, win_ref[:, QK_WIDTH + c * w2:QK_WIDTH + (c + 1) * w2],
                     preferred_element_type=F32)
        for j in range(2):
            lo = c * w2 + j * LANES
            q_ref[0, :, lo:lo + LANES] = (rope(uq[:, j * LANES:(j + 1) * LANES]) * Q_SCALE).astype(BF16)
            k_ref[0, :, lo:lo + LANES] = rope(uk[:, j * LANES:(j + 1) * LANES]).astype(BF16)
    for c in range(V_WIDTH // w2):
        lo = 2 * QK_WIDTH + c * w2
        v_ref[0, :, c * w2:(c + 1) * w2] = jnp.dot(
            hb, win_ref[:, lo:lo + w2], preferred_element_type=F32).astype(BF16)
    for c in range(F_WIDTH // w2):
        lo = 2 * QK_WIDTH + V_WIDTH + c * w2
        f_ref[0, :, c * w2:(c + 1) * w2] = jnp.dot(
            hb, win_ref[:, lo:lo + w2], preferred_element_type=F32)


def _proj_call(x, y, pmod, mod, g1, w_in, cos_t, sin_t, n_batch):
    bsz, n_tok, _ = x.shape
    nt = n_tok // TM
    has_y = y is not None

    def mod_map(b, t):
        return (jnp.where(t == nt - 1, n_batch, b), 0, 0)

    tok = lambda w: pl.BlockSpec((1, TM, w), lambda b, t: (b, t, 0))
    modspec = pl.BlockSpec((1, N_MOD, D_MODEL), mod_map)
    in_specs = [tok(D_MODEL)]
    args = [x]
    if has_y:
        in_specs += [tok(D_MODEL), modspec]
        args += [y, pmod]
    in_specs += [
        modspec,
        pl.BlockSpec((1, D_MODEL), lambda b, t: (0, 0)),
        pl.BlockSpec((D_MODEL, IN_WIDTH), lambda b, t: (0, 0)),
        pl.BlockSpec((TM, LANES), lambda b, t: (t, 0)),
        pl.BlockSpec((TM, LANES), lambda b, t: (t, 0)),
    ]
    args += [mod, g1, w_in, cos_t, sin_t]
    out_specs = [tok(QK_WIDTH), tok(QK_WIDTH), tok(V_WIDTH), tok(F_WIDTH)]
    out_shape = [
        jax.ShapeDtypeStruct((bsz, n_tok, QK_WIDTH), BF16),
        jax.ShapeDtypeStruct((bsz, n_tok, QK_WIDTH), BF16),
        jax.ShapeDtypeStruct((bsz, n_tok, V_WIDTH), BF16),
        jax.ShapeDtypeStruct((bsz, n_tok, F_WIDTH), F32),
    ]
    if has_y:
        out_specs = [tok(D_MODEL)] + out_specs
        out_shape = [jax.ShapeDtypeStruct((bsz, n_tok, D_MODEL), F32)] + out_shape
    outs = pl.pallas_call(
        functools.partial(_proj_kernel, has_y=has_y),
        grid=(bsz, nt),
        in_specs=in_specs,
        out_specs=out_specs,
        out_shape=out_shape,
        compiler_params=_cparams(2),
        name="proj",
    )(*args)
    if has_y:
        return outs
    return [x] + list(outs)


def _split_maps_t(qt):
    row = lax.broadcasted_iota(jnp.int32, qt.shape, 0)
    zero = jnp.zeros_like(qt)
    return jnp.concatenate([jnp.where(row < HEAD_DIM, qt, zero),
                            jnp.where(row >= HEAD_DIM, qt, zero)], axis=1)


def _ones_rows(n_cols):
    row = lax.broadcasted_iota(jnp.int32, (SUM_ROWS, n_cols), 0)
    return jnp.where(row == 0, 1.0, 0.0).astype(BF16)


def _vt_block(vt, n_keys):
    return jnp.concatenate([vt, _ones_rows(n_keys)], axis=0)


def _softmax_block_t(q2t, kb, vbt_ext, m_prev, acc_prev):
    st = jnp.dot(kb, q2t, preferred_element_type=F32)
    m_new = jnp.maximum(m_prev, jnp.max(st, axis=0, keepdims=True))
    alpha = jnp.exp2(m_prev - m_new)
    pt = jnp.exp2(st - m_new).astype(BF16)
    acc = alpha * acc_prev + jnp.dot(vbt_ext, pt, preferred_element_type=F32)
    return m_new, acc


def _diff_combine_t(acc, lam_ref, gs_ref, lam_init):
    tq = acc.shape[1] // 2
    lp = lam_ref[...]
    lam = (jnp.exp(jnp.sum(lp[0:1] * lp[1:2], axis=1, keepdims=True))
           - jnp.exp(jnp.sum(lp[2:3] * lp[3:4], axis=1, keepdims=True)) + lam_init)
    pv = acc[:V_DIM]
    l = acc[V_DIM:V_DIM + 1]
    ot = pv[:, :tq] / l[:, :tq] - lam * (pv[:, tq:] / l[:, tq:])
    ot = ot * lax.rsqrt(jnp.mean(ot * ot, axis=0, keepdims=True) + EPS) * gs_ref[...]
    return (ot * (1.0 - lam_init)).T.astype(BF16)


def _attn_kernel(lam_ref, gs_ref, q_ref, k_ref, v_ref, o_ref, vt_ref, q2t_ref, m_ref, acc_ref,
                 lag_ref, pt_ref, *, n_lat_tok, tk, lam_init):
    n_tok = n_lat_tok + CTX_LEN

    @pl.when(pl.program_id(2) == 0)
    def _():
        for j in range(n_tok // MXU_TILE):
            sl = slice(j * MXU_TILE, (j + 1) * MXU_TILE)
            vt_ref[:, sl] = v_ref[0, sl, :].astype(F32).T.astype(BF16)

    q2t_ref[...] = _split_maps_t(q_ref[0].astype(F32).T.astype(BF16))

    def reset():
        m_ref[...] = jnp.full(m_ref.shape, -jnp.inf, F32)
        acc_ref[...] = jnp.zeros(acc_ref.shape, F32)

    def exact_chunk(off):
        m_new, acc = _softmax_block_t(
            q2t_ref[...], k_ref[0, pl.ds(off, MXU_TILE), :],
            _vt_block(vt_ref[:, pl.ds(off, MXU_TILE)], MXU_TILE), m_ref[...], acc_ref[...])
        m_ref[...] = m_new
        acc_ref[...] = acc

    def lagged_block(j, carry):
        off = pl.multiple_of(j * tk, tk)
        m_prev = m_ref[...]
        pmax = None
        for r in range(0, tk, QK_ROWS):
            st = jnp.dot(k_ref[0, pl.ds(off + r, QK_ROWS), :], q2t_ref[...],
                         preferred_element_type=F32)
            p = jnp.exp2(st - m_prev).astype(BF16)
            pt_ref[r:r + QK_ROWS, :] = p
            for t in range(0, QK_ROWS, BF16_ROWS):
                tile = p[t:t + BF16_ROWS, :]
                pmax = tile if pmax is None else jnp.maximum(pmax, tile)
        acc = acc_ref[...] + jnp.dot(_vt_block(vt_ref[:, pl.ds(off, tk)], tk), pt_ref[...],
                                     preferred_element_type=F32)
        excess = jnp.log2(jnp.max(pmax.astype(F32), axis=0, keepdims=True))
        m_new = m_prev + jnp.maximum(excess, 0.0)
        acc_ref[...] = acc * jnp.exp2(m_prev - m_new)
        m_ref[...] = m_new
        lag_ref[...] = jnp.maximum(lag_ref[...], excess)
        return carry

    reset()
    lag_ref[...] = jnp.full(lag_ref.shape, -jnp.inf, F32)
    exact_chunk(n_lat_tok)
    lax.fori_loop(0, n_lat_tok // tk, lagged_block, 0)

    @pl.when(jnp.max(lag_ref[...]) > LAG_LIMIT)
    def _():
        reset()
        lax.fori_loop(0, n_tok // MXU_TILE,
                      lambda c, carry: (exact_chunk(pl.multiple_of(c * MXU_TILE, MXU_TILE)), carry)[1], 0)

    o_ref[0] = _diff_combine_t(acc_ref[...], lam_ref, gs_ref, lam_init)


def _attn_ctx_kernel(lam_ref, gs_ref, q_ref, k_ref, v_ref, o_any, o_ref, *, lam_init):
    del o_any
    q2t = _split_maps_t(q_ref[0].astype(F32).T.astype(BF16))
    vbt = _vt_block(v_ref[0].astype(F32).T.astype(BF16), CTX_LEN)
    m0 = jnp.full((1, 2 * CTX_LEN), -jnp.inf, F32)
    acc0 = jnp.zeros((V_DIM + SUM_ROWS, 2 * CTX_LEN), F32)
    _, acc = _softmax_block_t(q2t, k_ref[0], vbt, m0, acc0)
    o_ref[0] = _diff_combine_t(acc, lam_ref, gs_ref, lam_init)


def _key_block(n_tok):
    for mult in range(MAX_KEY_TILES, 0, -1):
        if n_tok % (mult * MXU_TILE) == 0:
            return mult * MXU_TILE
    raise ValueError(n_tok)


def _attn_call(q, k, v, lam_qk, g_subln, lam_init):
    bsz, n_tok, _ = q.shape
    n_lat_tok = n_tok - CTX_LEN
    tk = _key_block(n_lat_tok)
    tq_step = TQ
    n_col = 2 * TQ
    small = [pl.BlockSpec((4, HEAD_DIM), lambda *_: (0, 0)),
             pl.BlockSpec((V_DIM, 1), lambda *_: (0, 0))]
    gs = g_subln.reshape(V_DIM, 1)
    o = pl.pallas_call(
        functools.partial(_attn_kernel, n_lat_tok=n_lat_tok, tk=tk, lam_init=lam_init),
        grid=(bsz, N_HEADS, n_lat_tok // tq_step),
        in_specs=small + [
            pl.BlockSpec((1, tq_step, LANES), lambda b, h, t: (b, t, h)),
            pl.BlockSpec((1, n_tok, LANES), lambda b, h, t: (b, 0, h)),
            pl.BlockSpec((1, n_tok, LANES), lambda b, h, t: (b, 0, h)),
        ],
        out_specs=pl.BlockSpec((1, tq_step, V_DIM), lambda b, h, t: (b, t, h)),
        out_shape=jax.ShapeDtypeStruct((bsz, n_tok, V_WIDTH), BF16),
        scratch_shapes=[
            pltpu.VMEM((V_DIM, n_tok), BF16),
            pltpu.VMEM((LANES, n_col), BF16),
            pltpu.VMEM((1, n_col), F32),
            pltpu.VMEM((V_DIM + SUM_ROWS, n_col), F32),
            pltpu.VMEM((1, n_col), F32),
            pltpu.VMEM((tk, n_col), BF16),
        ],
        compiler_params=_cparams(3),
        name="diff_attn",
    )(lam_qk, gs, q, k, v)
    ctx_blk = n_lat_tok // CTX_LEN
    ctx_spec = pl.BlockSpec((1, CTX_LEN, LANES), lambda b, h: (b, ctx_blk, h))
    return pl.pallas_call(
        functools.partial(_attn_ctx_kernel, lam_init=lam_init),
        grid=(bsz, N_HEADS),
        in_specs=small + [ctx_spec, ctx_spec, ctx_spec, pl.BlockSpec(memory_space=pl.ANY)],
        out_specs=ctx_spec,
        out_shape=jax.ShapeDtypeStruct((bsz, n_tok, V_WIDTH), BF16),
        input_output_aliases={5: 0},
        compiler_params=_cparams(2),
        name="diff_attn_ctx",
    )(lam_qk, gs, q, k, v, o)


def _dft1_kernel(u_ref, c1_ref, s1_ref, twc_ref, tws_ref, o_ref):
    n1 = u_ref.shape[1]
    rows = n1 * DFT_GROUP
    u = u_ref[0].reshape(rows, F_WIDTH).astype(BF16)
    ar = jnp.dot(c1_ref[...], u, preferred_element_type=F32)
    ai = -jnp.dot(s1_ref[...], u, preferred_element_type=F32)
    twc = twc_ref[0]
    tws = tws_ref[0]
    out = []
    for part in (lambda a_r, a_i: a_r * twc + a_i * tws, lambda a_r, a_i: a_i * twc - a_r * tws):
        for g in range(F_GROUPS):
            sl = slice(g * LANES, (g + 1) * LANES)
            out.append(part(ar[:, sl], ai[:, sl]))
    o_ref[0] = jnp.concatenate(out, axis=1).reshape(n1, DFT_GROUP, 2 * F_WIDTH)


def _channel_dft(gr, gi, cc_ref, sc_ref, norm):
    grb = gr.astype(BF16)
    gib = gi.astype(BF16)
    out = []
    for g in range(F_GROUPS):
        sl = slice(g * LANES, (g + 1) * LANES)
        out.append((jnp.dot(grb[:, sl], cc_ref[...], preferred_element_type=F32)
                    + jnp.dot(gib[:, sl], sc_ref[...], preferred_element_type=F32)) * norm)
    return out


def _dft2_kernel(b_ref, c2_ref, s2_ref, cc_ref, sc_ref, o_ref, *, norm):
    for j in range(DFT_GROUP):
        bb = b_ref[0, j].astype(BF16)
        cb = jnp.dot(c2_ref[...], bb, preferred_element_type=F32)
        sb = jnp.dot(s2_ref[...], bb, preferred_element_type=F32)
        gr = cb[:, :F_WIDTH] + sb[:, F_WIDTH:]
        gi = cb[:, F_WIDTH:] - sb[:, :F_WIDTH]
        for g, z in enumerate(_channel_dft(gr, gi, cc_ref, sc_ref, norm)):
            o_ref[0, :, j, g * LANES:(g + 1) * LANES] = z


def _dft_ctx_kernel(f_ref, c_ref, s_ref, cc_ref, sc_ref, z_any, o_ref, *, norm):
    del z_any
    f = f_ref[0].astype(BF16)
    gr = jnp.dot(c_ref[...], f, preferred_element_type=F32)
    gi = -jnp.dot(s_ref[...], f, preferred_element_type=F32)
    for g, z in enumerate(_channel_dft(gr, gi, cc_ref, sc_ref, norm)):
        o_ref[0, :, g * LANES:(g + 1) * LANES] = z


def _dft_mats(n, repeat=1):
    idx = np.arange(n, dtype=np.float64)
    ang = 2.0 * np.pi * np.outer(idx, idx) / n
    eye = np.eye(repeat)
    return (jnp.asarray(np.kron(np.cos(ang), eye), BF16), jnp.asarray(np.kron(np.sin(ang), eye), BF16))


def _fourier_call(f, n_lat_tok):
    bsz, n_tok, _ = f.shape
    n1 = n_lat_tok // DFT2
    full = lambda *shape: pl.BlockSpec(shape, lambda *_: (0,) * len(shape))
    c1, s1 = _dft_mats(n1, DFT_GROUP)
    c2, s2 = _dft_mats(DFT2)
    cc, sc = _dft_mats(F_GROUP_DIM)
    cctx, sctx = _dft_mats(CTX_LEN)
    tw_ang = (2.0 * np.pi / n_lat_tok) * np.einsum(
        "sj,k->skj", np.arange(DFT2, dtype=np.float64).reshape(DFT2 // DFT_GROUP, DFT_GROUP),
        np.arange(n1, dtype=np.float64)).reshape(DFT2 // DFT_GROUP, n1 * DFT_GROUP)
    twc = jnp.asarray(np.broadcast_to(np.cos(tw_ang)[:, :, None], tw_ang.shape + (LANES,)), F32)
    tws = jnp.asarray(np.broadcast_to(np.sin(tw_ang)[:, :, None], tw_ang.shape + (LANES,)), F32)

    fv = f.reshape(bsz, n_tok // DFT2, DFT2, F_WIDTH)
    mid = pl.pallas_call(
        _dft1_kernel,
        grid=(bsz, DFT2 // DFT_GROUP),
        in_specs=[
            pl.BlockSpec((1, n1, DFT_GROUP, F_WIDTH), lambda b, j: (b, 0, j, 0)),
            full(n1 * DFT_GROUP, n1 * DFT_GROUP), full(n1 * DFT_GROUP, n1 * DFT_GROUP),
            pl.BlockSpec((1, n1 * DFT_GROUP, LANES), lambda b, j: (j, 0, 0)),
            pl.BlockSpec((1, n1 * DFT_GROUP, LANES), lambda b, j: (j, 0, 0)),
        ],
        out_specs=pl.BlockSpec((1, n1, DFT_GROUP, 2 * F_WIDTH), lambda b, j: (b, 0, j, 0)),
        out_shape=jax.ShapeDtypeStruct((bsz, n1, DFT2, 2 * F_WIDTH), F32),
        compiler_params=_cparams(2),
        name="dft_outer",
    )(fv, c1, s1, twc, tws)

    z = pl.pallas_call(
        functools.partial(_dft2_kernel, norm=1.0 / math.sqrt(n_lat_tok * F_GROUP_DIM)),
        grid=(bsz, n1 // DFT_GROUP),
        in_specs=[
            pl.BlockSpec((1, DFT_GROUP, DFT2, 2 * F_WIDTH), lambda b, j: (b, j, 0, 0)),
            full(DFT2, DFT2), full(DFT2, DFT2),
            full(F_GROUP_DIM, F_GROUP_DIM), full(F_GROUP_DIM, F_GROUP_DIM),
        ],
        out_specs=pl.BlockSpec((1, DFT2, DFT_GROUP, F_WIDTH), lambda b, j: (b, 0, j, 0)),
        out_shape=jax.ShapeDtypeStruct((bsz, n_tok // n1, n1, F_WIDTH), F32),
        compiler_params=_cparams(2),
        name="dft_inner",
    )(mid, c2, s2, cc, sc)
    z = z.reshape(bsz, n_tok, F_WIDTH)

    ctx_blk = n_lat_tok // CTX_LEN
    z = pl.pallas_call(
        functools.partial(_dft_ctx_kernel, norm=1.0 / math.sqrt(CTX_LEN * F_GROUP_DIM)),
        grid=(bsz,),
        in_specs=[
            pl.BlockSpec((1, CTX_LEN, F_WIDTH), lambda b: (b, ctx_blk, 0)),
            full(CTX_LEN, CTX_LEN), full(CTX_LEN, CTX_LEN),
            full(F_GROUP_DIM, F_GROUP_DIM), full(F_GROUP_DIM, F_GROUP_DIM),
            pl.BlockSpec(memory_space=pl.ANY),
        ],
        out_specs=pl.BlockSpec((1, CTX_LEN, F_WIDTH), lambda b: (b, ctx_blk, 0)),
        out_shape=jax.ShapeDtypeStruct((bsz, n_tok, F_WIDTH), F32),
        input_output_aliases={5: 0},
        compiler_params=_cparams(1),
        name="dft_ctx",
    )(f, cctx, sctx, cc, sc, z)
    return z


def _merge_kernel(x_ref, o_ref, z_ref, mod_ref, g1_ref, g2_ref, wgate_ref, bgate_ref,
                  wao_ref, wfo_ref, wout_ref, wr_ref, br_ref, xo_ref, h2_ref, route_ref):
    x = x_ref[0]
    mod = mod_ref[0]
    hb = _rms_mod(x, g1_ref[...], mod[0:1], mod[1:2]).astype(BF16)
    gates = jax.nn.sigmoid(jnp.dot(hb, wgate_ref[...], preferred_element_type=F32) + bgate_ref[...])
    ya = jnp.dot(o_ref[0], wao_ref[...], preferred_element_type=F32)
    yf = jnp.dot(z_ref[0].astype(BF16), wfo_ref[...], preferred_element_type=F32)
    merged = gates[:, :D_MODEL] * ya + gates[:, D_MODEL:] * yf
    out = jnp.dot(merged.astype(BF16), wout_ref[...], preferred_element_type=F32)
    xn = x + mod[2:3] * out
    xo_ref[0] = xn
    h2 = _rms_mod(xn, g2_ref[...], mod[3:4], mod[4:5])
    h2_ref[0] = h2

    h2_hi = h2.astype(BF16)
    h2_lo = (h2 - h2_hi.astype(F32)).astype(BF16)
    part = jnp.dot(h2_hi, wr_ref[...], preferred_element_type=F32)
    logits = (part[:, :LANES] + part[:, LANES:]
              + jnp.dot(h2_lo, wr_ref[:, :LANES], preferred_element_type=F32))
    aff = jax.nn.sigmoid(logits)
    lane = lax.broadcasted_iota(jnp.int32, (TM, LANES), 1)
    lanef = lane.astype(F32)
    neg = jnp.full((TM, LANES), -jnp.inf, F32)
    sel = jnp.where(lane < N_EXPERTS, aff + br_ref[...], neg)
    grp = lane // EXPERTS_PER_GROUP
    best = i1b = i2b = None
    for g in range(N_GROUPS):
        sg = jnp.where(grp == g, sel, neg)
        m1 = jnp.max(sg, axis=1, keepdims=True)
        i1 = jnp.min(jnp.where(sg == m1, lanef, float(LANES)), axis=1, keepdims=True)
        sg2 = jnp.where(lanef == i1, neg, sg)
        m2 = jnp.max(sg2, axis=1, keepdims=True)
        i2 = jnp.min(jnp.where(sg2 == m2, lanef, float(LANES)), axis=1, keepdims=True)
        score = m1 + m2
        if g == 0:
            best, i1b, i2b = score, i1, i2
        else:
            upd = score > best
            best = jnp.where(upd, score, best)
            i1b = jnp.where(upd, i1, i1b)
            i2b = jnp.where(upd, i2, i2b)
    hit1 = lanef == i1b
    hit2 = lanef == i2b
    a1 = jnp.sum(jnp.where(hit1, aff, 0.0), axis=1, keepdims=True)
    a2 = jnp.sum(jnp.where(hit2, aff, 0.0), axis=1, keepdims=True)
    den = a1 + a2
    route_ref[0] = (jnp.where(lane == 0, i1b, 0.0) + jnp.where(lane == 1, i2b, 0.0)
                    + jnp.where(lane == 2, a1 / den, 0.0) + jnp.where(lane == 3, a2 / den, 0.0))


def _merge_call(x, o, z, mod, g1, g2, w_gate, b_gate, w_ao, w_fo, w_out, w_r, b_r, n_batch):
    bsz, n_tok, _ = x.shape
    nt = n_tok // TM
    tok = lambda w: pl.BlockSpec((1, TM, w), lambda b, t: (b, t, 0))
    full = lambda *shape: pl.BlockSpec(shape, lambda b, t: (0,) * len(shape))
    return pl.pallas_call(
        _merge_kernel,
        grid=(bsz, nt),
        in_specs=[
            tok(D_MODEL), tok(V_WIDTH), tok(F_WIDTH),
            pl.BlockSpec((1, N_MOD, D_MODEL),
                         lambda b, t: (jnp.where(t == nt - 1, n_batch, b), 0, 0)),
            full(1, D_MODEL), full(1, D_MODEL),
            full(D_MODEL, 2 * D_MODEL), full(1, 2 * D_MODEL),
            full(V_WIDTH, D_MODEL), full(F_WIDTH, D_MODEL), full(D_MODEL, D_MODEL),
            full(D_MODEL, 2 * LANES), full(1, LANES),
        ],
        out_specs=[tok(D_MODEL), tok(D_MODEL), tok(LANES)],
        out_shape=[
            jax.ShapeDtypeStruct((bsz, n_tok, D_MODEL), F32),
            jax.ShapeDtypeStruct((bsz, n_tok, D_MODEL), F32),
            jax.ShapeDtypeStruct((bsz, n_tok, LANES), F32),
        ],
        compiler_params=_cparams(2),
        name="merge_route",
    )(x, o, z, mod, g1, g2, w_gate, b_gate, w_ao, w_fo, w_out, w_r, b_r)


def _slot_copy(src_ref, src_row, dst_ref, dst_row, sem):
    return pltpu.make_async_copy(src_ref.at[pl.ds(src_row, 1)], dst_ref.at[pl.ds(dst_row, 1)], sem)


def _dispatch_kernel(pos_ref, h_ref, xs_in, xs_ref, sem):
    del xs_in

    def copy(r, k):
        return _slot_copy(h_ref, r, xs_ref, pos_ref[0, 0, 2 * r + k], sem)

    def issue(r, carry):
        copy(r, 0).start(priority=0)
        copy(r, 1).start(priority=1)
        return carry

    def drain(r, carry):
        copy(r, 0).wait()
        copy(r, 1).wait()
        return carry

    lax.fori_loop(0, TSLOT, issue, 0, unroll=DMA_UNROLL)
    lax.fori_loop(0, TSLOT, drain, 0, unroll=DMA_UNROLL)


def _experts_kernel(te_ref, nv_ref, xs_ref, wg_ref, wu_ref, wd_ref, ys_ref):
    del te_ref
    t = pl.program_id(0)

    @pl.when(t < nv_ref[0])
    def _():
        h = xs_ref[...].astype(BF16)
        a = jnp.dot(h, wg_ref[0], preferred_element_type=F32)
        u = jnp.dot(h, wu_ref[0], preferred_element_type=F32)
        he = (a * jax.nn.sigmoid(a) * u).astype(BF16)
        ys_ref[...] = jnp.dot(he, wd_ref[0], preferred_element_type=F32)

    @pl.when(t >= nv_ref[0])
    def _():
        ys_ref[...] = jnp.zeros(ys_ref.shape, F32)


def _combine_kernel(pos_ref, route_ref, ys_ref, y_ref, buf_ref, sem):
    def copy(r, k):
        return _slot_copy(ys_ref, pos_ref[0, 0, 2 * r + k], buf_ref.at[k], r, sem)

    def issue(r, carry):
        copy(r, 0).start(priority=0)
        copy(r, 1).start(priority=1)
        return carry

    def drain(r, carry):
        copy(r, 0).wait()
        copy(r, 1).wait()
        return carry

    lax.fori_loop(0, TSLOT, issue, 0, unroll=DMA_UNROLL)
    lax.fori_loop(0, TSLOT, drain, 0, unroll=DMA_UNROLL)
    route = route_ref[...]
    y_ref[...] = route[:, 2:3] * buf_ref[0] + route[:, 3:4] * buf_ref[1]


def _moe_call(h2, route, w_eg, w_eu, w_ed):
    n = h2.shape[0]
    n_slots = 2 * n
    n_rows = n_slots + N_EXPERTS * TMR
    n_tiles = n_rows // TMR

    e_slot = route[:, :2].astype(jnp.int32).reshape(n_slots)
    onehot = (e_slot[:, None] == jnp.arange(N_EXPERTS, dtype=jnp.int32)[None, :]).astype(jnp.int32)
    csum = jnp.cumsum(onehot, axis=0)
    rank = jnp.sum(csum * onehot, axis=1) - 1
    counts = csum[-1]
    padded = ((counts + TMR - 1) // TMR) * TMR
    ends = jnp.cumsum(padded)
    pos = jnp.sum(onehot * (ends - padded)[None, :], axis=1) + rank
    tile_expert = jnp.minimum(
        jnp.sum((jnp.arange(n_tiles, dtype=jnp.int32) * TMR)[:, None] >= ends[None, :], axis=1),
        N_EXPERTS - 1).astype(jnp.int32)
    n_valid = (ends[-1] // TMR).astype(jnp.int32).reshape(1)
    pos = jnp.clip(pos, 0, n_rows - 1)
    pos_tiles = pos.astype(jnp.int32).reshape(n // TSLOT, 1, 2 * TSLOT)

    pos_spec = pl.BlockSpec((1, 1, 2 * TSLOT), lambda t: (t, 0, 0), memory_space=pltpu.SMEM)
    xs = pl.pallas_call(
        _dispatch_kernel,
        grid=(n // TSLOT,),
        in_specs=[pos_spec,
                  pl.BlockSpec((TSLOT, D_MODEL), lambda t: (t, 0)),
                  pl.BlockSpec(memory_space=pl.ANY)],
        out_specs=pl.BlockSpec(memory_space=pl.ANY),
        out_shape=jax.ShapeDtypeStruct((n_rows, D_MODEL), F32),
        scratch_shapes=[pltpu.SemaphoreType.DMA],
        input_output_aliases={2: 0},
        compiler_params=_cparams(1),
        name="dispatch",
    )(pos_tiles, h2, jnp.zeros((n_rows, D_MODEL), F32))

    w_map = lambda t, te, nv: (te[t], 0, 0)
    ys = pl.pallas_call(
        _experts_kernel,
        grid_spec=pltpu.PrefetchScalarGridSpec(
            num_scalar_prefetch=2,
            grid=(n_tiles,),
            in_specs=[
                pl.BlockSpec((TMR, D_MODEL), lambda t, te, nv: (t, 0)),
                pl.BlockSpec((1, D_MODEL, D_EXPERT), w_map),
                pl.BlockSpec((1, D_MODEL, D_EXPERT), w_map),
                pl.BlockSpec((1, D_EXPERT, D_MODEL), w_map),
            ],
            out_specs=pl.BlockSpec((TMR, D_MODEL), lambda t, te, nv: (t, 0)),
        ),
        out_shape=jax.ShapeDtypeStruct((n_rows, D_MODEL), F32),
        compiler_params=_cparams(1),
        name="experts",
    )(tile_expert, n_valid, xs, w_eg, w_eu, w_ed)

    return pl.pallas_call(
        _combine_kernel,
        grid=(n // TSLOT,),
        in_specs=[pos_spec,
                  pl.BlockSpec((TSLOT, LANES), lambda t: (t, 0)),
                  pl.BlockSpec(memory_space=pl.ANY)],
        out_specs=pl.BlockSpec((TSLOT, D_MODEL), lambda t: (t, 0)),
        out_shape=jax.ShapeDtypeStruct((n, D_MODEL), F32),
        scratch_shapes=[pltpu.VMEM((2, TSLOT, D_MODEL), F32), pltpu.SemaphoreType.DMA],
        compiler_params=_cparams(1),
        name="combine",
    )(pos_tiles, route, ys)


def _final_kernel(x_ref, y_ref, mod_ref, g_ref, o_ref):
    x = x_ref[0] + mod_ref[0][5:6] * y_ref[0]
    o_ref[0] = x * lax.rsqrt(jnp.mean(x * x, axis=-1, keepdims=True) + EPS) * g_ref[...]


def _final_call(x, y, mod, g_final, n_lat_tok):
    bsz = x.shape[0]
    tok = pl.BlockSpec((1, TM, D_MODEL), lambda b, t: (b, t, 0))
    return pl.pallas_call(
        _final_kernel,
        grid=(bsz, n_lat_tok // TM),
        in_specs=[tok, tok,
                  pl.BlockSpec((1, N_MOD, D_MODEL), lambda b, t: (b, 0, 0)),
                  pl.BlockSpec((1, D_MODEL), lambda b, t: (0, 0))],
        out_specs=tok,
        out_shape=jax.ShapeDtypeStruct((bsz, n_lat_tok, D_MODEL), F32),
        compiler_params=_cparams(2),
        name="final_norm",
    )(x, y, mod, g_final)


def _rope_tables(n_lat_tok):
    t = np.arange(n_lat_tok)
    pos = np.stack([t // GRID_W, t % GRID_W], axis=0).astype(np.float64)
    inv = ROPE_THETA ** (-np.arange(ROPE_HALF, dtype=np.float64) / ROPE_HALF)
    d = np.arange(LANES) % HEAD_DIM
    axis = d // (2 * ROPE_HALF)
    freq = d % ROPE_HALF
    ang = pos[axis, :].T * inv[freq][None, :]
    sign = np.where((d % (2 * ROPE_HALF)) < ROPE_HALF, -1.0, 1.0)
    cos = np.concatenate([np.cos(ang), np.ones((CTX_LEN, LANES))], axis=0)
    sin = np.concatenate([np.sin(ang) * sign[None, :], np.zeros((CTX_LEN, LANES))], axis=0)
    return jnp.asarray(cos, F32), jnp.asarray(sin, F32)


def kernel(x, c, ctx, c_ctx, w_ada, b_ada, g_norm1, w_in, lam_qk, g_subln, w_attn_out, w_four_out,
           w_gate, b_gate, w_out, g_norm2, w_router, b_router, w_e_gate, w_e_up, w_e_down, g_final):
    bsz, n_lat_tok, d = x.shape
    depth = w_ada.shape[0]
    assert d == D_MODEL and ctx.shape[1] == CTX_LEN and bsz < MOD_ROWS
    assert n_lat_tok % TQ == 0 and n_lat_tok % (DFT2 * DFT_GROUP) == 0
    n_tok = n_lat_tok + CTX_LEN
    assert (bsz * n_tok) % TSLOT == 0

    cond = jnp.zeros((MOD_ROWS, D_MODEL), F32).at[:bsz].set(c).at[bsz].set(c_ctx)
    mods = _ada_call(cond, w_ada, b_ada).reshape(depth, MOD_ROWS, N_MOD, D_MODEL)
    cos_t, sin_t = _rope_tables(n_lat_tok)
    w_r = jnp.zeros((D_MODEL, LANES), F32).at[:, :N_EXPERTS].set(w_router)
    w_r_hi = w_r.astype(BF16)
    w_r = jnp.concatenate([w_r_hi, (w_r - w_r_hi.astype(F32)).astype(BF16)], axis=1)
    b_r = jnp.zeros((1, LANES), F32).at[0, :N_EXPERTS].set(b_router)

    xs = jnp.concatenate([x, ctx], axis=1)
    y = None
    for i in range(depth):
        lam_init = 0.8 - 0.6 * math.exp(-0.3 * i)
        xs, q, k, v, f = _proj_call(
            xs, y, mods[i - 1] if i else None, mods[i], g_norm1[i].reshape(1, D_MODEL),
            w_in[i].astype(BF16), cos_t, sin_t, bsz)
        o = _attn_call(q, k, v, lam_qk[i], g_subln[i], lam_init)
        z = _fourier_call(f, n_lat_tok)
        xs, h2, route = _merge_call(
            xs, o, z, mods[i], g_norm1[i].reshape(1, D_MODEL), g_norm2[i].reshape(1, D_MODEL),
            w_gate[i].astype(BF16), b_gate[i].reshape(1, 2 * D_MODEL), w_attn_out[i].astype(BF16),
            w_four_out[i].astype(BF16), w_out[i].astype(BF16), w_r, b_r, bsz)
        y = _moe_call(h2.reshape(bsz * n_tok, D_MODEL), route.reshape(bsz * n_tok, LANES),
                      w_e_gate[i].astype(BF16), w_e_up[i].astype(BF16), w_e_down[i].astype(BF16))
        y = y.reshape(bsz, n_tok, D_MODEL)
    return _final_call(xs, y, mods[depth - 1], g_final.reshape(1, D_MODEL), n_lat_tok)
```

```python
import functools
import math

import numpy as np
import jax
import jax.numpy as jnp
from jax import lax
from jax.experimental import pallas as pl
from jax.experimental.pallas import tpu as pltpu

F32 = jnp.float32
BF16 = jnp.bfloat16
HIGHEST = lax.Precision.HIGHEST

D_MODEL = 1024
GRID_W = 64
CTX_LEN = 256
N_HEADS = 8
HEAD_DIM = 64
V_DIM = 128
QK_WIDTH = 1024
V_WIDTH = 1024
F_GROUPS = 4
F_WIDTH = 512
F_GROUP_DIM = 128
IN_WIDTH = 3584
ROPE_THETA = 10000.0
ROPE_HALF = 16
N_EXPERTS = 16
N_GROUPS = 4
EXPERTS_PER_GROUP = 4
D_EXPERT = 512
EPS = 1e-6
N_MOD = 6
MOD_ROWS = 8

LANES = 128
TM = 256
TQ = 1024
MXU_TILE = 256
BF16_ROWS = 16
SUM_ROWS = 16
MAX_KEY_TILES = 16
QK_ROWS = 512
LAG_LIMIT = 100.0
SUM_FLOOR = 2.0 ** -60
TMR = 256
TSLOT = 512
DMA_UNROLL = 8
ADA_TN = 512
DFT2 = 128
DFT_GROUP = 8
Q_SCALE = (HEAD_DIM ** -0.5) * math.log2(math.e)
VMEM_LIMIT = 56 * 1024 * 1024


def _cparams(n_axes):
    return pltpu.CompilerParams(dimension_semantics=("arbitrary",) * n_axes,
                                vmem_limit_bytes=VMEM_LIMIT)


def _rms_mod(x, g, shift, scale):
    y = x * lax.rsqrt(jnp.mean(x * x, axis=-1, keepdims=True) + EPS) * g
    return y * (1.0 + scale) + shift


def _ada_kernel(c_ref, w_ref, b_ref, o_ref):
    c = c_ref[...]
    s = c * jax.nn.sigmoid(c)
    o_ref[0] = jnp.dot(s, w_ref[0], preferred_element_type=F32, precision=HIGHEST) + b_ref[0]


def _ada_call(cond, w_ada, b_ada):
    depth = w_ada.shape[0]
    n_out = w_ada.shape[2]
    return pl.pallas_call(
        _ada_kernel,
        grid=(depth, n_out // ADA_TN),
        in_specs=[
            pl.BlockSpec((MOD_ROWS, D_MODEL), lambda l, j: (0, 0)),
            pl.BlockSpec((1, D_MODEL, ADA_TN), lambda l, j: (l, 0, j)),
            pl.BlockSpec((1, 1, ADA_TN), lambda l, j: (l, 0, j)),
        ],
        out_specs=pl.BlockSpec((1, MOD_ROWS, ADA_TN), lambda l, j: (l, 0, j)),
        out_shape=jax.ShapeDtypeStruct((depth, MOD_ROWS, n_out), F32),
        compiler_params=_cparams(2),
        name="ada",
    )(cond, w_ada, b_ada.reshape(depth, 1, n_out))


def _proj_kernel(*refs, has_y):
    if has_y:
        (x_ref, y_ref, pmod_ref, mod_ref, g1_ref, win_ref, cos_ref, sin_ref,
         xo_ref, q_ref, k_ref, v_ref, f_ref) = refs
        x = x_ref[0] + pmod_ref[0][5:6] * y_ref[0]
        xo_ref[0] = x
    else:
        (x_ref, mod_ref, g1_ref, win_ref, cos_ref, sin_ref,
         q_ref, k_ref, v_ref, f_ref) = refs
        x = x_ref[0]
    mod = mod_ref[0]
    hb = _rms_mod(x, g1_ref[...], mod[0:1], mod[1:2]).astype(BF16)
    cos = cos_ref[...]
    sin = sin_ref[...]
    lane = lax.broadcasted_iota(jnp.int32, (TM, LANES), 1)
    first = (lane % (2 * ROPE_HALF)) < ROPE_HALF

    def rope(xc):
        up = pltpu.roll(xc, LANES - ROPE_HALF, 1)
        dn = pltpu.roll(xc, ROPE_HALF, 1)
        return xc * cos + jnp.where(first, up, dn) * sin

    w2 = 2 * LANES
    for c in range(QK_WIDTH // w2):
        uq = jnp.dot(hb, win_ref[:, c * w2:(c + 1) * w2], preferred_element_type=F32)
        uk = jnp.dot(hb, win_ref[:, QK_WIDTH + c * w2:QK_WIDTH + (c + 1) * w2],
                     preferred_element_type=F32)
        for j in range(2):
            lo = c * w2 + j * LANES
            q_ref[0, :, lo:lo + LANES] = (rope(uq[:, j * LANES:(j + 1) * LANES]) * Q_SCALE).astype(BF16)
            k_ref[0, :, lo:lo + LANES] = rope(uk[:, j * LANES:(j + 1) * LANES]).astype(BF16)
    for c in range(V_WIDTH // w2):
        lo = 2 * QK_WIDTH + c * w2
        v_ref[0, :, c * w2:(c + 1) * w2] = jnp.dot(
            hb, win_ref[:, lo:lo + w2], preferred_element_type=F32).astype(BF16)
    for c in range(F_WIDTH // w2):
        lo = 2 * QK_WIDTH + V_WIDTH + c * w2
        f_ref[0, :, c * w2:(c + 1) * w2] = jnp.dot(
            hb, win_ref[:, lo:lo + w2], preferred_element_type=F32)


def _proj_call(x, y, pmod, mod, g1, w_in, cos_t, sin_t, n_batch):
    bsz, n_tok, _ = x.shape
    nt = n_tok // TM
    has_y = y is not None

    def mod_map(b, t):
        return (jnp.where(t == nt - 1, n_batch, b), 0, 0)

    tok = lambda w: pl.BlockSpec((1, TM, w), lambda b, t: (b, t, 0))
    modspec = pl.BlockSpec((1, N_MOD, D_MODEL), mod_map)
    in_specs = [tok(D_MODEL)]
    args = [x]
    if has_y:
        in_specs += [tok(D_MODEL), modspec]
        args += [y, pmod]
    in_specs += [
        modspec,
        pl.BlockSpec((1, D_MODEL), lambda b, t: (0, 0)),
        pl.BlockSpec((D_MODEL, IN_WIDTH), lambda b, t: (0, 0)),
        pl.BlockSpec((TM, LANES), lambda b, t: (t, 0)),
        pl.BlockSpec((TM, LANES), lambda b, t: (t, 0)),
    ]
    args += [mod, g1, w_in, cos_t, sin_t]
    out_specs = [tok(QK_WIDTH), tok(QK_WIDTH), tok(V_WIDTH), tok(F_WIDTH)]
    out_shape = [
        jax.ShapeDtypeStruct((bsz, n_tok, QK_WIDTH), BF16),
        jax.ShapeDtypeStruct((bsz, n_tok, QK_WIDTH), BF16),
        jax.ShapeDtypeStruct((bsz, n_tok, V_WIDTH), BF16),
        jax.ShapeDtypeStruct((bsz, n_tok, F_WIDTH), F32),
    ]
    if has_y:
        out_specs = [tok(D_MODEL)] + out_specs
        out_shape = [jax.ShapeDtypeStruct((bsz, n_tok, D_MODEL), F32)] + out_shape
    outs = pl.pallas_call(
        functools.partial(_proj_kernel, has_y=has_y),
        grid=(bsz, nt),
        in_specs=in_specs,
        out_specs=out_specs,
        out_shape=out_shape,
        compiler_params=_cparams(2),
        name="proj",
    )(*args)
    if has_y:
        return outs
    return [x] + list(outs)


def _split_maps_t(qt):
    row = lax.broadcasted_iota(jnp.int32, qt.shape, 0)
    zero = jnp.zeros_like(qt)
    return jnp.concatenate([jnp.where(row < HEAD_DIM, qt, zero),
                            jnp.where(row >= HEAD_DIM, qt, zero)], axis=1)


def _ones_rows(n_cols):
    row = lax.broadcasted_iota(jnp.int32, (SUM_ROWS, n_cols), 0)
    return jnp.where(row == 0, 1.0, 0.0).astype(BF16)


def _vt_block(vt, n_keys):
    return jnp.concatenate([vt, _ones_rows(n_keys)], axis=0)


def _softmax_block_t(q2t, kb, vbt_ext, m_prev, acc_prev):
    st = jnp.dot(kb, q2t, preferred_element_type=F32)
    m_new = jnp.maximum(m_prev, jnp.max(st, axis=0, keepdims=True))
    alpha = jnp.exp2(m_prev - m_new)
    pt = jnp.exp2(st - m_new).astype(BF16)
    acc = alpha * acc_prev + jnp.dot(vbt_ext, pt, preferred_element_type=F32)
    return m_new, acc


def _diff_combine_t(acc, lam_ref, gs_ref, lam_init):
    tq = acc.shape[1] // 2
    lp = lam_ref[...]
    lam = (jnp.exp(jnp.sum(lp[0:1] * lp[1:2], axis=1, keepdims=True))
           - jnp.exp(jnp.sum(lp[2:3] * lp[3:4], axis=1, keepdims=True)) + lam_init)
    pv = acc[:V_DIM]
    l = acc[V_DIM:V_DIM + 1]
    ot = pv[:, :tq] / l[:, :tq] - lam * (pv[:, tq:] / l[:, tq:])
    ot = ot * lax.rsqrt(jnp.mean(ot * ot, axis=0, keepdims=True) + EPS) * gs_ref[...]
    return (ot * (1.0 - lam_init)).T.astype(BF16)


def _attn_kernel(lam_ref, gs_ref, q_ref, k_ref, v_ref, o_ref, vt_ref, q2t_ref, m_ref, acc_ref,
                 lag_ref, pt_ref, *, n_lat_tok, tk, lam_init):
    n_tok = n_lat_tok + CTX_LEN

    @pl.when(pl.program_id(2) == 0)
    def _():
        for j in range(n_tok // MXU_TILE):
            sl = slice(j * MXU_TILE, (j + 1) * MXU_TILE)
            vt_ref[:, sl] = v_ref[0, sl, :].astype(F32).T.astype(BF16)

    q2t_ref[...] = _split_maps_t(q_ref[0].astype(F32).T.astype(BF16))

    def reset(m0):
        m_ref[...] = jnp.full(m_ref.shape, m0, F32)
        acc_ref[...] = jnp.zeros(acc_ref.shape, F32)

    def exact_chunk(off):
        m_new, acc = _softmax_block_t(
            q2t_ref[...], k_ref[0, pl.ds(off, MXU_TILE), :],
            _vt_block(vt_ref[:, pl.ds(off, MXU_TILE)], MXU_TILE), m_ref[...], acc_ref[...])
        m_ref[...] = m_new
        acc_ref[...] = acc

    def lagged_block(off, n_keys):
        m_prev = m_ref[...]
        pmax = None
        for r in range(0, n_keys, QK_ROWS):
            rows = min(QK_ROWS, n_keys - r)
            st = jnp.dot(k_ref[0, pl.ds(off + r, rows), :], q2t_ref[...], preferred_element_type=F32)
            p = jnp.exp2(st - m_prev).astype(BF16)
            pt_ref[r:r + rows, :] = p
            for t in range(0, rows, BF16_ROWS):
                tile = p[t:t + BF16_ROWS, :]
                pmax = tile if pmax is None else jnp.maximum(pmax, tile)
        acc = acc_ref[...] + jnp.dot(_vt_block(vt_ref[:, pl.ds(off, n_keys)], n_keys), pt_ref[0:n_keys, :],
                                     preferred_element_type=F32)
        excess = jnp.log2(jnp.max(pmax.astype(F32), axis=0, keepdims=True))
        m_new = m_prev + jnp.maximum(excess, 0.0)
        acc_ref[...] = acc * jnp.exp2(m_prev - m_new)
        m_ref[...] = m_new
        lag_ref[...] = jnp.maximum(lag_ref[...], excess)

    reset(0.0)
    lag_ref[...] = jnp.full(lag_ref.shape, -jnp.inf, F32)
    n_full = n_tok // tk - 1
    lax.fori_loop(0, n_full, lambda j, carry: (lagged_block(pl.multiple_of(j * tk, tk), tk), carry)[1], 0)
    lagged_block(n_full * tk, n_tok - n_full * tk)

    sum_min = jnp.min(acc_ref[V_DIM:V_DIM + 1, :])

    @pl.when(jnp.logical_or(jnp.max(lag_ref[...]) > LAG_LIMIT, sum_min < SUM_FLOOR))
    def _():
        reset(-jnp.inf)
        lax.fori_loop(0, n_tok // MXU_TILE,
                      lambda c, carry: (exact_chunk(pl.multiple_of(c * MXU_TILE, MXU_TILE)), carry)[1], 0)

    o_ref[0] = _diff_combine_t(acc_ref[...], lam_ref, gs_ref, lam_init)


def _attn_ctx_kernel(lam_ref, gs_ref, q_ref, k_ref, v_ref, o_any, o_ref, *, lam_init):
    del o_any
    q2t = _split_maps_t(q_ref[0].astype(F32).T.astype(BF16))
    vbt = _vt_block(v_ref[0].astype(F32).T.astype(BF16), CTX_LEN)
    m0 = jnp.full((1, 2 * CTX_LEN), -jnp.inf, F32)
    acc0 = jnp.zeros((V_DIM + SUM_ROWS, 2 * CTX_LEN), F32)
    _, acc = _softmax_block_t(q2t, k_ref[0], vbt, m0, acc0)
    o_ref[0] = _diff_combine_t(acc, lam_ref, gs_ref, lam_init)


def _key_block(n_tok):
    for mult in range(MAX_KEY_TILES, 0, -1):
        if n_tok % (mult * MXU_TILE) == 0:
            return mult * MXU_TILE
    raise ValueError(n_tok)


def _attn_call(q, k, v, lam_qk, g_subln, lam_init):
    bsz, n_tok, _ = q.shape
    n_lat_tok = n_tok - CTX_LEN
    tk = _key_block(n_lat_tok)
    tq_step = TQ
    n_col = 2 * TQ
    small = [pl.BlockSpec((4, HEAD_DIM), lambda *_: (0, 0)),
             pl.BlockSpec((V_DIM, 1), lambda *_: (0, 0))]
    gs = g_subln.reshape(V_DIM, 1)
    o = pl.pallas_call(
        functools.partial(_attn_kernel, n_lat_tok=n_lat_tok, tk=tk, lam_init=lam_init),
        grid=(bsz, N_HEADS, n_lat_tok // tq_step),
        in_specs=small + [
            pl.BlockSpec((1, tq_step, LANES), lambda b, h, t: (b, t, h)),
            pl.BlockSpec((1, n_tok, LANES), lambda b, h, t: (b, 0, h)),
            pl.BlockSpec((1, n_tok, LANES), lambda b, h, t: (b, 0, h)),
        ],
        out_specs=pl.BlockSpec((1, tq_step, V_DIM), lambda b, h, t: (b, t, h)),
        out_shape=jax.ShapeDtypeStruct((bsz, n_tok, V_WIDTH), BF16),
        scratch_shapes=[
            pltpu.VMEM((V_DIM, n_tok), BF16),
            pltpu.VMEM((LANES, n_col), BF16),
            pltpu.VMEM((1, n_col), F32),
            pltpu.VMEM((V_DIM + SUM_ROWS, n_col), F32),
            pltpu.VMEM((1, n_col), F32),
            pltpu.VMEM((n_tok - (n_tok // tk - 1) * tk, n_col), BF16),
        ],
        compiler_params=_cparams(3),
        name="diff_attn",
    )(lam_qk, gs, q, k, v)
    ctx_blk = n_lat_tok // CTX_LEN
    ctx_spec = pl.BlockSpec((1, CTX_LEN, LANES), lambda b, h: (b, ctx_blk, h))
    return pl.pallas_call(
        functools.partial(_attn_ctx_kernel, lam_init=lam_init),
        grid=(bsz, N_HEADS),
        in_specs=small + [ctx_spec, ctx_spec, ctx_spec, pl.BlockSpec(memory_space=pl.ANY)],
        out_specs=ctx_spec,
        out_shape=jax.ShapeDtypeStruct((bsz, n_tok, V_WIDTH), BF16),
        input_output_aliases={5: 0},
        compiler_params=_cparams(2),
        name="diff_attn_ctx",
    )(lam_qk, gs, q, k, v, o)


def _dft1_kernel(u_ref, c1_ref, s1_ref, twc_ref, tws_ref, o_ref):
    n1 = u_ref.shape[1]
    rows = n1 * DFT_GROUP
    u = u_ref[0].reshape(rows, F_WIDTH).astype(BF16)
    ar = jnp.dot(c1_ref[...], u, preferred_element_type=F32)
    ai = -jnp.dot(s1_ref[...], u, preferred_element_type=F32)
    twc = twc_ref[0]
    tws = tws_ref[0]
    out = []
    for part in (lambda a_r, a_i: a_r * twc + a_i * tws, lambda a_r, a_i: a_i * twc - a_r * tws):
        for g in range(F_GROUPS):
            sl = slice(g * LANES, (g + 1) * LANES)
            out.append(part(ar[:, sl], ai[:, sl]))
    o_ref[0] = jnp.concatenate(out, axis=1).reshape(n1, DFT_GROUP, 2 * F_WIDTH)


def _channel_dft(gr, gi, cc_ref, sc_ref, norm):
    grb = gr.astype(BF16)
    gib = gi.astype(BF16)
    out = []
    for g in range(F_GROUPS):
        sl = slice(g * LANES, (g + 1) * LANES)
        out.append((jnp.dot(grb[:, sl], cc_ref[...], preferred_element_type=F32)
                    + jnp.dot(gib[:, sl], sc_ref[...], preferred_element_type=F32)) * norm)
    return out


def _dft2_kernel(b_ref, c2_ref, s2_ref, cc_ref, sc_ref, o_ref, *, norm):
    for j in range(DFT_GROUP):
        bb = b_ref[0, j].astype(BF16)
        cb = jnp.dot(c2_ref[...], bb, preferred_element_type=F32)
        sb = jnp.dot(s2_ref[...], bb, preferred_element_type=F32)
        gr = cb[:, :F_WIDTH] + sb[:, F_WIDTH:]
        gi = cb[:, F_WIDTH:] - sb[:, :F_WIDTH]
        for g, z in enumerate(_channel_dft(gr, gi, cc_ref, sc_ref, norm)):
            o_ref[0, :, j, g * LANES:(g + 1) * LANES] = z


def _dft_ctx_kernel(f_ref, c_ref, s_ref, cc_ref, sc_ref, z_any, o_ref, *, norm):
    del z_any
    f = f_ref[0].astype(BF16)
    gr = jnp.dot(c_ref[...], f, preferred_element_type=F32)
    gi = -jnp.dot(s_ref[...], f, preferred_element_type=F32)
    for g, z in enumerate(_channel_dft(gr, gi, cc_ref, sc_ref, norm)):
        o_ref[0, :, g * LANES:(g + 1) * LANES] = z


def _dft_mats(n, repeat=1):
    idx = np.arange(n, dtype=np.float64)
    ang = 2.0 * np.pi * np.outer(idx, idx) / n
    eye = np.eye(repeat)
    return (jnp.asarray(np.kron(np.cos(ang), eye), BF16), jnp.asarray(np.kron(np.sin(ang), eye), BF16))


def _fourier_call(f, n_lat_tok):
    bsz, n_tok, _ = f.shape
    n1 = n_lat_tok // DFT2
    full = lambda *shape: pl.BlockSpec(shape, lambda *_: (0,) * len(shape))
    c1, s1 = _dft_mats(n1, DFT_GROUP)
    c2, s2 = _dft_mats(DFT2)
    cc, sc = _dft_mats(F_GROUP_DIM)
    cctx, sctx = _dft_mats(CTX_LEN)
    tw_ang = (2.0 * np.pi / n_lat_tok) * np.einsum(
        "sj,k->skj", np.arange(DFT2, dtype=np.float64).reshape(DFT2 // DFT_GROUP, DFT_GROUP),
        np.arange(n1, dtype=np.float64)).reshape(DFT2 // DFT_GROUP, n1 * DFT_GROUP)
    twc = jnp.asarray(np.broadcast_to(np.cos(tw_ang)[:, :, None], tw_ang.shape + (LANES,)), F32)
    tws = jnp.asarray(np.broadcast_to(np.sin(tw_ang)[:, :, None], tw_ang.shape + (LANES,)), F32)

    fv = f.reshape(bsz, n_tok // DFT2, DFT2, F_WIDTH)
    mid = pl.pallas_call(
        _dft1_kernel,
        grid=(bsz, DFT2 // DFT_GROUP),
        in_specs=[
            pl.BlockSpec((1, n1, DFT_GROUP, F_WIDTH), lambda b, j: (b, 0, j, 0)),
            full(n1 * DFT_GROUP, n1 * DFT_GROUP), full(n1 * DFT_GROUP, n1 * DFT_GROUP),
            pl.BlockSpec((1, n1 * DFT_GROUP, LANES), lambda b, j: (j, 0, 0)),
            pl.BlockSpec((1, n1 * DFT_GROUP, LANES), lambda b, j: (j, 0, 0)),
        ],
        out_specs=pl.BlockSpec((1, n1, DFT_GROUP, 2 * F_WIDTH), lambda b, j: (b, 0, j, 0)),
        out_shape=jax.ShapeDtypeStruct((bsz, n1, DFT2, 2 * F_WIDTH), F32),
        compiler_params=_cparams(2),
        name="dft_outer",
    )(fv, c1, s1, twc, tws)

    z = pl.pallas_call(
        functools.partial(_dft2_kernel, norm=1.0 / math.sqrt(n_lat_tok * F_GROUP_DIM)),
        grid=(bsz, n1 // DFT_GROUP),
        in_specs=[
            pl.BlockSpec((1, DFT_GROUP, DFT2, 2 * F_WIDTH), lambda b, j: (b, j, 0, 0)),
            full(DFT2, DFT2), full(DFT2, DFT2),
            full(F_GROUP_DIM, F_GROUP_DIM), full(F_GROUP_DIM, F_GROUP_DIM),
        ],
        out_specs=pl.BlockSpec((1, DFT2, DFT_GROUP, F_WIDTH), lambda b, j: (b, 0, j, 0)),
        out_shape=jax.ShapeDtypeStruct((bsz, n_tok // n1, n1, F_WIDTH), F32),
        compiler_params=_cparams(2),
        name="dft_inner",
    )(mid, c2, s2, cc, sc)
    z = z.reshape(bsz, n_tok, F_WIDTH)

    ctx_blk = n_lat_tok // CTX_LEN
    z = pl.pallas_call(
        functools.partial(_dft_ctx_kernel, norm=1.0 / math.sqrt(CTX_LEN * F_GROUP_DIM)),
        grid=(bsz,),
        in_specs=[
            pl.BlockSpec((1, CTX_LEN, F_WIDTH), lambda b: (b, ctx_blk, 0)),
            full(CTX_LEN, CTX_LEN), full(CTX_LEN, CTX_LEN),
            full(F_GROUP_DIM, F_GROUP_DIM), full(F_GROUP_DIM, F_GROUP_DIM),
            pl.BlockSpec(memory_space=pl.ANY),
        ],
        out_specs=pl.BlockSpec((1, CTX_LEN, F_WIDTH), lambda b: (b, ctx_blk, 0)),
        out_shape=jax.ShapeDtypeStruct((bsz, n_tok, F_WIDTH), F32),
        input_output_aliases={5: 0},
        compiler_params=_cparams(1),
        name="dft_ctx",
    )(f, cctx, sctx, cc, sc, z)
    return z


def _merge_kernel(x_ref, o_ref, z_ref, mod_ref, g1_ref, g2_ref, wgate_ref, bgate_ref,
                  wao_ref, wfo_ref, wout_ref, wr_ref, br_ref, xo_ref, h2_ref, route_ref):
    x = x_ref[0]
    mod = mod_ref[0]
    hb = _rms_mod(x, g1_ref[...], mod[0:1], mod[1:2]).astype(BF16)
    gates = jax.nn.sigmoid(jnp.dot(hb, wgate_ref[...], preferred_element_type=F32) + bgate_ref[...])
    ya = jnp.dot(o_ref[0], wao_ref[...], preferred_element_type=F32)
    yf = jnp.dot(z_ref[0].astype(BF16), wfo_ref[...], preferred_element_type=F32)
    merged = gates[:, :D_MODEL] * ya + gates[:, D_MODEL:] * yf
    out = jnp.dot(merged.astype(BF16), wout_ref[...], preferred_element_type=F32)
    xn = x + mod[2:3] * out
    xo_ref[0] = xn
    h2 = _rms_mod(xn, g2_ref[...], mod[3:4], mod[4:5])
    h2_ref[0] = h2

    h2_hi = h2.astype(BF16)
    h2_lo = (h2 - h2_hi.astype(F32)).astype(BF16)
    part = jnp.dot(h2_hi, wr_ref[...], preferred_element_type=F32)
    logits = (part[:, :LANES] + part[:, LANES:]
              + jnp.dot(h2_lo, wr_ref[:, :LANES], preferred_element_type=F32))
    aff = jax.nn.sigmoid(logits)
    lane = lax.broadcasted_iota(jnp.int32, (TM, LANES), 1)
    lanef = lane.astype(F32)
    neg = jnp.full((TM, LANES), -jnp.inf, F32)
    sel = jnp.where(lane < N_EXPERTS, aff + br_ref[...], neg)
    grp = lane // EXPERTS_PER_GROUP
    best = i1b = i2b = None
    for g in range(N_GROUPS):
        sg = jnp.where(grp == g, sel, neg)
        m1 = jnp.max(sg, axis=1, keepdims=True)
        i1 = jnp.min(jnp.where(sg == m1, lanef, float(LANES)), axis=1, keepdims=True)
        sg2 = jnp.where(lanef == i1, neg, sg)
        m2 = jnp.max(sg2, axis=1, keepdims=True)
        i2 = jnp.min(jnp.where(sg2 == m2, lanef, float(LANES)), axis=1, keepdims=True)
        score = m1 + m2
        if g == 0:
            best, i1b, i2b = score, i1, i2
        else:
            upd = score > best
            best = jnp.where(upd, score, best)
            i1b = jnp.where(upd, i1, i1b)
            i2b = jnp.where(upd, i2, i2b)
    hit1 = lanef == i1b
    hit2 = lanef == i2b
    a1 = jnp.sum(jnp.where(hit1, aff, 0.0), axis=1, keepdims=True)
    a2 = jnp.sum(jnp.where(hit2, aff, 0.0), axis=1, keepdims=True)
    den = a1 + a2
    route_ref[0] = (jnp.where(lane == 0, i1b, 0.0) + jnp.where(lane == 1, i2b, 0.0)
                    + jnp.where(lane == 2, a1 / den, 0.0) + jnp.where(lane == 3, a2 / den, 0.0))


def _merge_call(x, o, z, mod, g1, g2, w_gate, b_gate, w_ao, w_fo, w_out, w_r, b_r, n_batch):
    bsz, n_tok, _ = x.shape
    nt = n_tok // TM
    tok = lambda w: pl.BlockSpec((1, TM, w), lambda b, t: (b, t, 0))
    full = lambda *shape: pl.BlockSpec(shape, lambda b, t: (0,) * len(shape))
    return pl.pallas_call(
        _merge_kernel,
        grid=(bsz, nt),
        in_specs=[
            tok(D_MODEL), tok(V_WIDTH), tok(F_WIDTH),
            pl.BlockSpec((1, N_MOD, D_MODEL),
                         lambda b, t: (jnp.where(t == nt - 1, n_batch, b), 0, 0)),
            full(1, D_MODEL), full(1, D_MODEL),
            full(D_MODEL, 2 * D_MODEL), full(1, 2 * D_MODEL),
            full(V_WIDTH, D_MODEL), full(F_WIDTH, D_MODEL), full(D_MODEL, D_MODEL),
            full(D_MODEL, 2 * LANES), full(1, LANES),
        ],
        out_specs=[tok(D_MODEL), tok(D_MODEL), tok(LANES)],
        out_shape=[
            jax.ShapeDtypeStruct((bsz, n_tok, D_MODEL), F32),
            jax.ShapeDtypeStruct((bsz, n_tok, D_MODEL), F32),
            jax.ShapeDtypeStruct((bsz, n_tok, LANES), F32),
        ],
        compiler_params=_cparams(2),
        name="merge_route",
    )(x, o, z, mod, g1, g2, w_gate, b_gate, w_ao, w_fo, w_out, w_r, b_r)


def _slot_copy(src_ref, src_row, dst_ref, dst_row, sem):
    return pltpu.make_async_copy(src_ref.at[pl.ds(src_row, 1)], dst_ref.at[pl.ds(dst_row, 1)], sem)


def _dispatch_kernel(pos_ref, h_ref, xs_in, xs_ref, sem):
    del xs_in

    def copy(r, k):
        return _slot_copy(h_ref, r, xs_ref, pos_ref[0, 0, 2 * r + k], sem)

    def issue(r, carry):
        copy(r, 0).start(priority=0)
        copy(r, 1).start(priority=1)
        return carry

    def drain(r, carry):
        copy(r, 0).wait()
        copy(r, 1).wait()
        return carry

    lax.fori_loop(0, TSLOT, issue, 0, unroll=DMA_UNROLL)
    lax.fori_loop(0, TSLOT, drain, 0, unroll=DMA_UNROLL)


def _experts_kernel(te_ref, nv_ref, xs_ref, wg_ref, wu_ref, wd_ref, ys_ref):
    del te_ref
    t = pl.program_id(0)

    @pl.when(t < nv_ref[0])
    def _():
        h = xs_ref[...].astype(BF16)
        a = jnp.dot(h, wg_ref[0], preferred_element_type=F32)
        u = jnp.dot(h, wu_ref[0], preferred_element_type=F32)
        he = (a * jax.nn.sigmoid(a) * u).astype(BF16)
        ys_ref[...] = jnp.dot(he, wd_ref[0], preferred_element_type=F32)

    @pl.when(t >= nv_ref[0])
    def _():
        ys_ref[...] = jnp.zeros(ys_ref.shape, F32)


def _combine_kernel(pos_ref, route_ref, ys_ref, y_ref, buf_ref, sem):
    def copy(r, k):
        return _slot_copy(ys_ref, pos_ref[0, 0, 2 * r + k], buf_ref.at[k], r, sem)

    def issue(r, carry):
        copy(r, 0).start(priority=0)
        copy(r, 1).start(priority=1)
        return carry

    def drain(r, carry):
        copy(r, 0).wait()
        copy(r, 1).wait()
        return carry

    lax.fori_loop(0, TSLOT, issue, 0, unroll=DMA_UNROLL)
    lax.fori_loop(0, TSLOT, drain, 0, unroll=DMA_UNROLL)
    route = route_ref[...]
    y_ref[...] = route[:, 2:3] * buf_ref[0] + route[:, 3:4] * buf_ref[1]


def _moe_call(h2, route, w_eg, w_eu, w_ed):
    n = h2.shape[0]
    n_slots = 2 * n
    n_rows = n_slots + N_EXPERTS * TMR
    n_tiles = n_rows // TMR

    e_slot = route[:, :2].astype(jnp.int32).reshape(n_slots)
    onehot = (e_slot[:, None] == jnp.arange(N_EXPERTS, dtype=jnp.int32)[None, :]).astype(jnp.int32)
    csum = jnp.cumsum(onehot, axis=0)
    rank = jnp.sum(csum * onehot, axis=1) - 1
    counts = csum[-1]
    padded = ((counts + TMR - 1) // TMR) * TMR
    ends = jnp.cumsum(padded)
    pos = jnp.sum(onehot * (ends - padded)[None, :], axis=1) + rank
    tile_expert = jnp.minimum(
        jnp.sum((jnp.arange(n_tiles, dtype=jnp.int32) * TMR)[:, None] >= ends[None, :], axis=1),
        N_EXPERTS - 1).astype(jnp.int32)
    n_valid = (ends[-1] // TMR).astype(jnp.int32).reshape(1)
    pos = jnp.clip(pos, 0, n_rows - 1)
    pos_tiles = pos.astype(jnp.int32).reshape(n // TSLOT, 1, 2 * TSLOT)

    pos_spec = pl.BlockSpec((1, 1, 2 * TSLOT), lambda t: (t, 0, 0), memory_space=pltpu.SMEM)
    xs = pl.pallas_call(
        _dispatch_kernel,
        grid=(n // TSLOT,),
        in_specs=[pos_spec,
                  pl.BlockSpec((TSLOT, D_MODEL), lambda t: (t, 0)),
                  pl.BlockSpec(memory_space=pl.ANY)],
        out_specs=pl.BlockSpec(memory_space=pl.ANY),
        out_shape=jax.ShapeDtypeStruct((n_rows, D_MODEL), F32),
        scratch_shapes=[pltpu.SemaphoreType.DMA],
        input_output_aliases={2: 0},
        compiler_params=_cparams(1),
        name="dispatch",
    )(pos_tiles, h2, jnp.zeros((n_rows, D_MODEL), F32))

    w_map = lambda t, te, nv: (te[t], 0, 0)
    ys = pl.pallas_call(
        _experts_kernel,
        grid_spec=pltpu.PrefetchScalarGridSpec(
            num_scalar_prefetch=2,
            grid=(n_tiles,),
            in_specs=[
                pl.BlockSpec((TMR, D_MODEL), lambda t, te, nv: (t, 0)),
                pl.BlockSpec((1, D_MODEL, D_EXPERT), w_map),
                pl.BlockSpec((1, D_MODEL, D_EXPERT), w_map),
                pl.BlockSpec((1, D_EXPERT, D_MODEL), w_map),
            ],
            out_specs=pl.BlockSpec((TMR, D_MODEL), lambda t, te, nv: (t, 0)),
        ),
        out_shape=jax.ShapeDtypeStruct((n_rows, D_MODEL), F32),
        compiler_params=_cparams(1),
        name="experts",
    )(tile_expert, n_valid, xs, w_eg, w_eu, w_ed)

    return pl.pallas_call(
        _combine_kernel,
        grid=(n // TSLOT,),
        in_specs=[pos_spec,
                  pl.BlockSpec((TSLOT, LANES), lambda t: (t, 0)),
                  pl.BlockSpec(memory_space=pl.ANY)],
        out_specs=pl.BlockSpec((TSLOT, D_MODEL), lambda t: (t, 0)),
        out_shape=jax.ShapeDtypeStruct((n, D_MODEL), F32),
        scratch_shapes=[pltpu.VMEM((2, TSLOT, D_MODEL), F32), pltpu.SemaphoreType.DMA],
        compiler_params=_cparams(1),
        name="combine",
    )(pos_tiles, route, ys)


def _final_kernel(x_ref, y_ref, mod_ref, g_ref, o_ref):
    x = x_ref[0] + mod_ref[0][5:6] * y_ref[0]
    o_ref[0] = x * lax.rsqrt(jnp.mean(x * x, axis=-1, keepdims=True) + EPS) * g_ref[...]


def _final_call(x, y, mod, g_final, n_lat_tok):
    bsz = x.shape[0]
    tok = pl.BlockSpec((1, TM, D_MODEL), lambda b, t: (b, t, 0))
    return pl.pallas_call(
        _final_kernel,
        grid=(bsz, n_lat_tok // TM),
        in_specs=[tok, tok,
                  pl.BlockSpec((1, N_MOD, D_MODEL), lambda b, t: (b, 0, 0)),
                  pl.BlockSpec((1, D_MODEL), lambda b, t: (0, 0))],
        out_specs=tok,
        out_shape=jax.ShapeDtypeStruct((bsz, n_lat_tok, D_MODEL), F32),
        compiler_params=_cparams(2),
        name="final_norm",
    )(x, y, mod, g_final)


def _rope_tables(n_lat_tok):
    t = np.arange(n_lat_tok)
    pos = np.stack([t // GRID_W, t % GRID_W], axis=0).astype(np.float64)
    inv = ROPE_THETA ** (-np.arange(ROPE_HALF, dtype=np.float64) / ROPE_HALF)
    d = np.arange(LANES) % HEAD_DIM
    axis = d // (2 * ROPE_HALF)
    freq = d % ROPE_HALF
    ang = pos[axis, :].T * inv[freq][None, :]
    sign = np.where((d % (2 * ROPE_HALF)) < ROPE_HALF, -1.0, 1.0)
    cos = np.concatenate([np.cos(ang), np.ones((CTX_LEN, LANES))], axis=0)
    sin = np.concatenate([np.sin(ang) * sign[None, :], np.zeros((CTX_LEN, LANES))], axis=0)
    return jnp.asarray(cos, F32), jnp.asarray(sin, F32)


def kernel(x, c, ctx, c_ctx, w_ada, b_ada, g_norm1, w_in, lam_qk, g_subln, w_attn_out, w_four_out,
           w_gate, b_gate, w_out, g_norm2, w_router, b_router, w_e_gate, w_e_up, w_e_down, g_final):
    bsz, n_lat_tok, d = x.shape
    depth = w_ada.shape[0]
    assert d == D_MODEL and ctx.shape[1] == CTX_LEN and bsz < MOD_ROWS
    assert n_lat_tok % TQ == 0 and n_lat_tok % (DFT2 * DFT_GROUP) == 0
    n_tok = n_lat_tok + CTX_LEN
    assert (bsz * n_tok) % TSLOT == 0

    cond = jnp.zeros((MOD_ROWS, D_MODEL), F32).at[:bsz].set(c).at[bsz].set(c_ctx)
    mods = _ada_call(cond, w_ada, b_ada).reshape(depth, MOD_ROWS, N_MOD, D_MODEL)
    cos_t, sin_t = _rope_tables(n_lat_tok)
    w_r = jnp.zeros((D_MODEL, LANES), F32).at[:, :N_EXPERTS].set(w_router)
    w_r_hi = w_r.astype(BF16)
    w_r = jnp.concatenate([w_r_hi, (w_r - w_r_hi.astype(F32)).astype(BF16)], axis=1)
    b_r = jnp.zeros((1, LANES), F32).at[0, :N_EXPERTS].set(b_router)

    xs = jnp.concatenate([x, ctx], axis=1)
    y = None
    for i in range(depth):
        lam_init = 0.8 - 0.6 * math.exp(-0.3 * i)
        xs, q, k, v, f = _proj_call(
            xs, y, mods[i - 1] if i else None, mods[i], g_norm1[i].reshape(1, D_MODEL),
            w_in[i].astype(BF16), cos_t, sin_t, bsz)
        o = _attn_call(q, k, v, lam_qk[i], g_subln[i], lam_init)
        z = _fourier_call(f, n_lat_tok)
        xs, h2, route = _merge_call(
            xs, o, z, mods[i], g_norm1[i].reshape(1, D_MODEL), g_norm2[i].reshape(1, D_MODEL),
            w_gate[i].astype(BF16), b_gate[i].reshape(1, 2 * D_MODEL), w_attn_out[i].astype(BF16),
            w_four_out[i].astype(BF16), w_out[i].astype(BF16), w_r, b_r, bsz)
        y = _moe_call(h2.reshape(bsz * n_tok, D_MODEL), route.reshape(bsz * n_tok, LANES),
                      w_e_gate[i].astype(BF16), w_e_up[i].astype(BF16), w_e_down[i].astype(BF16))
        y = y.reshape(bsz, n_tok, D_MODEL)
    return _final_call(xs, y, mods[depth - 1], g_final.reshape(1, D_MODEL), n_lat_tok)
```

```python
import functools
import math

import numpy as np
import jax
import jax.numpy as jnp
from jax import lax
from jax.experimental import pallas as pl
from jax.experimental.pallas import tpu as pltpu

F32 = jnp.float32
BF16 = jnp.bfloat16
HIGHEST = lax.Precision.HIGHEST

D_MODEL = 1024
GRID_W = 64
CTX_LEN = 256
N_HEADS = 8
HEAD_DIM = 64
V_DIM = 128
QK_WIDTH = 1024
V_WIDTH = 1024
F_GROUPS = 4
F_WIDTH = 512
F_GROUP_DIM = 128
IN_WIDTH = 3584
ROPE_THETA = 10000.0
ROPE_HALF = 16
N_EXPERTS = 16
N_GROUPS = 4
EXPERTS_PER_GROUP = 4
D_EXPERT = 512
EPS = 1e-6
N_MOD = 6
MOD_ROWS = 8

LANES = 128
TM = 256
TQ = 1024
MXU_TILE = 256
BF16_ROWS = 16
SUM_ROWS = 16
MAX_KEY_TILES = 16
QK_ROWS = 512
LAG_LIMIT = 100.0
SUM_FLOOR = 2.0 ** -60
TMR = 256
TSLOT = 512
DMA_UNROLL = 8
ADA_TN = 512
DFT2 = 128
DFT_GROUP = 8
Q_SCALE = (HEAD_DIM ** -0.5) * math.log2(math.e)
VMEM_LIMIT = 56 * 1024 * 1024


def _cparams(n_axes):
    return pltpu.CompilerParams(dimension_semantics=("arbitrary",) * n_axes,
                                vmem_limit_bytes=VMEM_LIMIT)


def _rms_mod(x, g, shift, scale):
    y = x * lax.rsqrt(jnp.mean(x * x, axis=-1, keepdims=True) + EPS) * g
    return y * (1.0 + scale) + shift


def _ada_kernel(c_ref, w_ref, b_ref, o_ref):
    c = c_ref[...]
    s = c * jax.nn.sigmoid(c)
    o_ref[0] = jnp.dot(s, w_ref[0], preferred_element_type=F32, precision=HIGHEST) + b_ref[0]


def _ada_call(cond, w_ada, b_ada):
    depth = w_ada.shape[0]
    n_out = w_ada.shape[2]
    return pl.pallas_call(
        _ada_kernel,
        grid=(depth, n_out // ADA_TN),
        in_specs=[
            pl.BlockSpec((MOD_ROWS, D_MODEL), lambda l, j: (0, 0)),
            pl.BlockSpec((1, D_MODEL, ADA_TN), lambda l, j: (l, 0, j)),
            pl.BlockSpec((1, 1, ADA_TN), lambda l, j: (l, 0, j)),
        ],
        out_specs=pl.BlockSpec((1, MOD_ROWS, ADA_TN), lambda l, j: (l, 0, j)),
        out_shape=jax.ShapeDtypeStruct((depth, MOD_ROWS, n_out), F32),
        compiler_params=_cparams(2),
        name="ada",
    )(cond, w_ada, b_ada.reshape(depth, 1, n_out))


def _proj_kernel(*refs, has_y):
    if has_y:
        (x_ref, y_ref, pmod_ref, mod_ref, g1_ref, win_ref, cos_ref, sin_ref,
         xo_ref, q_ref, k_ref, v_ref, f_ref) = refs
        x = x_ref[0] + pmod_ref[0][5:6] * y_ref[0]
        xo_ref[0] = x
    else:
        (x_ref, mod_ref, g1_ref, win_ref, cos_ref, sin_ref,
         q_ref, k_ref, v_ref, f_ref) = refs
        x = x_ref[0]
    mod = mod_ref[0]
    hb = _rms_mod(x, g1_ref[...], mod[0:1], mod[1:2]).astype(BF16)
    cos = cos_ref[...]
    sin = sin_ref[...]
    lane = lax.broadcasted_iota(jnp.int32, (TM, LANES), 1)
    first = (lane % (2 * ROPE_HALF)) < ROPE_HALF

    def rope(xc):
        up = pltpu.roll(xc, LANES - ROPE_HALF, 1)
        dn = pltpu.roll(xc, ROPE_HALF, 1)
        return xc * cos + jnp.where(first, up, dn) * sin

    w2 = 2 * LANES
    for c in range(QK_WIDTH // w2):
        uq = jnp.dot(hb, win_ref[:, c * w2:(c + 1) * w2], preferred_element_type=F32)
        uk = jnp.dot(hb, win_ref[:, QK_WIDTH + c * w2:QK_WIDTH + (c + 1) * w2],
                     preferred_element_type=F32)
        for j in range(2):
            lo = c * w2 + j * LANES
            q_ref[0, :, lo:lo + LANES] = (rope(uq[:, j * LANES:(j + 1) * LANES]) * Q_SCALE).astype(BF16)
            k_ref[0, :, lo:lo + LANES] = rope(uk[:, j * LANES:(j + 1) * LANES]).astype(BF16)
    for c in range(V_WIDTH // w2):
        lo = 2 * QK_WIDTH + c * w2
        v_ref[0, :, c * w2:(c + 1) * w2] = jnp.dot(
            hb, win_ref[:, lo:lo + w2], preferred_element_type=F32).astype(BF16)
    for c in range(F_WIDTH // w2):
        lo = 2 * QK_WIDTH + V_WIDTH + c * w2
        f_ref[0, :, c * w2:(c + 1) * w2] = jnp.dot(
            hb, win_ref[:, lo:lo + w2], preferred_element_type=F32)


def _proj_call(x, y, pmod, mod, g1, w_in, cos_t, sin_t, n_batch):
    bsz, n_tok, _ = x.shape
    nt = n_tok // TM
    has_y = y is not None

    def mod_map(b, t):
        return (jnp.where(t == nt - 1, n_batch, b), 0, 0)

    tok = lambda w: pl.BlockSpec((1, TM, w), lambda b, t: (b, t, 0))
    modspec = pl.BlockSpec((1, N_MOD, D_MODEL), mod_map)
    in_specs = [tok(D_MODEL)]
    args = [x]
    if has_y:
        in_specs += [tok(D_MODEL), modspec]
        args += [y, pmod]
    in_specs += [
        modspec,
        pl.BlockSpec((1, D_MODEL), lambda b, t: (0, 0)),
        pl.BlockSpec((D_MODEL, IN_WIDTH), lambda b, t: (0, 0)),
        pl.BlockSpec((TM, LANES), lambda b, t: (t, 0)),
        pl.BlockSpec((TM, LANES), lambda b, t: (t, 0)),
    ]
    args += [mod, g1, w_in, cos_t, sin_t]
    out_specs = [tok(QK_WIDTH), tok(QK_WIDTH), tok(V_WIDTH), tok(F_WIDTH)]
    out_shape = [
        jax.ShapeDtypeStruct((bsz, n_tok, QK_WIDTH), BF16),
        jax.ShapeDtypeStruct((bsz, n_tok, QK_WIDTH), BF16),
        jax.ShapeDtypeStruct((bsz, n_tok, V_WIDTH), BF16),
        jax.ShapeDtypeStruct((bsz, n_tok, F_WIDTH), F32),
    ]
    if has_y:
        out_specs = [tok(D_MODEL)] + out_specs
        out_shape = [jax.ShapeDtypeStruct((bsz, n_tok, D_MODEL), F32)] + out_shape
    outs = pl.pallas_call(
        functools.partial(_proj_kernel, has_y=has_y),
        grid=(bsz, nt),
        in_specs=in_specs,
        out_specs=out_specs,
        out_shape=out_shape,
        compiler_params=_cparams(2),
        name="proj",
    )(*args)
    if has_y:
        return outs
    return [x] + list(outs)


def _split_maps_t(qt):
    row = lax.broadcasted_iota(jnp.int32, qt.shape, 0)
    zero = jnp.zeros_like(qt)
    return jnp.concatenate([jnp.where(row < HEAD_DIM, qt, zero),
                            jnp.where(row >= HEAD_DIM, qt, zero)], axis=1)


def _ones_rows(n_cols):
    row = lax.broadcasted_iota(jnp.int32, (SUM_ROWS, n_cols), 0)
    return jnp.where(row == 0, 1.0, 0.0).astype(BF16)


def _vt_block(vt, n_keys):
    return jnp.concatenate([vt, _ones_rows(n_keys)], axis=0)


def _softmax_block_t(q2t, kb, vbt_ext, m_prev, acc_prev):
    st = jnp.dot(kb, q2t, preferred_element_type=F32)
    m_new = jnp.maximum(m_prev, jnp.max(st, axis=0, keepdims=True))
    alpha = jnp.exp2(m_prev - m_new)
    pt = jnp.exp2(st - m_new).astype(BF16)
    acc = alpha * acc_prev + jnp.dot(vbt_ext, pt, preferred_element_type=F32)
    return m_new, acc


def _diff_combine_t(acc, lam_ref, gs_ref, lam_init):
    tq = acc.shape[1] // 2
    lp = lam_ref[...]
    lam = (jnp.exp(jnp.sum(lp[0:1] * lp[1:2], axis=1, keepdims=True))
           - jnp.exp(jnp.sum(lp[2:3] * lp[3:4], axis=1, keepdims=True)) + lam_init)
    pv = acc[:V_DIM]
    l = acc[V_DIM:V_DIM + 1]
    ot = pv[:, :tq] / l[:, :tq] - lam * (pv[:, tq:] / l[:, tq:])
    ot = ot * lax.rsqrt(jnp.mean(ot * ot, axis=0, keepdims=True) + EPS) * gs_ref[...]
    return (ot * (1.0 - lam_init)).T.astype(BF16)


def _attn_kernel(lam_ref, gs_ref, q_ref, k_ref, v_ref, o_ref, vt_ref, q2t_ref, m_ref, acc_ref,
                 lag_ref, pt_ref, *, n_lat_tok, tk, lam_init):
    n_tok = n_lat_tok + CTX_LEN

    @pl.when(pl.program_id(2) == 0)
    def _():
        for j in range(n_tok // MXU_TILE):
            sl = slice(j * MXU_TILE, (j + 1) * MXU_TILE)
            vt_ref[:, sl] = v_ref[0, sl, :].astype(F32).T.astype(BF16)

    q2t_ref[...] = _split_maps_t(q_ref[0].astype(F32).T.astype(BF16))

    def reset(m0):
        m_ref[...] = jnp.full(m_ref.shape, m0, F32)
        acc_ref[...] = jnp.zeros(acc_ref.shape, F32)

    def exact_chunk(off):
        m_new, acc = _softmax_block_t(
            q2t_ref[...], k_ref[0, pl.ds(off, MXU_TILE), :],
            _vt_block(vt_ref[:, pl.ds(off, MXU_TILE)], MXU_TILE), m_ref[...], acc_ref[...])
        m_ref[...] = m_new
        acc_ref[...] = acc

    def lagged_block(off, n_keys):
        m_prev = m_ref[...]
        pmax = None
        for r in range(0, n_keys, QK_ROWS):
            rows = min(QK_ROWS, n_keys - r)
            st = jnp.dot(k_ref[0, pl.ds(off + r, rows), :], q2t_ref[...], preferred_element_type=F32)
            p = jnp.exp2(st - m_prev).astype(BF16)
            pt_ref[r:r + rows, :] = p
            for t in range(0, rows, BF16_ROWS):
                tile = p[t:t + BF16_ROWS, :]
                pmax = tile if pmax is None else jnp.maximum(pmax, tile)
        acc = acc_ref[...] + jnp.dot(_vt_block(vt_ref[:, pl.ds(off, n_keys)], n_keys), pt_ref[0:n_keys, :],
                                     preferred_element_type=F32)
        excess = jnp.log2(jnp.max(pmax.astype(F32), axis=0, keepdims=True))
        m_new = m_prev + jnp.maximum(excess, 0.0)
        acc_ref[...] = acc * jnp.exp2(m_prev - m_new)
        m_ref[...] = m_new
        lag_ref[...] = jnp.maximum(lag_ref[...], excess)

    reset(0.0)
    lag_ref[...] = jnp.full(lag_ref.shape, -jnp.inf, F32)
    n_full = n_tok // tk - 1
    lax.fori_loop(0, n_full, lambda j, carry: (lagged_block(pl.multiple_of(j * tk, tk), tk), carry)[1], 0)
    lagged_block(n_full * tk, n_tok - n_full * tk)

    sum_min = jnp.min(acc_ref[V_DIM:V_DIM + 1, :])

    @pl.when(jnp.logical_or(jnp.max(lag_ref[...]) > LAG_LIMIT, sum_min < SUM_FLOOR))
    def _():
        reset(-jnp.inf)
        lax.fori_loop(0, n_tok // MXU_TILE,
                      lambda c, carry: (exact_chunk(pl.multiple_of(c * MXU_TILE, MXU_TILE)), carry)[1], 0)

    o_ref[0] = _diff_combine_t(acc_ref[...], lam_ref, gs_ref, lam_init)


def _attn_ctx_kernel(lam_ref, gs_ref, q_ref, k_ref, v_ref, o_any, o_ref, *, lam_init):
    del o_any
    q2t = _split_maps_t(q_ref[0].astype(F32).T.astype(BF16))
    vbt = _vt_block(v_ref[0].astype(F32).T.astype(BF16), CTX_LEN)
    m0 = jnp.full((1, 2 * CTX_LEN), -jnp.inf, F32)
    acc0 = jnp.zeros((V_DIM + SUM_ROWS, 2 * CTX_LEN), F32)
    _, acc = _softmax_block_t(q2t, k_ref[0], vbt, m0, acc0)
    o_ref[0] = _diff_combine_t(acc, lam_ref, gs_ref, lam_init)


def _key_block(n_tok):
    for mult in range(MAX_KEY_TILES, 0, -1):
        if n_tok % (mult * MXU_TILE) == 0:
            return mult * MXU_TILE
    raise ValueError(n_tok)


def _attn_call(q, k, v, lam_qk, g_subln, lam_init):
    bsz, n_tok, _ = q.shape
    n_lat_tok = n_tok - CTX_LEN
    tk = _key_block(n_lat_tok)
    tq_step = TQ
    n_col = 2 * TQ
    small = [pl.BlockSpec((4, HEAD_DIM), lambda *_: (0, 0)),
             pl.BlockSpec((V_DIM, 1), lambda *_: (0, 0))]
    gs = g_subln.reshape(V_DIM, 1)
    o = pl.pallas_call(
        functools.partial(_attn_kernel, n_lat_tok=n_lat_tok, tk=tk, lam_init=lam_init),
        grid=(bsz, N_HEADS, n_lat_tok // tq_step),
        in_specs=small + [
            pl.BlockSpec((1, tq_step, LANES), lambda b, h, t: (b, t, h)),
            pl.BlockSpec((1, n_tok, LANES), lambda b, h, t: (b, 0, h)),
            pl.BlockSpec((1, n_tok, LANES), lambda b, h, t: (b, 0, h)),
        ],
        out_specs=pl.BlockSpec((1, tq_step, V_DIM), lambda b, h, t: (b, t, h)),
        out_shape=jax.ShapeDtypeStruct((bsz, n_tok, V_WIDTH), BF16),
        scratch_shapes=[
            pltpu.VMEM((V_DIM, n_tok), BF16),
            pltpu.VMEM((LANES, n_col), BF16),
            pltpu.VMEM((1, n_col), F32),
            pltpu.VMEM((V_DIM + SUM_ROWS, n_col), F32),
            pltpu.VMEM((1, n_col), F32),
            pltpu.VMEM((n_tok - (n_tok // tk - 1) * tk, n_col), BF16),
        ],
        compiler_params=_cparams(3),
        name="diff_attn",
    )(lam_qk, gs, q, k, v)
    ctx_blk = n_lat_tok // CTX_LEN
    ctx_spec = pl.BlockSpec((1, CTX_LEN, LANES), lambda b, h: (b, ctx_blk, h))
    return pl.pallas_call(
        functools.partial(_attn_ctx_kernel, lam_init=lam_init),
        grid=(bsz, N_HEADS),
        in_specs=small + [ctx_spec, ctx_spec, ctx_spec, pl.BlockSpec(memory_space=pl.ANY)],
        out_specs=ctx_spec,
        out_shape=jax.ShapeDtypeStruct((bsz, n_tok, V_WIDTH), BF16),
        input_output_aliases={5: 0},
        compiler_params=_cparams(2),
        name="diff_attn_ctx",
    )(lam_qk, gs, q, k, v, o)


def _dft1_kernel(u_ref, c1_ref, s1_ref, twc_ref, tws_ref, o_ref):
    n1 = u_ref.shape[1]
    rows = n1 * DFT_GROUP
    u = u_ref[0].reshape(rows, F_WIDTH).astype(BF16)
    ar = jnp.dot(c1_ref[...], u, preferred_element_type=F32)
    ai = -jnp.dot(s1_ref[...], u, preferred_element_type=F32)
    twc = twc_ref[0]
    tws = tws_ref[0]
    out = []
    for part in (lambda a_r, a_i: a_r * twc + a_i * tws, lambda a_r, a_i: a_i * twc - a_r * tws):
        for g in range(F_GROUPS):
            sl = slice(g * LANES, (g + 1) * LANES)
            out.append(part(ar[:, sl], ai[:, sl]))
    o_ref[0] = jnp.concatenate(out, axis=1).reshape(n1, DFT_GROUP, 2 * F_WIDTH)


def _channel_dft(gr, gi, cc_ref, sc_ref, norm):
    grb = gr.astype(BF16)
    gib = gi.astype(BF16)
    out = []
    for g in range(F_GROUPS):
        sl = slice(g * LANES, (g + 1) * LANES)
        out.append((jnp.dot(grb[:, sl], cc_ref[...], preferred_element_type=F32)
                    + jnp.dot(gib[:, sl], sc_ref[...], preferred_element_type=F32)) * norm)
    return out


def _dft2_kernel(b_ref, c2_ref, s2_ref, cc_ref, sc_ref, o_ref, *, norm):
    for j in range(DFT_GROUP):
        bb = b_ref[0, j].astype(BF16)
        cb = jnp.dot(c2_ref[...], bb, preferred_element_type=F32)
        sb = jnp.dot(s2_ref[...], bb, preferred_element_type=F32)
        gr = cb[:, :F_WIDTH] + sb[:, F_WIDTH:]
        gi = cb[:, F_WIDTH:] - sb[:, :F_WIDTH]
        for g, z in enumerate(_channel_dft(gr, gi, cc_ref, sc_ref, norm)):
            o_ref[0, :, j, g * LANES:(g + 1) * LANES] = z


def _dft_ctx_kernel(f_ref, c_ref, s_ref, cc_ref, sc_ref, z_any, o_ref, *, norm):
    del z_any
    f = f_ref[0].astype(BF16)
    gr = jnp.dot(c_ref[...], f, preferred_element_type=F32)
    gi = -jnp.dot(s_ref[...], f, preferred_element_type=F32)
    for g, z in enumerate(_channel_dft(gr, gi, cc_ref, sc_ref, norm)):
        o_ref[0, :, g * LANES:(g + 1) * LANES] = z


def _dft_mats(n, repeat=1):
    idx = np.arange(n, dtype=np.float64)
    ang = 2.0 * np.pi * np.outer(idx, idx) / n
    eye = np.eye(repeat)
    return (jnp.asarray(np.kron(np.cos(ang), eye), BF16), jnp.asarray(np.kron(np.sin(ang), eye), BF16))


def _fourier_call(f, n_lat_tok):
    bsz, n_tok, _ = f.shape
    n1 = n_lat_tok // DFT2
    full = lambda *shape: pl.BlockSpec(shape, lambda *_: (0,) * len(shape))
    c1, s1 = _dft_mats(n1, DFT_GROUP)
    c2, s2 = _dft_mats(DFT2)
    cc, sc = _dft_mats(F_GROUP_DIM)
    cctx, sctx = _dft_mats(CTX_LEN)
    tw_ang = (2.0 * np.pi / n_lat_tok) * np.einsum(
        "sj,k->skj", np.arange(DFT2, dtype=np.float64).reshape(DFT2 // DFT_GROUP, DFT_GROUP),
        np.arange(n1, dtype=np.float64)).reshape(DFT2 // DFT_GROUP, n1 * DFT_GROUP)
    twc = jnp.asarray(np.broadcast_to(np.cos(tw_ang)[:, :, None], tw_ang.shape + (LANES,)), F32)
    tws = jnp.asarray(np.broadcast_to(np.sin(tw_ang)[:, :, None], tw_ang.shape + (LANES,)), F32)

    fv = f.reshape(bsz, n_tok // DFT2, DFT2, F_WIDTH)
    mid = pl.pallas_call(
        _dft1_kernel,
        grid=(bsz, DFT2 // DFT_GROUP),
        in_specs=[
            pl.BlockSpec((1, n1, DFT_GROUP, F_WIDTH), lambda b, j: (b, 0, j, 0)),
            full(n1 * DFT_GROUP, n1 * DFT_GROUP), full(n1 * DFT_GROUP, n1 * DFT_GROUP),
            pl.BlockSpec((1, n1 * DFT_GROUP, LANES), lambda b, j: (j, 0, 0)),
            pl.BlockSpec((1, n1 * DFT_GROUP, LANES), lambda b, j: (j, 0, 0)),
        ],
        out_specs=pl.BlockSpec((1, n1, DFT_GROUP, 2 * F_WIDTH), lambda b, j: (b, 0, j, 0)),
        out_shape=jax.ShapeDtypeStruct((bsz, n1, DFT2, 2 * F_WIDTH), F32),
        compiler_params=_cparams(2),
        name="dft_outer",
    )(fv, c1, s1, twc, tws)

    z = pl.pallas_call(
        functools.partial(_dft2_kernel, norm=1.0 / math.sqrt(n_lat_tok * F_GROUP_DIM)),
        grid=(bsz, n1 // DFT_GROUP),
        in_specs=[
            pl.BlockSpec((1, DFT_GROUP, DFT2, 2 * F_WIDTH), lambda b, j: (b, j, 0, 0)),
            full(DFT2, DFT2), full(DFT2, DFT2),
            full(F_GROUP_DIM, F_GROUP_DIM), full(F_GROUP_DIM, F_GROUP_DIM),
        ],
        out_specs=pl.BlockSpec((1, DFT2, DFT_GROUP, F_WIDTH), lambda b, j: (b, 0, j, 0)),
        out_shape=jax.ShapeDtypeStruct((bsz, n_tok // n1, n1, F_WIDTH), F32),
        compiler_params=_cparams(2),
        name="dft_inner",
    )(mid, c2, s2, cc, sc)
    z = z.reshape(bsz, n_tok, F_WIDTH)

    ctx_blk = n_lat_tok // CTX_LEN
    z = pl.pallas_call(
        functools.partial(_dft_ctx_kernel, norm=1.0 / math.sqrt(CTX_LEN * F_GROUP_DIM)),
        grid=(bsz,),
        in_specs=[
            pl.BlockSpec((1, CTX_LEN, F_WIDTH), lambda b: (b, ctx_blk, 0)),
            full(CTX_LEN, CTX_LEN), full(CTX_LEN, CTX_LEN),
            full(F_GROUP_DIM, F_GROUP_DIM), full(F_GROUP_DIM, F_GROUP_DIM),
            pl.BlockSpec(memory_space=pl.ANY),
        ],
        out_specs=pl.BlockSpec((1, CTX_LEN, F_WIDTH), lambda b: (b, ctx_blk, 0)),
        out_shape=jax.ShapeDtypeStruct((bsz, n_tok, F_WIDTH), F32),
        input_output_aliases={5: 0},
        compiler_params=_cparams(1),
        name="dft_ctx",
    )(f, cctx, sctx, cc, sc, z)
    return z


def _merge_kernel(x_ref, o_ref, z_ref, mod_ref, g1_ref, g2_ref, wgate_ref, bgate_ref,
                  wao_ref, wfo_ref, wout_ref, wr_ref, br_ref, xo_ref, h2_ref, route_ref):
    x = x_ref[0]
    mod = mod_ref[0]
    hb = _rms_mod(x, g1_ref[...], mod[0:1], mod[1:2]).astype(BF16)
    gates = jax.nn.sigmoid(jnp.dot(hb, wgate_ref[...], preferred_element_type=F32) + bgate_ref[...])
    ya = jnp.dot(o_ref[0], wao_ref[...], preferred_element_type=F32)
    yf = jnp.dot(z_ref[0].astype(BF16), wfo_ref[...], preferred_element_type=F32)
    merged = gates[:, :D_MODEL] * ya + gates[:, D_MODEL:] * yf
    out = jnp.dot(merged.astype(BF16), wout_ref[...], preferred_element_type=F32)
    xn = x + mod[2:3] * out
    xo_ref[0] = xn
    h2 = _rms_mod(xn, g2_ref[...], mod[3:4], mod[4:5])
    h2_ref[0] = h2

    h2_hi = h2.astype(BF16)
    h2_lo = (h2 - h2_hi.astype(F32)).astype(BF16)
    part = jnp.dot(h2_hi, wr_ref[...], preferred_element_type=F32)
    logits = (part[:, :LANES] + part[:, LANES:]
              + jnp.dot(h2_lo, wr_ref[:, :LANES], preferred_element_type=F32))
    aff = jax.nn.sigmoid(logits)
    lane = lax.broadcasted_iota(jnp.int32, (TM, LANES), 1)
    lanef = lane.astype(F32)
    neg = jnp.full((TM, LANES), -jnp.inf, F32)
    sel = jnp.where(lane < N_EXPERTS, aff + br_ref[...], neg)
    grp = lane // EXPERTS_PER_GROUP
    best = i1b = i2b = None
    for g in range(N_GROUPS):
        sg = jnp.where(grp == g, sel, neg)
        m1 = jnp.max(sg, axis=1, keepdims=True)
        i1 = jnp.min(jnp.where(sg == m1, lanef, float(LANES)), axis=1, keepdims=True)
        sg2 = jnp.where(lanef == i1, neg, sg)
        m2 = jnp.max(sg2, axis=1, keepdims=True)
        i2 = jnp.min(jnp.where(sg2 == m2, lanef, float(LANES)), axis=1, keepdims=True)
        score = m1 + m2
        if g == 0:
            best, i1b, i2b = score, i1, i2
        else:
            upd = score > best
            best = jnp.where(upd, score, best)
            i1b = jnp.where(upd, i1, i1b)
            i2b = jnp.where(upd, i2, i2b)
    hit1 = lanef == i1b
    hit2 = lanef == i2b
    a1 = jnp.sum(jnp.where(hit1, aff, 0.0), axis=1, keepdims=True)
    a2 = jnp.sum(jnp.where(hit2, aff, 0.0), axis=1, keepdims=True)
    den = a1 + a2
    route_ref[0] = (jnp.where(lane == 0, i1b, 0.0) + jnp.where(lane == 1, i2b, 0.0)
                    + jnp.where(lane == 2, a1 / den, 0.0) + jnp.where(lane == 3, a2 / den, 0.0))


def _merge_call(x, o, z, mod, g1, g2, w_gate, b_gate, w_ao, w_fo, w_out, w_r, b_r, n_batch):
    bsz, n_tok, _ = x.shape
    nt = n_tok // TM
    tok = lambda w: pl.BlockSpec((1, TM, w), lambda b, t: (b, t, 0))
    full = lambda *shape: pl.BlockSpec(shape, lambda b, t: (0,) * len(shape))
    return pl.pallas_call(
        _merge_kernel,
        grid=(bsz, nt),
        in_specs=[
            tok(D_MODEL), tok(V_WIDTH), tok(F_WIDTH),
            pl.BlockSpec((1, N_MOD, D_MODEL),
                         lambda b, t: (jnp.where(t == nt - 1, n_batch, b), 0, 0)),
            full(1, D_MODEL), full(1, D_MODEL),
            full(D_MODEL, 2 * D_MODEL), full(1, 2 * D_MODEL),
            full(V_WIDTH, D_MODEL), full(F_WIDTH, D_MODEL), full(D_MODEL, D_MODEL),
            full(D_MODEL, 2 * LANES), full(1, LANES),
        ],
        out_specs=[tok(D_MODEL), tok(D_MODEL), tok(LANES)],
        out_shape=[
            jax.ShapeDtypeStruct((bsz, n_tok, D_MODEL), F32),
            jax.ShapeDtypeStruct((bsz, n_tok, D_MODEL), F32),
            jax.ShapeDtypeStruct((bsz, n_tok, LANES), F32),
        ],
        compiler_params=_cparams(2),
        name="merge_route",
    )(x, o, z, mod, g1, g2, w_gate, b_gate, w_ao, w_fo, w_out, w_r, b_r)


def _slot_copy(src_ref, src_row, dst_ref, dst_row, sem):
    return pltpu.make_async_copy(src_ref.at[pl.ds(src_row, 1)], dst_ref.at[pl.ds(dst_row, 1)], sem)


def _dispatch_kernel(pos_ref, h_ref, xs_in, xs_ref, sem):
    del xs_in

    def copy(r, k):
        return _slot_copy(h_ref, r, xs_ref, pos_ref[0, 0, 2 * r + k], sem)

    def issue(r, carry):
        copy(r, 0).start(priority=0)
        copy(r, 1).start(priority=1)
        return carry

    def drain(r, carry):
        copy(r, 0).wait()
        copy(r, 1).wait()
        return carry

    lax.fori_loop(0, TSLOT, issue, 0, unroll=DMA_UNROLL)
    lax.fori_loop(0, TSLOT, drain, 0, unroll=DMA_UNROLL)


def _experts_kernel(te_ref, nv_ref, xs_ref, wg_ref, wu_ref, wd_ref, ys_ref):
    del te_ref
    t = pl.program_id(0)

    @pl.when(t < nv_ref[0])
    def _():
        h = xs_ref[...].astype(BF16)
        a = jnp.dot(h, wg_ref[0].astype(BF16), preferred_element_type=F32)
        u = jnp.dot(h, wu_ref[0].astype(BF16), preferred_element_type=F32)
        he = (a * jax.nn.sigmoid(a) * u).astype(BF16)
        ys_ref[...] = jnp.dot(he, wd_ref[0].astype(BF16), preferred_element_type=F32)

    @pl.when(t >= nv_ref[0])
    def _():
        ys_ref[...] = jnp.zeros(ys_ref.shape, F32)


def _combine_kernel(pos_ref, route_ref, ys_ref, y_ref, buf_ref, sem):
    def copy(r, k):
        return _slot_copy(ys_ref, pos_ref[0, 0, 2 * r + k], buf_ref.at[k], r, sem)

    def issue(r, carry):
        copy(r, 0).start(priority=0)
        copy(r, 1).start(priority=1)
        return carry

    def drain(r, carry):
        copy(r, 0).wait()
        copy(r, 1).wait()
        return carry

    lax.fori_loop(0, TSLOT, issue, 0, unroll=DMA_UNROLL)
    lax.fori_loop(0, TSLOT, drain, 0, unroll=DMA_UNROLL)
    route = route_ref[...]
    y_ref[...] = route[:, 2:3] * buf_ref[0] + route[:, 3:4] * buf_ref[1]


def _moe_call(h2, route, w_eg, w_eu, w_ed):
    n = h2.shape[0]
    n_slots = 2 * n
    n_rows = n_slots + N_EXPERTS * TMR
    n_tiles = n_rows // TMR

    e_slot = route[:, :2].astype(jnp.int32).reshape(n_slots)
    onehot = (e_slot[:, None] == jnp.arange(N_EXPERTS, dtype=jnp.int32)[None, :]).astype(jnp.int32)
    csum = jnp.cumsum(onehot, axis=0)
    rank = jnp.sum(csum * onehot, axis=1) - 1
    counts = csum[-1]
    padded = ((counts + TMR - 1) // TMR) * TMR
    ends = jnp.cumsum(padded)
    pos = jnp.sum(onehot * (ends - padded)[None, :], axis=1) + rank
    tile_expert = jnp.minimum(
        jnp.sum((jnp.arange(n_tiles, dtype=jnp.int32) * TMR)[:, None] >= ends[None, :], axis=1),
        N_EXPERTS - 1).astype(jnp.int32)
    n_valid = (ends[-1] // TMR).astype(jnp.int32).reshape(1)
    pos = jnp.clip(pos, 0, n_rows - 1)
    pos_tiles = pos.astype(jnp.int32).reshape(n // TSLOT, 1, 2 * TSLOT)

    pos_spec = pl.BlockSpec((1, 1, 2 * TSLOT), lambda t: (t, 0, 0), memory_space=pltpu.SMEM)
    xs = pl.pallas_call(
        _dispatch_kernel,
        grid=(n // TSLOT,),
        in_specs=[pos_spec,
                  pl.BlockSpec((TSLOT, D_MODEL), lambda t: (t, 0)),
                  pl.BlockSpec(memory_space=pl.ANY)],
        out_specs=pl.BlockSpec(memory_space=pl.ANY),
        out_shape=jax.ShapeDtypeStruct((n_rows, D_MODEL), F32),
        scratch_shapes=[pltpu.SemaphoreType.DMA],
        input_output_aliases={2: 0},
        compiler_params=_cparams(1),
        name="dispatch",
    )(pos_tiles, h2, jnp.zeros((n_rows, D_MODEL), F32))

    w_map = lambda t, te, nv: (te[t], 0, 0)
    ys = pl.pallas_call(
        _experts_kernel,
        grid_spec=pltpu.PrefetchScalarGridSpec(
            num_scalar_prefetch=2,
            grid=(n_tiles,),
            in_specs=[
                pl.BlockSpec((TMR, D_MODEL), lambda t, te, nv: (t, 0)),
                pl.BlockSpec((1, D_MODEL, D_EXPERT), w_map),
                pl.BlockSpec((1, D_MODEL, D_EXPERT), w_map),
                pl.BlockSpec((1, D_EXPERT, D_MODEL), w_map),
            ],
            out_specs=pl.BlockSpec((TMR, D_MODEL), lambda t, te, nv: (t, 0)),
        ),
        out_shape=jax.ShapeDtypeStruct((n_rows, D_MODEL), F32),
        compiler_params=_cparams(1),
        name="experts",
    )(tile_expert, n_valid, xs, w_eg, w_eu, w_ed)

    return pl.pallas_call(
        _combine_kernel,
        grid=(n // TSLOT,),
        in_specs=[pos_spec,
                  pl.BlockSpec((TSLOT, LANES), lambda t: (t, 0)),
                  pl.BlockSpec(memory_space=pl.ANY)],
        out_specs=pl.BlockSpec((TSLOT, D_MODEL), lambda t: (t, 0)),
        out_shape=jax.ShapeDtypeStruct((n, D_MODEL), F32),
        scratch_shapes=[pltpu.VMEM((2, TSLOT, D_MODEL), F32), pltpu.SemaphoreType.DMA],
        compiler_params=_cparams(1),
        name="combine",
    )(pos_tiles, route, ys)


def _final_kernel(x_ref, y_ref, mod_ref, g_ref, o_ref):
    x = x_ref[0] + mod_ref[0][5:6] * y_ref[0]
    o_ref[0] = x * lax.rsqrt(jnp.mean(x * x, axis=-1, keepdims=True) + EPS) * g_ref[...]


def _final_call(x, y, mod, g_final, n_lat_tok):
    bsz = x.shape[0]
    tok = pl.BlockSpec((1, TM, D_MODEL), lambda b, t: (b, t, 0))
    return pl.pallas_call(
        _final_kernel,
        grid=(bsz, n_lat_tok // TM),
        in_specs=[tok, tok,
                  pl.BlockSpec((1, N_MOD, D_MODEL), lambda b, t: (b, 0, 0)),
                  pl.BlockSpec((1, D_MODEL), lambda b, t: (0, 0))],
        out_specs=tok,
        out_shape=jax.ShapeDtypeStruct((bsz, n_lat_tok, D_MODEL), F32),
        compiler_params=_cparams(2),
        name="final_norm",
    )(x, y, mod, g_final)


def _rope_tables(n_lat_tok):
    t = np.arange(n_lat_tok)
    pos = np.stack([t // GRID_W, t % GRID_W], axis=0).astype(np.float64)
    inv = ROPE_THETA ** (-np.arange(ROPE_HALF, dtype=np.float64) / ROPE_HALF)
    d = np.arange(LANES) % HEAD_DIM
    axis = d // (2 * ROPE_HALF)
    freq = d % ROPE_HALF
    ang = pos[axis, :].T * inv[freq][None, :]
    sign = np.where((d % (2 * ROPE_HALF)) < ROPE_HALF, -1.0, 1.0)
    cos = np.concatenate([np.cos(ang), np.ones((CTX_LEN, LANES))], axis=0)
    sin = np.concatenate([np.sin(ang) * sign[None, :], np.zeros((CTX_LEN, LANES))], axis=0)
    return jnp.asarray(cos, F32), jnp.asarray(sin, F32)


def kernel(x, c, ctx, c_ctx, w_ada, b_ada, g_norm1, w_in, lam_qk, g_subln, w_attn_out, w_four_out,
           w_gate, b_gate, w_out, g_norm2, w_router, b_router, w_e_gate, w_e_up, w_e_down, g_final):
    bsz, n_lat_tok, d = x.shape
    depth = w_ada.shape[0]
    assert d == D_MODEL and ctx.shape[1] == CTX_LEN and bsz < MOD_ROWS
    assert n_lat_tok % TQ == 0 and n_lat_tok % (DFT2 * DFT_GROUP) == 0
    n_tok = n_lat_tok + CTX_LEN
    assert (bsz * n_tok) % TSLOT == 0

    cond = jnp.zeros((MOD_ROWS, D_MODEL), F32).at[:bsz].set(c).at[bsz].set(c_ctx)
    mods = _ada_call(cond, w_ada, b_ada).reshape(depth, MOD_ROWS, N_MOD, D_MODEL)
    cos_t, sin_t = _rope_tables(n_lat_tok)
    w_r = jnp.zeros((D_MODEL, LANES), F32).at[:, :N_EXPERTS].set(w_router)
    w_r_hi = w_r.astype(BF16)
    w_r = jnp.concatenate([w_r_hi, (w_r - w_r_hi.astype(F32)).astype(BF16)], axis=1)
    b_r = jnp.zeros((1, LANES), F32).at[0, :N_EXPERTS].set(b_router)

    xs = jnp.concatenate([x, ctx], axis=1)
    y = None
    for i in range(depth):
        lam_init = 0.8 - 0.6 * math.exp(-0.3 * i)
        xs, q, k, v, f = _proj_call(
            xs, y, mods[i - 1] if i else None, mods[i], g_norm1[i].reshape(1, D_MODEL),
            w_in[i].astype(BF16), cos_t, sin_t, bsz)
        o = _attn_call(q, k, v, lam_qk[i], g_subln[i], lam_init)
        z = _fourier_call(f, n_lat_tok)
        xs, h2, route = _merge_call(
            xs, o, z, mods[i], g_norm1[i].reshape(1, D_MODEL), g_norm2[i].reshape(1, D_MODEL),
            w_gate[i].astype(BF16), b_gate[i].reshape(1, 2 * D_MODEL), w_attn_out[i].astype(BF16),
            w_four_out[i].astype(BF16), w_out[i].astype(BF16), w_r, b_r, bsz)
        y = _moe_call(h2.reshape(bsz * n_tok, D_MODEL), route.reshape(bsz * n_tok, LANES),
                      w_e_gate[i], w_e_up[i], w_e_down[i])
        y = y.reshape(bsz, n_tok, D_MODEL)
    return _final_call(xs, y, mods[depth - 1], g_final.reshape(1, D_MODEL), n_lat_tok)
```

```python
import functools
import math

import numpy as np
import jax
import jax.numpy as jnp
from jax import lax
from jax.experimental import pallas as pl
from jax.experimental.pallas import tpu as pltpu

F32 = jnp.float32
BF16 = jnp.bfloat16
HIGHEST = lax.Precision.HIGHEST

D_MODEL = 1024
GRID_W = 64
CTX_LEN = 256
N_HEADS = 8
HEAD_DIM = 64
V_DIM = 128
QK_WIDTH = 1024
V_WIDTH = 1024
F_GROUPS = 4
F_WIDTH = 512
F_GROUP_DIM = 128
IN_WIDTH = 3584
ROPE_THETA = 10000.0
ROPE_HALF = 16
N_EXPERTS = 16
N_GROUPS = 4
EXPERTS_PER_GROUP = 4
D_EXPERT = 512
EPS = 1e-6
N_MOD = 6
MOD_ROWS = 8

LANES = 128
SUBLANES = 8
TM = 256
TQ = 1024
MXU_TILE = 256
BF16_ROWS = 16
SUM_ROWS = 16
MAX_KEY_TILES = 16
QK_ROWS = 512
LAG_LIMIT = 100.0
SUM_FLOOR = 2.0 ** -60
TMR = 256
TSLOT = 512
DMA_UNROLL = 8
ADA_TN = 512
DFT2 = 128
DFT_GROUP = 8
Q_SCALE = (HEAD_DIM ** -0.5) * math.log2(math.e)
VMEM_LIMIT = 56 * 1024 * 1024


def _cparams(n_axes):
    return pltpu.CompilerParams(dimension_semantics=("arbitrary",) * n_axes,
                                vmem_limit_bytes=VMEM_LIMIT)


def _rms_mod(x, g, shift, scale):
    y = x * lax.rsqrt(jnp.mean(x * x, axis=-1, keepdims=True) + EPS) * g
    return y * (1.0 + scale) + shift


def _ada_kernel(c_ref, w_ref, b_ref, o_ref):
    c = c_ref[...]
    s = c * jax.nn.sigmoid(c)
    o_ref[0] = jnp.dot(s, w_ref[0], preferred_element_type=F32, precision=HIGHEST) + b_ref[0]


def _ada_call(cond, w_ada, b_ada):
    depth = w_ada.shape[0]
    n_out = w_ada.shape[2]
    return pl.pallas_call(
        _ada_kernel,
        grid=(depth, n_out // ADA_TN),
        in_specs=[
            pl.BlockSpec((MOD_ROWS, D_MODEL), lambda l, j: (0, 0)),
            pl.BlockSpec((1, D_MODEL, ADA_TN), lambda l, j: (l, 0, j)),
            pl.BlockSpec((1, 1, ADA_TN), lambda l, j: (l, 0, j)),
        ],
        out_specs=pl.BlockSpec((1, MOD_ROWS, ADA_TN), lambda l, j: (l, 0, j)),
        out_shape=jax.ShapeDtypeStruct((depth, MOD_ROWS, n_out), F32),
        compiler_params=_cparams(2),
        name="ada",
    )(cond, w_ada, b_ada.reshape(depth, 1, n_out))


def _proj_kernel(*refs, has_y):
    if has_y:
        (x_ref, y_ref, pmod_ref, mod_ref, g1_ref, win_ref, cos_ref, sin_ref,
         xo_ref, q_ref, k_ref, v_ref, f_ref) = refs
        x = x_ref[0] + pmod_ref[0][5:6] * y_ref[0]
        xo_ref[0] = x
    else:
        (x_ref, mod_ref, g1_ref, win_ref, cos_ref, sin_ref,
         q_ref, k_ref, v_ref, f_ref) = refs
        x = x_ref[0]
    mod = mod_ref[0]
    hb = _rms_mod(x, g1_ref[...], mod[0:1], mod[1:2]).astype(BF16)
    cos = cos_ref[...]
    sin = sin_ref[...]
    lane = lax.broadcasted_iota(jnp.int32, (TM, LANES), 1)
    first = (lane % (2 * ROPE_HALF)) < ROPE_HALF

    def rope(xc):
        up = pltpu.roll(xc, LANES - ROPE_HALF, 1)
        dn = pltpu.roll(xc, ROPE_HALF, 1)
        return xc * cos + jnp.where(first, up, dn) * sin

    w2 = 2 * LANES
    for c in range(QK_WIDTH // w2):
        uq = jnp.dot(hb, win_ref[:, c * w2:(c + 1) * w2], preferred_element_type=F32)
        uk = jnp.dot(hb, win_ref[:, QK_WIDTH + c * w2:QK_WIDTH + (c + 1) * w2],
                     preferred_element_type=F32)
        for j in range(2):
            lo = c * w2 + j * LANES
            q_ref[0, :, lo:lo + LANES] = (rope(uq[:, j * LANES:(j + 1) * LANES]) * Q_SCALE).astype(BF16)
            k_ref[0, :, lo:lo + LANES] = rope(uk[:, j * LANES:(j + 1) * LANES]).astype(BF16)
    for c in range(V_WIDTH // w2):
        lo = 2 * QK_WIDTH + c * w2
        v_ref[0, :, c * w2:(c + 1) * w2] = jnp.dot(
            hb, win_ref[:, lo:lo + w2], preferred_element_type=F32).astype(BF16)
    for c in range(F_WIDTH // w2):
        lo = 2 * QK_WIDTH + V_WIDTH + c * w2
        f_ref[0, :, c * w2:(c + 1) * w2] = jnp.dot(
            hb, win_ref[:, lo:lo + w2], preferred_element_type=F32)


def _proj_call(x, y, pmod, mod, g1, w_in, cos_t, sin_t, n_batch):
    bsz, n_tok, _ = x.shape
    nt = n_tok // TM
    has_y = y is not None

    def mod_map(b, t):
        return (jnp.where(t == nt - 1, n_batch, b), 0, 0)

    tok = lambda w: pl.BlockSpec((1, TM, w), lambda b, t: (b, t, 0))
    modspec = pl.BlockSpec((1, N_MOD, D_MODEL), mod_map)
    in_specs = [tok(D_MODEL)]
    args = [x]
    if has_y:
        in_specs += [tok(D_MODEL), modspec]
        args += [y, pmod]
    in_specs += [
        modspec,
        pl.BlockSpec((1, D_MODEL), lambda b, t: (0, 0)),
        pl.BlockSpec((D_MODEL, IN_WIDTH), lambda b, t: (0, 0)),
        pl.BlockSpec((TM, LANES), lambda b, t: (t, 0)),
        pl.BlockSpec((TM, LANES), lambda b, t: (t, 0)),
    ]
    args += [mod, g1, w_in, cos_t, sin_t]
    out_specs = [tok(QK_WIDTH), tok(QK_WIDTH), tok(V_WIDTH), tok(F_WIDTH)]
    out_shape = [
        jax.ShapeDtypeStruct((bsz, n_tok, QK_WIDTH), BF16),
        jax.ShapeDtypeStruct((bsz, n_tok, QK_WIDTH), BF16),
        jax.ShapeDtypeStruct((bsz, n_tok, V_WIDTH), BF16),
        jax.ShapeDtypeStruct((bsz, n_tok, F_WIDTH), F32),
    ]
    if has_y:
        out_specs = [tok(D_MODEL)] + out_specs
        out_shape = [jax.ShapeDtypeStruct((bsz, n_tok, D_MODEL), F32)] + out_shape
    outs = pl.pallas_call(
        functools.partial(_proj_kernel, has_y=has_y),
        grid=(bsz, nt),
        in_specs=in_specs,
        out_specs=out_specs,
        out_shape=out_shape,
        compiler_params=_cparams(2),
        name="proj",
    )(*args)
    if has_y:
        return outs
    return [x] + list(outs)


def _split_maps_t(qt):
    row = lax.broadcasted_iota(jnp.int32, qt.shape, 0)
    zero = jnp.zeros_like(qt)
    return jnp.concatenate([jnp.where(row < HEAD_DIM, qt, zero),
                            jnp.where(row >= HEAD_DIM, qt, zero)], axis=1)


def _ones_rows(n_cols):
    row = lax.broadcasted_iota(jnp.int32, (SUM_ROWS, n_cols), 0)
    return jnp.where(row == 0, 1.0, 0.0).astype(BF16)


def _vt_block(vt, n_keys):
    return jnp.concatenate([vt, _ones_rows(n_keys)], axis=0)


def _softmax_block_t(q2t, kb, vbt_ext, m_prev, acc_prev):
    st = jnp.dot(kb, q2t, preferred_element_type=F32)
    m_new = jnp.maximum(m_prev, jnp.max(st, axis=0, keepdims=True))
    alpha = jnp.exp2(m_prev - m_new)
    pt = jnp.exp2(st - m_new).astype(BF16)
    acc = alpha * acc_prev + jnp.dot(vbt_ext, pt, preferred_element_type=F32)
    return m_new, acc


def _diff_combine_t(acc, lam_ref, gs_ref, lam_init):
    tq = acc.shape[1] // 2
    lp = lam_ref[...]
    lam = (jnp.exp(jnp.sum(lp[0:1] * lp[1:2], axis=1, keepdims=True))
           - jnp.exp(jnp.sum(lp[2:3] * lp[3:4], axis=1, keepdims=True)) + lam_init)
    pv = acc[:V_DIM]
    l = acc[V_DIM:V_DIM + 1]
    ot = pv[:, :tq] / l[:, :tq] - lam * (pv[:, tq:] / l[:, tq:])
    ot = ot * lax.rsqrt(jnp.mean(ot * ot, axis=0, keepdims=True) + EPS) * gs_ref[...]
    return (ot * (1.0 - lam_init)).T.astype(BF16)


def _attn_kernel(lam_ref, gs_ref, q_ref, k_ref, v_ref, o_ref, vt_ref, q2t_ref, m_ref, acc_ref,
                 lag_ref, pt_ref, *, n_lat_tok, tk, lam_init):
    n_tok = n_lat_tok + CTX_LEN

    @pl.when(pl.program_id(2) == 0)
    def _():
        for j in range(n_tok // MXU_TILE):
            sl = slice(j * MXU_TILE, (j + 1) * MXU_TILE)
            vt_ref[:, sl] = v_ref[0, sl, :].astype(F32).T.astype(BF16)

    q2t_ref[...] = _split_maps_t(q_ref[0].astype(F32).T.astype(BF16))

    def reset(m0):
        m_ref[...] = jnp.full(m_ref.shape, m0, F32)
        acc_ref[...] = jnp.zeros(acc_ref.shape, F32)

    def exact_chunk(off):
        m_new, acc = _softmax_block_t(
            q2t_ref[...], k_ref[0, pl.ds(off, MXU_TILE), :],
            _vt_block(vt_ref[:, pl.ds(off, MXU_TILE)], MXU_TILE), m_ref[...], acc_ref[...])
        m_ref[...] = m_new
        acc_ref[...] = acc

    def lagged_block(off, n_keys):
        m_prev = m_ref[...]
        pmax = None
        for r in range(0, n_keys, QK_ROWS):
            rows = min(QK_ROWS, n_keys - r)
            st = jnp.dot(k_ref[0, pl.ds(off + r, rows), :], q2t_ref[...], preferred_element_type=F32)
            p = jnp.exp2(st - m_prev).astype(BF16)
            pt_ref[r:r + rows, :] = p
            for t in range(0, rows, BF16_ROWS):
                tile = p[t:t + BF16_ROWS, :]
                pmax = tile if pmax is None else jnp.maximum(pmax, tile)
        acc = acc_ref[...] + jnp.dot(_vt_block(vt_ref[:, pl.ds(off, n_keys)], n_keys), pt_ref[0:n_keys, :],
                                     preferred_element_type=F32)
        excess = jnp.log2(jnp.max(pmax.astype(F32), axis=0, keepdims=True))
        m_new = m_prev + jnp.maximum(excess, 0.0)
        acc_ref[...] = acc * jnp.exp2(m_prev - m_new)
        m_ref[...] = m_new
        lag_ref[...] = jnp.maximum(lag_ref[...], excess)

    reset(0.0)
    lag_ref[...] = jnp.full(lag_ref.shape, -jnp.inf, F32)
    n_full = n_tok // tk - 1
    lax.fori_loop(0, n_full, lambda j, carry: (lagged_block(pl.multiple_of(j * tk, tk), tk), carry)[1], 0)
    lagged_block(n_full * tk, n_tok - n_full * tk)

    sum_min = jnp.min(acc_ref[V_DIM:V_DIM + 1, :])

    @pl.when(jnp.logical_or(jnp.max(lag_ref[...]) > LAG_LIMIT, sum_min < SUM_FLOOR))
    def _():
        reset(-jnp.inf)
        lax.fori_loop(0, n_tok // MXU_TILE,
                      lambda c, carry: (exact_chunk(pl.multiple_of(c * MXU_TILE, MXU_TILE)), carry)[1], 0)

    o_ref[0] = _diff_combine_t(acc_ref[...], lam_ref, gs_ref, lam_init)


def _attn_ctx_kernel(lam_ref, gs_ref, q_ref, k_ref, v_ref, o_any, o_ref, *, lam_init):
    del o_any
    q2t = _split_maps_t(q_ref[0].astype(F32).T.astype(BF16))
    vbt = _vt_block(v_ref[0].astype(F32).T.astype(BF16), CTX_LEN)
    m0 = jnp.full((1, 2 * CTX_LEN), -jnp.inf, F32)
    acc0 = jnp.zeros((V_DIM + SUM_ROWS, 2 * CTX_LEN), F32)
    _, acc = _softmax_block_t(q2t, k_ref[0], vbt, m0, acc0)
    o_ref[0] = _diff_combine_t(acc, lam_ref, gs_ref, lam_init)


def _key_block(n_tok):
    for mult in range(MAX_KEY_TILES, 0, -1):
        if n_tok % (mult * MXU_TILE) == 0:
            return mult * MXU_TILE
    raise ValueError(n_tok)


def _attn_call(q, k, v, lam_qk, g_subln, lam_init):
    bsz, n_tok, _ = q.shape
    n_lat_tok = n_tok - CTX_LEN
    tk = _key_block(n_lat_tok)
    tq_step = TQ
    n_col = 2 * TQ
    small = [pl.BlockSpec((4, HEAD_DIM), lambda *_: (0, 0)),
             pl.BlockSpec((V_DIM, 1), lambda *_: (0, 0))]
    gs = g_subln.reshape(V_DIM, 1)
    o = pl.pallas_call(
        functools.partial(_attn_kernel, n_lat_tok=n_lat_tok, tk=tk, lam_init=lam_init),
        grid=(bsz, N_HEADS, n_lat_tok // tq_step),
        in_specs=small + [
            pl.BlockSpec((1, tq_step, LANES), lambda b, h, t: (b, t, h)),
            pl.BlockSpec((1, n_tok, LANES), lambda b, h, t: (b, 0, h)),
            pl.BlockSpec((1, n_tok, LANES), lambda b, h, t: (b, 0, h)),
        ],
        out_specs=pl.BlockSpec((1, tq_step, V_DIM), lambda b, h, t: (b, t, h)),
        out_shape=jax.ShapeDtypeStruct((bsz, n_tok, V_WIDTH), BF16),
        scratch_shapes=[
            pltpu.VMEM((V_DIM, n_tok), BF16),
            pltpu.VMEM((LANES, n_col), BF16),
            pltpu.VMEM((1, n_col), F32),
            pltpu.VMEM((V_DIM + SUM_ROWS, n_col), F32),
            pltpu.VMEM((1, n_col), F32),
            pltpu.VMEM((n_tok - (n_tok // tk - 1) * tk, n_col), BF16),
        ],
        compiler_params=_cparams(3),
        name="diff_attn",
    )(lam_qk, gs, q, k, v)
    ctx_blk = n_lat_tok // CTX_LEN
    ctx_spec = pl.BlockSpec((1, CTX_LEN, LANES), lambda b, h: (b, ctx_blk, h))
    return pl.pallas_call(
        functools.partial(_attn_ctx_kernel, lam_init=lam_init),
        grid=(bsz, N_HEADS),
        in_specs=small + [ctx_spec, ctx_spec, ctx_spec, pl.BlockSpec(memory_space=pl.ANY)],
        out_specs=ctx_spec,
        out_shape=jax.ShapeDtypeStruct((bsz, n_tok, V_WIDTH), BF16),
        input_output_aliases={5: 0},
        compiler_params=_cparams(2),
        name="diff_attn_ctx",
    )(lam_qk, gs, q, k, v, o)


def _dft1_kernel(u_ref, c1_ref, s1_ref, twc_ref, tws_ref, o_ref):
    n1 = u_ref.shape[1]
    rows = n1 * DFT_GROUP
    u = u_ref[0].reshape(rows, F_WIDTH).astype(BF16)
    ar = jnp.dot(c1_ref[...], u, preferred_element_type=F32)
    ai = -jnp.dot(s1_ref[...], u, preferred_element_type=F32)
    twc = twc_ref[0]
    tws = tws_ref[0]
    out = []
    for part in (lambda a_r, a_i: a_r * twc + a_i * tws, lambda a_r, a_i: a_i * twc - a_r * tws):
        for g in range(F_GROUPS):
            sl = slice(g * LANES, (g + 1) * LANES)
            out.append(part(ar[:, sl], ai[:, sl]))
    o_ref[0] = jnp.concatenate(out, axis=1).reshape(n1, DFT_GROUP, 2 * F_WIDTH)


def _channel_dft(gr, gi, cc_ref, sc_ref, norm):
    grb = gr.astype(BF16)
    gib = gi.astype(BF16)
    out = []
    for g in range(F_GROUPS):
        sl = slice(g * LANES, (g + 1) * LANES)
        out.append((jnp.dot(grb[:, sl], cc_ref[...], preferred_element_type=F32)
                    + jnp.dot(gib[:, sl], sc_ref[...], preferred_element_type=F32)) * norm)
    return out


def _dft2_kernel(b_ref, c2_ref, s2_ref, cc_ref, sc_ref, o_ref, *, norm):
    for j in range(DFT_GROUP):
        bb = b_ref[0, j].astype(BF16)
        cb = jnp.dot(c2_ref[...], bb, preferred_element_type=F32)
        sb = jnp.dot(s2_ref[...], bb, preferred_element_type=F32)
        gr = cb[:, :F_WIDTH] + sb[:, F_WIDTH:]
        gi = cb[:, F_WIDTH:] - sb[:, :F_WIDTH]
        for g, z in enumerate(_channel_dft(gr, gi, cc_ref, sc_ref, norm)):
            o_ref[0, :, j, g * LANES:(g + 1) * LANES] = z


def _dft_ctx_kernel(f_ref, c_ref, s_ref, cc_ref, sc_ref, z_any, o_ref, *, norm):
    del z_any
    f = f_ref[0].astype(BF16)
    gr = jnp.dot(c_ref[...], f, preferred_element_type=F32)
    gi = -jnp.dot(s_ref[...], f, preferred_element_type=F32)
    for g, z in enumerate(_channel_dft(gr, gi, cc_ref, sc_ref, norm)):
        o_ref[0, :, g * LANES:(g + 1) * LANES] = z


def _dft_mats(n, repeat=1):
    idx = np.arange(n, dtype=np.float64)
    ang = 2.0 * np.pi * np.outer(idx, idx) / n
    eye = np.eye(repeat)
    return (jnp.asarray(np.kron(np.cos(ang), eye), BF16), jnp.asarray(np.kron(np.sin(ang), eye), BF16))


def _fourier_call(f, n_lat_tok):
    bsz, n_tok, _ = f.shape
    n1 = n_lat_tok // DFT2
    full = lambda *shape: pl.BlockSpec(shape, lambda *_: (0,) * len(shape))
    c1, s1 = _dft_mats(n1, DFT_GROUP)
    c2, s2 = _dft_mats(DFT2)
    cc, sc = _dft_mats(F_GROUP_DIM)
    cctx, sctx = _dft_mats(CTX_LEN)
    tw_ang = (2.0 * np.pi / n_lat_tok) * np.einsum(
        "sj,k->skj", np.arange(DFT2, dtype=np.float64).reshape(DFT2 // DFT_GROUP, DFT_GROUP),
        np.arange(n1, dtype=np.float64)).reshape(DFT2 // DFT_GROUP, n1 * DFT_GROUP)
    twc = jnp.asarray(np.broadcast_to(np.cos(tw_ang)[:, :, None], tw_ang.shape + (LANES,)), F32)
    tws = jnp.asarray(np.broadcast_to(np.sin(tw_ang)[:, :, None], tw_ang.shape + (LANES,)), F32)

    fv = f.reshape(bsz, n_tok // DFT2, DFT2, F_WIDTH)
    mid = pl.pallas_call(
        _dft1_kernel,
        grid=(bsz, DFT2 // DFT_GROUP),
        in_specs=[
            pl.BlockSpec((1, n1, DFT_GROUP, F_WIDTH), lambda b, j: (b, 0, j, 0)),
            full(n1 * DFT_GROUP, n1 * DFT_GROUP), full(n1 * DFT_GROUP, n1 * DFT_GROUP),
            pl.BlockSpec((1, n1 * DFT_GROUP, LANES), lambda b, j: (j, 0, 0)),
            pl.BlockSpec((1, n1 * DFT_GROUP, LANES), lambda b, j: (j, 0, 0)),
        ],
        out_specs=pl.BlockSpec((1, n1, DFT_GROUP, 2 * F_WIDTH), lambda b, j: (b, 0, j, 0)),
        out_shape=jax.ShapeDtypeStruct((bsz, n1, DFT2, 2 * F_WIDTH), F32),
        compiler_params=_cparams(2),
        name="dft_outer",
    )(fv, c1, s1, twc, tws)

    z = pl.pallas_call(
        functools.partial(_dft2_kernel, norm=1.0 / math.sqrt(n_lat_tok * F_GROUP_DIM)),
        grid=(bsz, n1 // DFT_GROUP),
        in_specs=[
            pl.BlockSpec((1, DFT_GROUP, DFT2, 2 * F_WIDTH), lambda b, j: (b, j, 0, 0)),
            full(DFT2, DFT2), full(DFT2, DFT2),
            full(F_GROUP_DIM, F_GROUP_DIM), full(F_GROUP_DIM, F_GROUP_DIM),
        ],
        out_specs=pl.BlockSpec((1, DFT2, DFT_GROUP, F_WIDTH), lambda b, j: (b, 0, j, 0)),
        out_shape=jax.ShapeDtypeStruct((bsz, n_tok // n1, n1, F_WIDTH), F32),
        compiler_params=_cparams(2),
        name="dft_inner",
    )(mid, c2, s2, cc, sc)
    z = z.reshape(bsz, n_tok, F_WIDTH)

    ctx_blk = n_lat_tok // CTX_LEN
    z = pl.pallas_call(
        functools.partial(_dft_ctx_kernel, norm=1.0 / math.sqrt(CTX_LEN * F_GROUP_DIM)),
        grid=(bsz,),
        in_specs=[
            pl.BlockSpec((1, CTX_LEN, F_WIDTH), lambda b: (b, ctx_blk, 0)),
            full(CTX_LEN, CTX_LEN), full(CTX_LEN, CTX_LEN),
            full(F_GROUP_DIM, F_GROUP_DIM), full(F_GROUP_DIM, F_GROUP_DIM),
            pl.BlockSpec(memory_space=pl.ANY),
        ],
        out_specs=pl.BlockSpec((1, CTX_LEN, F_WIDTH), lambda b: (b, ctx_blk, 0)),
        out_shape=jax.ShapeDtypeStruct((bsz, n_tok, F_WIDTH), F32),
        input_output_aliases={5: 0},
        compiler_params=_cparams(1),
        name="dft_ctx",
    )(f, cctx, sctx, cc, sc, z)
    return z


def _merge_kernel(x_ref, o_ref, z_ref, mod_ref, g1_ref, g2_ref, wgate_ref, bgate_ref,
                  wao_ref, wfo_ref, wout_ref, wr_ref, br_ref, xo_ref, h2_ref, route_ref):
    x = x_ref[0]
    mod = mod_ref[0]
    hb = _rms_mod(x, g1_ref[...], mod[0:1], mod[1:2]).astype(BF16)
    gates = jax.nn.sigmoid(jnp.dot(hb, wgate_ref[...], preferred_element_type=F32) + bgate_ref[...])
    ya = jnp.dot(o_ref[0], wao_ref[...], preferred_element_type=F32)
    yf = jnp.dot(z_ref[0].astype(BF16), wfo_ref[...], preferred_element_type=F32)
    merged = gates[:, :D_MODEL] * ya + gates[:, D_MODEL:] * yf
    out = jnp.dot(merged.astype(BF16), wout_ref[...], preferred_element_type=F32)
    xn = x + mod[2:3] * out
    xo_ref[0] = xn
    h2 = _rms_mod(xn, g2_ref[...], mod[3:4], mod[4:5])
    h2_ref[0] = h2

    h2_hi = h2.astype(BF16)
    h2_lo = (h2 - h2_hi.astype(F32)).astype(BF16)
    part = jnp.dot(h2_hi, wr_ref[...], preferred_element_type=F32)
    logits = (part[:, :LANES] + part[:, LANES:]
              + jnp.dot(h2_lo, wr_ref[:, :LANES], preferred_element_type=F32))
    aff = jax.nn.sigmoid(logits)
    lane = lax.broadcasted_iota(jnp.int32, (TM, LANES), 1)
    lanef = lane.astype(F32)
    neg = jnp.full((TM, LANES), -jnp.inf, F32)
    sel = jnp.where(lane < N_EXPERTS, aff + br_ref[...], neg)
    grp = lane // EXPERTS_PER_GROUP
    best = i1b = i2b = None
    for g in range(N_GROUPS):
        sg = jnp.where(grp == g, sel, neg)
        m1 = jnp.max(sg, axis=1, keepdims=True)
        i1 = jnp.min(jnp.where(sg == m1, lanef, float(LANES)), axis=1, keepdims=True)
        sg2 = jnp.where(lanef == i1, neg, sg)
        m2 = jnp.max(sg2, axis=1, keepdims=True)
        i2 = jnp.min(jnp.where(sg2 == m2, lanef, float(LANES)), axis=1, keepdims=True)
        score = m1 + m2
        if g == 0:
            best, i1b, i2b = score, i1, i2
        else:
            upd = score > best
            best = jnp.where(upd, score, best)
            i1b = jnp.where(upd, i1, i1b)
            i2b = jnp.where(upd, i2, i2b)
    hit1 = lanef == i1b
    hit2 = lanef == i2b
    a1 = jnp.sum(jnp.where(hit1, aff, 0.0), axis=1, keepdims=True)
    a2 = jnp.sum(jnp.where(hit2, aff, 0.0), axis=1, keepdims=True)
    den = a1 + a2
    route_ref[0] = (jnp.where(lane == 0, i1b, 0.0) + jnp.where(lane == 1, i2b, 0.0)
                    + jnp.where(lane == 2, a1 / den, 0.0) + jnp.where(lane == 3, a2 / den, 0.0))


def _merge_call(x, o, z, mod, g1, g2, w_gate, b_gate, w_ao, w_fo, w_out, w_r, b_r, n_batch):
    bsz, n_tok, _ = x.shape
    nt = n_tok // TM
    tok = lambda w: pl.BlockSpec((1, TM, w), lambda b, t: (b, t, 0))
    full = lambda *shape: pl.BlockSpec(shape, lambda b, t: (0,) * len(shape))
    return pl.pallas_call(
        _merge_kernel,
        grid=(bsz, nt),
        in_specs=[
            tok(D_MODEL), tok(V_WIDTH), tok(F_WIDTH),
            pl.BlockSpec((1, N_MOD, D_MODEL),
                         lambda b, t: (jnp.where(t == nt - 1, n_batch, b), 0, 0)),
            full(1, D_MODEL), full(1, D_MODEL),
            full(D_MODEL, 2 * D_MODEL), full(1, 2 * D_MODEL),
            full(V_WIDTH, D_MODEL), full(F_WIDTH, D_MODEL), full(D_MODEL, D_MODEL),
            full(D_MODEL, 2 * LANES), full(1, LANES),
        ],
        out_specs=[tok(D_MODEL), tok(D_MODEL), tok(LANES)],
        out_shape=[
            jax.ShapeDtypeStruct((bsz, n_tok, D_MODEL), F32),
            jax.ShapeDtypeStruct((bsz, n_tok, D_MODEL), F32),
            jax.ShapeDtypeStruct((bsz, n_tok, LANES), F32),
        ],
        compiler_params=_cparams(2),
        name="merge_route",
    )(x, o, z, mod, g1, g2, w_gate, b_gate, w_ao, w_fo, w_out, w_r, b_r)


def _slot_copy(src_ref, src_row, dst_ref, dst_row, sem):
    return pltpu.make_async_copy(src_ref.at[pl.ds(src_row, 1)], dst_ref.at[pl.ds(dst_row, 1)], sem)


def _dispatch_kernel(pad_start_ref, pad_len_ref, pos_ref, h_ref, xs_ref, zero_ref, sem, pad_sem):
    @pl.when(pl.program_id(0) == 0)
    def _():
        zero_ref[...] = jnp.zeros(zero_ref.shape, F32)

        def pad_copy(e, r):
            return _slot_copy(zero_ref, 0, xs_ref, pad_start_ref[e] + r, pad_sem)

        for e in range(N_EXPERTS):
            lax.fori_loop(0, pad_len_ref[e], lambda r, carry, e=e: (pad_copy(e, r).start(), carry)[1], 0)
        for e in range(N_EXPERTS):
            lax.fori_loop(0, pad_len_ref[e], lambda r, carry, e=e: (pad_copy(e, r).wait(), carry)[1], 0)

    def copy(r, k):
        return _slot_copy(h_ref, r, xs_ref, pos_ref[0, 0, 2 * r + k], sem)

    def issue(r, carry):
        copy(r, 0).start(priority=0)
        copy(r, 1).start(priority=1)
        return carry

    def drain(r, carry):
        copy(r, 0).wait()
        copy(r, 1).wait()
        return carry

    lax.fori_loop(0, TSLOT, issue, 0, unroll=DMA_UNROLL)
    lax.fori_loop(0, TSLOT, drain, 0, unroll=DMA_UNROLL)


def _experts_kernel(te_ref, nv_ref, xs_ref, wg_ref, wu_ref, wd_ref, ys_ref):
    del te_ref
    t = pl.program_id(0)

    @pl.when(t < nv_ref[0])
    def _():
        h = xs_ref[...].astype(BF16)
        a = jnp.dot(h, wg_ref[0].astype(BF16), preferred_element_type=F32)
        u = jnp.dot(h, wu_ref[0].astype(BF16), preferred_element_type=F32)
        he = (a * jax.nn.sigmoid(a) * u).astype(BF16)
        ys_ref[...] = jnp.dot(he, wd_ref[0].astype(BF16), preferred_element_type=F32)

    @pl.when(t >= nv_ref[0])
    def _():
        ys_ref[...] = jnp.zeros(ys_ref.shape, F32)


def _combine_kernel(pos_ref, route_ref, ys_ref, y_ref, buf_ref, sem):
    def copy(r, k):
        return _slot_copy(ys_ref, pos_ref[0, 0, 2 * r + k], buf_ref.at[k], r, sem)

    def issue(r, carry):
        copy(r, 0).start(priority=0)
        copy(r, 1).start(priority=1)
        return carry

    def drain(r, carry):
        copy(r, 0).wait()
        copy(r, 1).wait()
        return carry

    lax.fori_loop(0, TSLOT, issue, 0, unroll=DMA_UNROLL)
    lax.fori_loop(0, TSLOT, drain, 0, unroll=DMA_UNROLL)
    route = route_ref[...]
    y_ref[...] = route[:, 2:3] * buf_ref[0] + route[:, 3:4] * buf_ref[1]


def _moe_call(h2, route, w_eg, w_eu, w_ed):
    n = h2.shape[0]
    n_slots = 2 * n
    n_rows = n_slots + N_EXPERTS * TMR
    n_tiles = n_rows // TMR

    e_slot = route[:, :2].astype(jnp.int32).reshape(n_slots)
    onehot = (e_slot[:, None] == jnp.arange(N_EXPERTS, dtype=jnp.int32)[None, :]).astype(jnp.int32)
    csum = jnp.cumsum(onehot, axis=0)
    rank = jnp.sum(csum * onehot, axis=1) - 1
    counts = csum[-1]
    padded = ((counts + TMR - 1) // TMR) * TMR
    ends = jnp.cumsum(padded)
    pos = jnp.sum(onehot * (ends - padded)[None, :], axis=1) + rank
    tile_expert = jnp.minimum(
        jnp.sum((jnp.arange(n_tiles, dtype=jnp.int32) * TMR)[:, None] >= ends[None, :], axis=1),
        N_EXPERTS - 1).astype(jnp.int32)
    n_valid = (ends[-1] // TMR).astype(jnp.int32).reshape(1)
    pos = jnp.clip(pos, 0, n_rows - 1)
    pos_tiles = pos.astype(jnp.int32).reshape(n // TSLOT, 1, 2 * TSLOT)

    pos_block = (1, 1, 2 * TSLOT)
    xs = pl.pallas_call(
        _dispatch_kernel,
        grid_spec=pltpu.PrefetchScalarGridSpec(
            num_scalar_prefetch=2,
            grid=(n // TSLOT,),
            in_specs=[pl.BlockSpec(pos_block, lambda t, ps, pn: (t, 0, 0), memory_space=pltpu.SMEM),
                      pl.BlockSpec((TSLOT, D_MODEL), lambda t, ps, pn: (t, 0))],
            out_specs=pl.BlockSpec(memory_space=pl.ANY),
            scratch_shapes=[pltpu.VMEM((SUBLANES, D_MODEL), F32),
                            pltpu.SemaphoreType.DMA, pltpu.SemaphoreType.DMA],
        ),
        out_shape=jax.ShapeDtypeStruct((n_rows, D_MODEL), F32),
        compiler_params=_cparams(1),
        name="dispatch",
    )((ends - padded + counts).astype(jnp.int32), (padded - counts).astype(jnp.int32), pos_tiles, h2)

    w_map = lambda t, te, nv: (te[t], 0, 0)
    ys = pl.pallas_call(
        _experts_kernel,
        grid_spec=pltpu.PrefetchScalarGridSpec(
            num_scalar_prefetch=2,
            grid=(n_tiles,),
            in_specs=[
                pl.BlockSpec((TMR, D_MODEL), lambda t, te, nv: (t, 0)),
                pl.BlockSpec((1, D_MODEL, D_EXPERT), w_map),
                pl.BlockSpec((1, D_MODEL, D_EXPERT), w_map),
                pl.BlockSpec((1, D_EXPERT, D_MODEL), w_map),
            ],
            out_specs=pl.BlockSpec((TMR, D_MODEL), lambda t, te, nv: (t, 0)),
        ),
        out_shape=jax.ShapeDtypeStruct((n_rows, D_MODEL), F32),
        compiler_params=_cparams(1),
        name="experts",
    )(tile_expert, n_valid, xs, w_eg, w_eu, w_ed)

    return pl.pallas_call(
        _combine_kernel,
        grid=(n // TSLOT,),
        in_specs=[pl.BlockSpec(pos_block, lambda t: (t, 0, 0), memory_space=pltpu.SMEM),
                  pl.BlockSpec((TSLOT, LANES), lambda t: (t, 0)),
                  pl.BlockSpec(memory_space=pl.ANY)],
        out_specs=pl.BlockSpec((TSLOT, D_MODEL), lambda t: (t, 0)),
        out_shape=jax.ShapeDtypeStruct((n, D_MODEL), F32),
        scratch_shapes=[pltpu.VMEM((2, TSLOT, D_MODEL), F32), pltpu.SemaphoreType.DMA],
        compiler_params=_cparams(1),
        name="combine",
    )(pos_tiles, route, ys)


def _final_kernel(x_ref, y_ref, mod_ref, g_ref, o_ref):
    x = x_ref[0] + mod_ref[0][5:6] * y_ref[0]
    o_ref[0] = x * lax.rsqrt(jnp.mean(x * x, axis=-1, keepdims=True) + EPS) * g_ref[...]


def _final_call(x, y, mod, g_final, n_lat_tok):
    bsz = x.shape[0]
    tok = pl.BlockSpec((1, TM, D_MODEL), lambda b, t: (b, t, 0))
    return pl.pallas_call(
        _final_kernel,
        grid=(bsz, n_lat_tok // TM),
        in_specs=[tok, tok,
                  pl.BlockSpec((1, N_MOD, D_MODEL), lambda b, t: (b, 0, 0)),
                  pl.BlockSpec((1, D_MODEL), lambda b, t: (0, 0))],
        out_specs=tok,
        out_shape=jax.ShapeDtypeStruct((bsz, n_lat_tok, D_MODEL), F32),
        compiler_params=_cparams(2),
        name="final_norm",
    )(x, y, mod, g_final)


def _rope_tables(n_lat_tok):
    t = np.arange(n_lat_tok)
    pos = np.stack([t // GRID_W, t % GRID_W], axis=0).astype(np.float64)
    inv = ROPE_THETA ** (-np.arange(ROPE_HALF, dtype=np.float64) / ROPE_HALF)
    d = np.arange(LANES) % HEAD_DIM
    axis = d // (2 * ROPE_HALF)
    freq = d % ROPE_HALF
    ang = pos[axis, :].T * inv[freq][None, :]
    sign = np.where((d % (2 * ROPE_HALF)) < ROPE_HALF, -1.0, 1.0)
    cos = np.concatenate([np.cos(ang), np.ones((CTX_LEN, LANES))], axis=0)
    sin = np.concatenate([np.sin(ang) * sign[None, :], np.zeros((CTX_LEN, LANES))], axis=0)
    return jnp.asarray(cos, F32), jnp.asarray(sin, F32)


def kernel(x, c, ctx, c_ctx, w_ada, b_ada, g_norm1, w_in, lam_qk, g_subln, w_attn_out, w_four_out,
           w_gate, b_gate, w_out, g_norm2, w_router, b_router, w_e_gate, w_e_up, w_e_down, g_final):
    bsz, n_lat_tok, d = x.shape
    depth = w_ada.shape[0]
    assert d == D_MODEL and ctx.shape[1] == CTX_LEN and bsz < MOD_ROWS
    assert n_lat_tok % TQ == 0 and n_lat_tok % (DFT2 * DFT_GROUP) == 0
    n_tok = n_lat_tok + CTX_LEN
    assert (bsz * n_tok) % TSLOT == 0

    cond = jnp.zeros((MOD_ROWS, D_MODEL), F32).at[:bsz].set(c).at[bsz].set(c_ctx)
    mods = _ada_call(cond, w_ada, b_ada).reshape(depth, MOD_ROWS, N_MOD, D_MODEL)
    cos_t, sin_t = _rope_tables(n_lat_tok)
    w_r = jnp.zeros((D_MODEL, LANES), F32).at[:, :N_EXPERTS].set(w_router)
    w_r_hi = w_r.astype(BF16)
    w_r = jnp.concatenate([w_r_hi, (w_r - w_r_hi.astype(F32)).astype(BF16)], axis=1)
    b_r = jnp.zeros((1, LANES), F32).at[0, :N_EXPERTS].set(b_router)

    xs = jnp.concatenate([x, ctx], axis=1)
    y = None
    for i in range(depth):
        lam_init = 0.8 - 0.6 * math.exp(-0.3 * i)
        xs, q, k, v, f = _proj_call(
            xs, y, mods[i - 1] if i else None, mods[i], g_norm1[i].reshape(1, D_MODEL),
            w_in[i].astype(BF16), cos_t, sin_t, bsz)
        o = _attn_call(q, k, v, lam_qk[i], g_subln[i], lam_init)
        z = _fourier_call(f, n_lat_tok)
        xs, h2, route = _merge_call(
            xs, o, z, mods[i], g_norm1[i].reshape(1, D_MODEL), g_norm2[i].reshape(1, D_MODEL),
            w_gate[i].astype(BF16), b_gate[i].reshape(1, 2 * D_MODEL), w_attn_out[i].astype(BF16),
            w_four_out[i].astype(BF16), w_out[i].astype(BF16), w_r, b_r, bsz)
        y = _moe_call(h2.reshape(bsz * n_tok, D_MODEL), route.reshape(bsz * n_tok, LANES),
                      w_e_gate[i], w_e_up[i], w_e_down[i])
        y = y.reshape(bsz, n_tok, D_MODEL)
    return _final_call(xs, y, mods[depth - 1], g_final.reshape(1, D_MODEL), n_lat_tok)
```

```python
import functools
import math

import numpy as np
import jax
import jax.numpy as jnp
from jax import lax
from jax.experimental import pallas as pl
from jax.experimental.pallas import tpu as pltpu

F32 = jnp.float32
BF16 = jnp.bfloat16
HIGHEST = lax.Precision.HIGHEST

D_MODEL = 1024
GRID_W = 64
CTX_LEN = 256
N_HEADS = 8
HEAD_DIM = 64
V_DIM = 128
QK_WIDTH = 1024
V_WIDTH = 1024
F_GROUPS = 4
F_WIDTH = 512
F_GROUP_DIM = 128
IN_WIDTH = 3584
ROPE_THETA = 10000.0
ROPE_HALF = 16
N_EXPERTS = 16
N_GROUPS = 4
EXPERTS_PER_GROUP = 4
D_EXPERT = 512
EPS = 1e-6
N_MOD = 6
MOD_ROWS = 8

LANES = 128
SUBLANES = 8
TM = 256
TQ = 1024
MXU_TILE = 256
BF16_ROWS = 16
SUM_ROWS = 16
MAX_KEY_TILES = 16
QK_ROWS = 512
LAG_LIMIT = 100.0
SUM_FLOOR = 2.0 ** -60
TMR = 256
TSLOT = 512
DMA_UNROLL = 8
ADA_TN = 512
DFT2 = 128
DFT_GROUP = 8
Q_SCALE = (HEAD_DIM ** -0.5) * math.log2(math.e)
VMEM_LIMIT = 56 * 1024 * 1024


def _cparams(n_axes):
    return pltpu.CompilerParams(dimension_semantics=("arbitrary",) * n_axes,
                                vmem_limit_bytes=VMEM_LIMIT)


def _rms_mod(x, g, shift, scale):
    y = x * lax.rsqrt(jnp.mean(x * x, axis=-1, keepdims=True) + EPS) * g
    return y * (1.0 + scale) + shift


def _ada_kernel(c_ref, w_ref, b_ref, o_ref):
    c = c_ref[...]
    s = c * jax.nn.sigmoid(c)
    o_ref[0] = jnp.dot(s, w_ref[0], preferred_element_type=F32, precision=HIGHEST) + b_ref[0]


def _ada_call(cond, w_ada, b_ada):
    depth = w_ada.shape[0]
    n_out = w_ada.shape[2]
    return pl.pallas_call(
        _ada_kernel,
        grid=(depth, n_out // ADA_TN),
        in_specs=[
            pl.BlockSpec((MOD_ROWS, D_MODEL), lambda l, j: (0, 0)),
            pl.BlockSpec((1, D_MODEL, ADA_TN), lambda l, j: (l, 0, j)),
            pl.BlockSpec((1, 1, ADA_TN), lambda l, j: (l, 0, j)),
        ],
        out_specs=pl.BlockSpec((1, MOD_ROWS, ADA_TN), lambda l, j: (l, 0, j)),
        out_shape=jax.ShapeDtypeStruct((depth, MOD_ROWS, n_out), F32),
        compiler_params=_cparams(2),
        name="ada",
    )(cond, w_ada, b_ada.reshape(depth, 1, n_out))


def _proj_kernel(*refs, has_y):
    if has_y:
        (x_ref, y_ref, pmod_ref, mod_ref, g1_ref, win_ref, cos_ref, sin_ref,
         xo_ref, q_ref, k_ref, v_ref, f_ref) = refs
        x = x_ref[0] + pmod_ref[0][5:6] * y_ref[0]
        xo_ref[0] = x
    else:
        (x_ref, mod_ref, g1_ref, win_ref, cos_ref, sin_ref,
         q_ref, k_ref, v_ref, f_ref) = refs
        x = x_ref[0]
    mod = mod_ref[0]
    hb = _rms_mod(x, g1_ref[...], mod[0:1], mod[1:2]).astype(BF16)
    cos = cos_ref[...]
    sin = sin_ref[...]
    lane = lax.broadcasted_iota(jnp.int32, (TM, LANES), 1)
    first = (lane % (2 * ROPE_HALF)) < ROPE_HALF

    def rope(xc):
        up = pltpu.roll(xc, LANES - ROPE_HALF, 1)
        dn = pltpu.roll(xc, ROPE_HALF, 1)
        return xc * cos + jnp.where(first, up, dn) * sin

    w2 = 2 * LANES
    for c in range(QK_WIDTH // w2):
        uq = jnp.dot(hb, win_ref[:, c * w2:(c + 1) * w2], preferred_element_type=F32)
        uk = jnp.dot(hb, win_ref[:, QK_WIDTH + c * w2:QK_WIDTH + (c + 1) * w2],
                     preferred_element_type=F32)
        for j in range(2):
            lo = c * w2 + j * LANES
            q_ref[0, :, lo:lo + LANES] = (rope(uq[:, j * LANES:(j + 1) * LANES]) * Q_SCALE).astype(BF16)
            k_ref[0, :, lo:lo + LANES] = rope(uk[:, j * LANES:(j + 1) * LANES]).astype(BF16)
    for c in range(V_WIDTH // w2):
        lo = 2 * QK_WIDTH + c * w2
        v_ref[0, :, c * w2:(c + 1) * w2] = jnp.dot(
            hb, win_ref[:, lo:lo + w2], preferred_element_type=F32).astype(BF16)
    for c in range(F_WIDTH // w2):
        lo = 2 * QK_WIDTH + V_WIDTH + c * w2
        f_ref[0, :, c * w2:(c + 1) * w2] = jnp.dot(
            hb, win_ref[:, lo:lo + w2], preferred_element_type=F32)


def _proj_call(x, y, pmod, mod, g1, w_in, cos_t, sin_t, n_batch):
    bsz, n_tok, _ = x.shape
    nt = n_tok // TM
    has_y = y is not None

    def mod_map(b, t):
        return (jnp.where(t == nt - 1, n_batch, b), 0, 0)

    tok = lambda w: pl.BlockSpec((1, TM, w), lambda b, t: (b, t, 0))
    modspec = pl.BlockSpec((1, N_MOD, D_MODEL), mod_map)
    in_specs = [tok(D_MODEL)]
    args = [x]
    if has_y:
        in_specs += [tok(D_MODEL), modspec]
        args += [y, pmod]
    in_specs += [
        modspec,
        pl.BlockSpec((1, D_MODEL), lambda b, t: (0, 0)),
        pl.BlockSpec((D_MODEL, IN_WIDTH), lambda b, t: (0, 0)),
        pl.BlockSpec((TM, LANES), lambda b, t: (t, 0)),
        pl.BlockSpec((TM, LANES), lambda b, t: (t, 0)),
    ]
    args += [mod, g1, w_in, cos_t, sin_t]
    out_specs = [tok(QK_WIDTH), tok(QK_WIDTH), tok(V_WIDTH), tok(F_WIDTH)]
    out_shape = [
        jax.ShapeDtypeStruct((bsz, n_tok, QK_WIDTH), BF16),
        jax.ShapeDtypeStruct((bsz, n_tok, QK_WIDTH), BF16),
        jax.ShapeDtypeStruct((bsz, n_tok, V_WIDTH), BF16),
        jax.ShapeDtypeStruct((bsz, n_tok, F_WIDTH), F32),
    ]
    if has_y:
        out_specs = [tok(D_MODEL)] + out_specs
        out_shape = [jax.ShapeDtypeStruct((bsz, n_tok, D_MODEL), F32)] + out_shape
    outs = pl.pallas_call(
        functools.partial(_proj_kernel, has_y=has_y),
        grid=(bsz, nt),
        in_specs=in_specs,
        out_specs=out_specs,
        out_shape=out_shape,
        compiler_params=_cparams(2),
        name="proj",
    )(*args)
    if has_y:
        return outs
    return [x] + list(outs)


def _split_maps_t(qt):
    row = lax.broadcasted_iota(jnp.int32, qt.shape, 0)
    zero = jnp.zeros_like(qt)
    return jnp.concatenate([jnp.where(row < HEAD_DIM, qt, zero),
                            jnp.where(row >= HEAD_DIM, qt, zero)], axis=1)


def _ones_rows(n_cols):
    row = lax.broadcasted_iota(jnp.int32, (SUM_ROWS, n_cols), 0)
    return jnp.where(row == 0, 1.0, 0.0).astype(BF16)


def _vt_block(vt, n_keys):
    return jnp.concatenate([vt, _ones_rows(n_keys)], axis=0)


def _softmax_block_t(q2t, kb, vbt_ext, m_prev, acc_prev):
    st = jnp.dot(kb, q2t, preferred_element_type=F32)
    m_new = jnp.maximum(m_prev, jnp.max(st, axis=0, keepdims=True))
    alpha = jnp.exp2(m_prev - m_new)
    pt = jnp.exp2(st - m_new).astype(BF16)
    acc = alpha * acc_prev + jnp.dot(vbt_ext, pt, preferred_element_type=F32)
    return m_new, acc


def _diff_combine_t(acc, lam_ref, gs_ref, lam_init):
    tq = acc.shape[1] // 2
    lp = lam_ref[...]
    lam = (jnp.exp(jnp.sum(lp[0:1] * lp[1:2], axis=1, keepdims=True))
           - jnp.exp(jnp.sum(lp[2:3] * lp[3:4], axis=1, keepdims=True)) + lam_init)
    pv = acc[:V_DIM]
    l = acc[V_DIM:V_DIM + 1]
    ot = pv[:, :tq] / l[:, :tq] - lam * (pv[:, tq:] / l[:, tq:])
    ot = ot * lax.rsqrt(jnp.mean(ot * ot, axis=0, keepdims=True) + EPS) * gs_ref[...]
    return (ot * (1.0 - lam_init)).T.astype(BF16)


def _attn_kernel(lam_ref, gs_ref, q_ref, k_ref, v_ref, o_ref, vt_ref, q2t_ref, m_ref, acc_ref,
                 lag_ref, pt_ref, *, n_lat_tok, tk, lam_init):
    n_tok = n_lat_tok + CTX_LEN

    @pl.when(pl.program_id(2) == 0)
    def _():
        for j in range(n_tok // MXU_TILE):
            sl = slice(j * MXU_TILE, (j + 1) * MXU_TILE)
            vt_ref[:, sl] = v_ref[0, sl, :].astype(F32).T.astype(BF16)

    q2t_ref[...] = _split_maps_t(q_ref[0].astype(F32).T.astype(BF16))

    def reset(m0):
        m_ref[...] = jnp.full(m_ref.shape, m0, F32)
        acc_ref[...] = jnp.zeros(acc_ref.shape, F32)

    def exact_chunk(off):
        m_new, acc = _softmax_block_t(
            q2t_ref[...], k_ref[0, pl.ds(off, MXU_TILE), :],
            _vt_block(vt_ref[:, pl.ds(off, MXU_TILE)], MXU_TILE), m_ref[...], acc_ref[...])
        m_ref[...] = m_new
        acc_ref[...] = acc

    def lagged_block(off, n_keys):
        m_prev = m_ref[...]
        pmax = None
        for r in range(0, n_keys, QK_ROWS):
            rows = min(QK_ROWS, n_keys - r)
            st = jnp.dot(k_ref[0, pl.ds(off + r, rows), :], q2t_ref[...], preferred_element_type=F32)
            p = jnp.exp2(st - m_prev).astype(BF16)
            pt_ref[r:r + rows, :] = p
            for t in range(0, rows, BF16_ROWS):
                tile = p[t:t + BF16_ROWS, :]
                pmax = tile if pmax is None else jnp.maximum(pmax, tile)
        acc = acc_ref[...] + jnp.dot(_vt_block(vt_ref[:, pl.ds(off, n_keys)], n_keys), pt_ref[0:n_keys, :],
                                     preferred_element_type=F32)
        excess = jnp.log2(jnp.max(pmax.astype(F32), axis=0, keepdims=True))
        m_new = m_prev + jnp.maximum(excess, 0.0)
        acc_ref[...] = acc * jnp.exp2(m_prev - m_new)
        m_ref[...] = m_new
        lag_ref[...] = jnp.maximum(lag_ref[...], excess)

    reset(0.0)
    lag_ref[...] = jnp.full(lag_ref.shape, -jnp.inf, F32)
    n_full = n_lat_tok // tk
    lax.fori_loop(0, n_full, lambda j, carry: (lagged_block(pl.multiple_of(j * tk, tk), tk), carry)[1], 0)
    lagged_block(n_full * tk, n_tok - n_full * tk)

    sum_min = jnp.min(acc_ref[V_DIM:V_DIM + 1, :])

    @pl.when(jnp.logical_or(jnp.max(lag_ref[...]) > LAG_LIMIT, sum_min < SUM_FLOOR))
    def _():
        reset(-jnp.inf)
        lax.fori_loop(0, n_tok // MXU_TILE,
                      lambda c, carry: (exact_chunk(pl.multiple_of(c * MXU_TILE, MXU_TILE)), carry)[1], 0)

    o_ref[0] = _diff_combine_t(acc_ref[...], lam_ref, gs_ref, lam_init)


def _attn_ctx_kernel(lam_ref, gs_ref, q_ref, k_ref, v_ref, o_any, o_ref, *, lam_init):
    del o_any
    q2t = _split_maps_t(q_ref[0].astype(F32).T.astype(BF16))
    vbt = _vt_block(v_ref[0].astype(F32).T.astype(BF16), CTX_LEN)
    m0 = jnp.full((1, 2 * CTX_LEN), -jnp.inf, F32)
    acc0 = jnp.zeros((V_DIM + SUM_ROWS, 2 * CTX_LEN), F32)
    _, acc = _softmax_block_t(q2t, k_ref[0], vbt, m0, acc0)
    o_ref[0] = _diff_combine_t(acc, lam_ref, gs_ref, lam_init)


def _key_block(n_tok):
    for mult in range(MAX_KEY_TILES, 0, -1):
        if n_tok % (mult * MXU_TILE) == 0:
            return mult * MXU_TILE
    raise ValueError(n_tok)


def _attn_call(q, k, v, lam_qk, g_subln, lam_init):
    bsz, n_tok, _ = q.shape
    n_lat_tok = n_tok - CTX_LEN
    tk = _key_block(n_lat_tok)
    tq_step = TQ
    n_col = 2 * TQ
    small = [pl.BlockSpec((4, HEAD_DIM), lambda *_: (0, 0)),
             pl.BlockSpec((V_DIM, 1), lambda *_: (0, 0))]
    gs = g_subln.reshape(V_DIM, 1)
    o = pl.pallas_call(
        functools.partial(_attn_kernel, n_lat_tok=n_lat_tok, tk=tk, lam_init=lam_init),
        grid=(bsz, N_HEADS, n_lat_tok // tq_step),
        in_specs=small + [
            pl.BlockSpec((1, tq_step, LANES), lambda b, h, t: (b, t, h)),
            pl.BlockSpec((1, n_tok, LANES), lambda b, h, t: (b, 0, h)),
            pl.BlockSpec((1, n_tok, LANES), lambda b, h, t: (b, 0, h)),
        ],
        out_specs=pl.BlockSpec((1, tq_step, V_DIM), lambda b, h, t: (b, t, h)),
        out_shape=jax.ShapeDtypeStruct((bsz, n_tok, V_WIDTH), BF16),
        scratch_shapes=[
            pltpu.VMEM((V_DIM, n_tok), BF16),
            pltpu.VMEM((LANES, n_col), BF16),
            pltpu.VMEM((1, n_col), F32),
            pltpu.VMEM((V_DIM + SUM_ROWS, n_col), F32),
            pltpu.VMEM((1, n_col), F32),
            pltpu.VMEM((n_tok - (n_tok // tk - 1) * tk, n_col), BF16),
        ],
        compiler_params=_cparams(3),
        name="diff_attn",
    )(lam_qk, gs, q, k, v)
    ctx_blk = n_lat_tok // CTX_LEN
    ctx_spec = pl.BlockSpec((1, CTX_LEN, LANES), lambda b, h: (b, ctx_blk, h))
    return pl.pallas_call(
        functools.partial(_attn_ctx_kernel, lam_init=lam_init),
        grid=(bsz, N_HEADS),
        in_specs=small + [ctx_spec, ctx_spec, ctx_spec, pl.BlockSpec(memory_space=pl.ANY)],
        out_specs=ctx_spec,
        out_shape=jax.ShapeDtypeStruct((bsz, n_tok, V_WIDTH), BF16),
        input_output_aliases={5: 0},
        compiler_params=_cparams(2),
        name="diff_attn_ctx",
    )(lam_qk, gs, q, k, v, o)


def _dft1_kernel(u_ref, c1_ref, s1_ref, twc_ref, tws_ref, o_ref):
    n1 = u_ref.shape[1]
    rows = n1 * DFT_GROUP
    u = u_ref[0].reshape(rows, F_WIDTH).astype(BF16)
    ar = jnp.dot(c1_ref[...], u, preferred_element_type=F32)
    ai = -jnp.dot(s1_ref[...], u, preferred_element_type=F32)
    twc = twc_ref[0]
    tws = tws_ref[0]
    out = []
    for part in (lambda a_r, a_i: a_r * twc + a_i * tws, lambda a_r, a_i: a_i * twc - a_r * tws):
        for g in range(F_GROUPS):
            sl = slice(g * LANES, (g + 1) * LANES)
            out.append(part(ar[:, sl], ai[:, sl]))
    o_ref[0] = jnp.concatenate(out, axis=1).reshape(n1, DFT_GROUP, 2 * F_WIDTH)


def _channel_dft(gr, gi, cc_ref, sc_ref, norm):
    grb = gr.astype(BF16)
    gib = gi.astype(BF16)
    out = []
    for g in range(F_GROUPS):
        sl = slice(g * LANES, (g + 1) * LANES)
        out.append((jnp.dot(grb[:, sl], cc_ref[...], preferred_element_type=F32)
                    + jnp.dot(gib[:, sl], sc_ref[...], preferred_element_type=F32)) * norm)
    return out


def _dft2_kernel(b_ref, c2_ref, s2_ref, cc_ref, sc_ref, o_ref, *, norm):
    for j in range(DFT_GROUP):
        bb = b_ref[0, j].astype(BF16)
        cb = jnp.dot(c2_ref[...], bb, preferred_element_type=F32)
        sb = jnp.dot(s2_ref[...], bb, preferred_element_type=F32)
        gr = cb[:, :F_WIDTH] + sb[:, F_WIDTH:]
        gi = cb[:, F_WIDTH:] - sb[:, :F_WIDTH]
        for g, z in enumerate(_channel_dft(gr, gi, cc_ref, sc_ref, norm)):
            o_ref[0, :, j, g * LANES:(g + 1) * LANES] = z


def _dft_ctx_kernel(f_ref, c_ref, s_ref, cc_ref, sc_ref, z_any, o_ref, *, norm):
    del z_any
    f = f_ref[0].astype(BF16)
    gr = jnp.dot(c_ref[...], f, preferred_element_type=F32)
    gi = -jnp.dot(s_ref[...], f, preferred_element_type=F32)
    for g, z in enumerate(_channel_dft(gr, gi, cc_ref, sc_ref, norm)):
        o_ref[0, :, g * LANES:(g + 1) * LANES] = z


def _dft_mats(n, repeat=1):
    idx = np.arange(n, dtype=np.float64)
    ang = 2.0 * np.pi * np.outer(idx, idx) / n
    eye = np.eye(repeat)
    return (jnp.asarray(np.kron(np.cos(ang), eye), BF16), jnp.asarray(np.kron(np.sin(ang), eye), BF16))


def _fourier_call(f, n_lat_tok):
    bsz, n_tok, _ = f.shape
    n1 = n_lat_tok // DFT2
    full = lambda *shape: pl.BlockSpec(shape, lambda *_: (0,) * len(shape))
    c1, s1 = _dft_mats(n1, DFT_GROUP)
    c2, s2 = _dft_mats(DFT2)
    cc, sc = _dft_mats(F_GROUP_DIM)
    cctx, sctx = _dft_mats(CTX_LEN)
    tw_ang = (2.0 * np.pi / n_lat_tok) * np.einsum(
        "sj,k->skj", np.arange(DFT2, dtype=np.float64).reshape(DFT2 // DFT_GROUP, DFT_GROUP),
        np.arange(n1, dtype=np.float64)).reshape(DFT2 // DFT_GROUP, n1 * DFT_GROUP)
    twc = jnp.asarray(np.broadcast_to(np.cos(tw_ang)[:, :, None], tw_ang.shape + (LANES,)), F32)
    tws = jnp.asarray(np.broadcast_to(np.sin(tw_ang)[:, :, None], tw_ang.shape + (LANES,)), F32)

    fv = f.reshape(bsz, n_tok // DFT2, DFT2, F_WIDTH)
    mid = pl.pallas_call(
        _dft1_kernel,
        grid=(bsz, DFT2 // DFT_GROUP),
        in_specs=[
            pl.BlockSpec((1, n1, DFT_GROUP, F_WIDTH), lambda b, j: (b, 0, j, 0)),
            full(n1 * DFT_GROUP, n1 * DFT_GROUP), full(n1 * DFT_GROUP, n1 * DFT_GROUP),
            pl.BlockSpec((1, n1 * DFT_GROUP, LANES), lambda b, j: (j, 0, 0)),
            pl.BlockSpec((1, n1 * DFT_GROUP, LANES), lambda b, j: (j, 0, 0)),
        ],
        out_specs=pl.BlockSpec((1, n1, DFT_GROUP, 2 * F_WIDTH), lambda b, j: (b, 0, j, 0)),
        out_shape=jax.ShapeDtypeStruct((bsz, n1, DFT2, 2 * F_WIDTH), F32),
        compiler_params=_cparams(2),
        name="dft_outer",
    )(fv, c1, s1, twc, tws)

    z = pl.pallas_call(
        functools.partial(_dft2_kernel, norm=1.0 / math.sqrt(n_lat_tok * F_GROUP_DIM)),
        grid=(bsz, n1 // DFT_GROUP),
        in_specs=[
            pl.BlockSpec((1, DFT_GROUP, DFT2, 2 * F_WIDTH), lambda b, j: (b, j, 0, 0)),
            full(DFT2, DFT2), full(DFT2, DFT2),
            full(F_GROUP_DIM, F_GROUP_DIM), full(F_GROUP_DIM, F_GROUP_DIM),
        ],
        out_specs=pl.BlockSpec((1, DFT2, DFT_GROUP, F_WIDTH), lambda b, j: (b, 0, j, 0)),
        out_shape=jax.ShapeDtypeStruct((bsz, n_tok // n1, n1, F_WIDTH), F32),
        compiler_params=_cparams(2),
        name="dft_inner",
    )(mid, c2, s2, cc, sc)
    z = z.reshape(bsz, n_tok, F_WIDTH)

    ctx_blk = n_lat_tok // CTX_LEN
    z = pl.pallas_call(
        functools.partial(_dft_ctx_kernel, norm=1.0 / math.sqrt(CTX_LEN * F_GROUP_DIM)),
        grid=(bsz,),
        in_specs=[
            pl.BlockSpec((1, CTX_LEN, F_WIDTH), lambda b: (b, ctx_blk, 0)),
            full(CTX_LEN, CTX_LEN), full(CTX_LEN, CTX_LEN),
            full(F_GROUP_DIM, F_GROUP_DIM), full(F_GROUP_DIM, F_GROUP_DIM),
            pl.BlockSpec(memory_space=pl.ANY),
        ],
        out_specs=pl.BlockSpec((1, CTX_LEN, F_WIDTH), lambda b: (b, ctx_blk, 0)),
        out_shape=jax.ShapeDtypeStruct((bsz, n_tok, F_WIDTH), F32),
        input_output_aliases={5: 0},
        compiler_params=_cparams(1),
        name="dft_ctx",
    )(f, cctx, sctx, cc, sc, z)
    return z


def _merge_kernel(x_ref, o_ref, z_ref, mod_ref, g1_ref, g2_ref, wgate_ref, bgate_ref,
                  wao_ref, wfo_ref, wout_ref, wr_ref, br_ref, xo_ref, h2_ref, route_ref):
    x = x_ref[0]
    mod = mod_ref[0]
    hb = _rms_mod(x, g1_ref[...], mod[0:1], mod[1:2]).astype(BF16)
    gates = jax.nn.sigmoid(jnp.dot(hb, wgate_ref[...], preferred_element_type=F32) + bgate_ref[...])
    ya = jnp.dot(o_ref[0], wao_ref[...], preferred_element_type=F32)
    yf = jnp.dot(z_ref[0].astype(BF16), wfo_ref[...], preferred_element_type=F32)
    merged = gates[:, :D_MODEL] * ya + gates[:, D_MODEL:] * yf
    out = jnp.dot(merged.astype(BF16), wout_ref[...], preferred_element_type=F32)
    xn = x + mod[2:3] * out
    xo_ref[0] = xn
    h2 = _rms_mod(xn, g2_ref[...], mod[3:4], mod[4:5])
    h2_ref[0] = h2

    h2_hi = h2.astype(BF16)
    h2_lo = (h2 - h2_hi.astype(F32)).astype(BF16)
    part = jnp.dot(h2_hi, wr_ref[...], preferred_element_type=F32)
    logits = (part[:, :LANES] + part[:, LANES:]
              + jnp.dot(h2_lo, wr_ref[:, :LANES], preferred_element_type=F32))
    aff = jax.nn.sigmoid(logits)
    lane = lax.broadcasted_iota(jnp.int32, (TM, LANES), 1)
    lanef = lane.astype(F32)
    neg = jnp.full((TM, LANES), -jnp.inf, F32)
    sel = jnp.where(lane < N_EXPERTS, aff + br_ref[...], neg)
    grp = lane // EXPERTS_PER_GROUP
    best = i1b = i2b = None
    for g in range(N_GROUPS):
        sg = jnp.where(grp == g, sel, neg)
        m1 = jnp.max(sg, axis=1, keepdims=True)
        i1 = jnp.min(jnp.where(sg == m1, lanef, float(LANES)), axis=1, keepdims=True)
        sg2 = jnp.where(lanef == i1, neg, sg)
        m2 = jnp.max(sg2, axis=1, keepdims=True)
        i2 = jnp.min(jnp.where(sg2 == m2, lanef, float(LANES)), axis=1, keepdims=True)
        score = m1 + m2
        if g == 0:
            best, i1b, i2b = score, i1, i2
        else:
            upd = score > best
            best = jnp.where(upd, score, best)
            i1b = jnp.where(upd, i1, i1b)
            i2b = jnp.where(upd, i2, i2b)
    hit1 = lanef == i1b
    hit2 = lanef == i2b
    a1 = jnp.sum(jnp.where(hit1, aff, 0.0), axis=1, keepdims=True)
    a2 = jnp.sum(jnp.where(hit2, aff, 0.0), axis=1, keepdims=True)
    den = a1 + a2
    route_ref[0] = (jnp.where(lane == 0, i1b, 0.0) + jnp.where(lane == 1, i2b, 0.0)
                    + jnp.where(lane == 2, a1 / den, 0.0) + jnp.where(lane == 3, a2 / den, 0.0))


def _merge_call(x, o, z, mod, g1, g2, w_gate, b_gate, w_ao, w_fo, w_out, w_r, b_r, n_batch):
    bsz, n_tok, _ = x.shape
    nt = n_tok // TM
    tok = lambda w: pl.BlockSpec((1, TM, w), lambda b, t: (b, t, 0))
    full = lambda *shape: pl.BlockSpec(shape, lambda b, t: (0,) * len(shape))
    return pl.pallas_call(
        _merge_kernel,
        grid=(bsz, nt),
        in_specs=[
            tok(D_MODEL), tok(V_WIDTH), tok(F_WIDTH),
            pl.BlockSpec((1, N_MOD, D_MODEL),
                         lambda b, t: (jnp.where(t == nt - 1, n_batch, b), 0, 0)),
            full(1, D_MODEL), full(1, D_MODEL),
            full(D_MODEL, 2 * D_MODEL), full(1, 2 * D_MODEL),
            full(V_WIDTH, D_MODEL), full(F_WIDTH, D_MODEL), full(D_MODEL, D_MODEL),
            full(D_MODEL, 2 * LANES), full(1, LANES),
        ],
        out_specs=[tok(D_MODEL), tok(D_MODEL), tok(LANES)],
        out_shape=[
            jax.ShapeDtypeStruct((bsz, n_tok, D_MODEL), F32),
            jax.ShapeDtypeStruct((bsz, n_tok, D_MODEL), F32),
            jax.ShapeDtypeStruct((bsz, n_tok, LANES), F32),
        ],
        compiler_params=_cparams(2),
        name="merge_route",
    )(x, o, z, mod, g1, g2, w_gate, b_gate, w_ao, w_fo, w_out, w_r, b_r)


def _slot_copy(src_ref, src_row, dst_ref, dst_row, sem):
    return pltpu.make_async_copy(src_ref.at[pl.ds(src_row, 1)], dst_ref.at[pl.ds(dst_row, 1)], sem)


def _dispatch_kernel(pad_start_ref, pad_len_ref, pos_ref, h_ref, xs_ref, zero_ref, sem, pad_sem):
    @pl.when(pl.program_id(0) == 0)
    def _():
        zero_ref[...] = jnp.zeros(zero_ref.shape, F32)

        def pad_copy(e, r):
            return _slot_copy(zero_ref, 0, xs_ref, pad_start_ref[e] + r, pad_sem)

        for e in range(N_EXPERTS):
            lax.fori_loop(0, pad_len_ref[e], lambda r, carry, e=e: (pad_copy(e, r).start(), carry)[1], 0)
        for e in range(N_EXPERTS):
            lax.fori_loop(0, pad_len_ref[e], lambda r, carry, e=e: (pad_copy(e, r).wait(), carry)[1], 0)

    def copy(r, k):
        return _slot_copy(h_ref, r, xs_ref, pos_ref[0, 0, 2 * r + k], sem)

    def issue(r, carry):
        copy(r, 0).start(priority=0)
        copy(r, 1).start(priority=1)
        return carry

    def drain(r, carry):
        copy(r, 0).wait()
        copy(r, 1).wait()
        return carry

    lax.fori_loop(0, TSLOT, issue, 0, unroll=DMA_UNROLL)
    lax.fori_loop(0, TSLOT, drain, 0, unroll=DMA_UNROLL)


def _experts_kernel(te_ref, nv_ref, xs_ref, wg_ref, wu_ref, wd_ref, ys_ref):
    del te_ref
    t = pl.program_id(0)

    @pl.when(t < nv_ref[0])
    def _():
        h = xs_ref[...].astype(BF16)
        a = jnp.dot(h, wg_ref[0].astype(BF16), preferred_element_type=F32)
        u = jnp.dot(h, wu_ref[0].astype(BF16), preferred_element_type=F32)
        he = (a * jax.nn.sigmoid(a) * u).astype(BF16)
        ys_ref[...] = jnp.dot(he, wd_ref[0].astype(BF16), preferred_element_type=F32)

    @pl.when(t >= nv_ref[0])
    def _():
        ys_ref[...] = jnp.zeros(ys_ref.shape, F32)


def _combine_kernel(pos_ref, route_ref, ys_ref, y_ref, buf_ref, sem):
    def copy(r, k):
        return _slot_copy(ys_ref, pos_ref[0, 0, 2 * r + k], buf_ref.at[k], r, sem)

    def issue(r, carry):
        copy(r, 0).start(priority=0)
        copy(r, 1).start(priority=1)
        return carry

    def drain(r, carry):
        copy(r, 0).wait()
        copy(r, 1).wait()
        return carry

    lax.fori_loop(0, TSLOT, issue, 0, unroll=DMA_UNROLL)
    lax.fori_loop(0, TSLOT, drain, 0, unroll=DMA_UNROLL)
    route = route_ref[...]
    y_ref[...] = route[:, 2:3] * buf_ref[0] + route[:, 3:4] * buf_ref[1]


def _moe_call(h2, route, w_eg, w_eu, w_ed):
    n = h2.shape[0]
    n_slots = 2 * n
    n_rows = n_slots + N_EXPERTS * TMR
    n_tiles = n_rows // TMR

    e_slot = route[:, :2].astype(jnp.int32).reshape(n_slots)
    onehot = (e_slot[:, None] == jnp.arange(N_EXPERTS, dtype=jnp.int32)[None, :]).astype(jnp.int32)
    csum = jnp.cumsum(onehot, axis=0)
    rank = jnp.sum(csum * onehot, axis=1) - 1
    counts = csum[-1]
    padded = ((counts + TMR - 1) // TMR) * TMR
    ends = jnp.cumsum(padded)
    pos = jnp.sum(onehot * (ends - padded)[None, :], axis=1) + rank
    tile_expert = jnp.minimum(
        jnp.sum((jnp.arange(n_tiles, dtype=jnp.int32) * TMR)[:, None] >= ends[None, :], axis=1),
        N_EXPERTS - 1).astype(jnp.int32)
    n_valid = (ends[-1] // TMR).astype(jnp.int32).reshape(1)
    pos = jnp.clip(pos, 0, n_rows - 1)
    pos_tiles = pos.astype(jnp.int32).reshape(n // TSLOT, 1, 2 * TSLOT)

    pos_block = (1, 1, 2 * TSLOT)
    xs = pl.pallas_call(
        _dispatch_kernel,
        grid_spec=pltpu.PrefetchScalarGridSpec(
            num_scalar_prefetch=2,
            grid=(n // TSLOT,),
            in_specs=[pl.BlockSpec(pos_block, lambda t, ps, pn: (t, 0, 0), memory_space=pltpu.SMEM),
                      pl.BlockSpec((TSLOT, D_MODEL), lambda t, ps, pn: (t, 0))],
            out_specs=pl.BlockSpec(memory_space=pl.ANY),
            scratch_shapes=[pltpu.VMEM((SUBLANES, D_MODEL), F32),
                            pltpu.SemaphoreType.DMA, pltpu.SemaphoreType.DMA],
        ),
        out_shape=jax.ShapeDtypeStruct((n_rows, D_MODEL), F32),
        compiler_params=_cparams(1),
        name="dispatch",
    )((ends - padded + counts).astype(jnp.int32), (padded - counts).astype(jnp.int32), pos_tiles, h2)

    w_map = lambda t, te, nv: (te[t], 0, 0)
    ys = pl.pallas_call(
        _experts_kernel,
        grid_spec=pltpu.PrefetchScalarGridSpec(
            num_scalar_prefetch=2,
            grid=(n_tiles,),
            in_specs=[
                pl.BlockSpec((TMR, D_MODEL), lambda t, te, nv: (t, 0)),
                pl.BlockSpec((1, D_MODEL, D_EXPERT), w_map),
                pl.BlockSpec((1, D_MODEL, D_EXPERT), w_map),
                pl.BlockSpec((1, D_EXPERT, D_MODEL), w_map),
            ],
            out_specs=pl.BlockSpec((TMR, D_MODEL), lambda t, te, nv: (t, 0)),
        ),
        out_shape=jax.ShapeDtypeStruct((n_rows, D_MODEL), F32),
        compiler_params=_cparams(1),
        name="experts",
    )(tile_expert, n_valid, xs, w_eg, w_eu, w_ed)

    return pl.pallas_call(
        _combine_kernel,
        grid=(n // TSLOT,),
        in_specs=[pl.BlockSpec(pos_block, lambda t: (t, 0, 0), memory_space=pltpu.SMEM),
                  pl.BlockSpec((TSLOT, LANES), lambda t: (t, 0)),
                  pl.BlockSpec(memory_space=pl.ANY)],
        out_specs=pl.BlockSpec((TSLOT, D_MODEL), lambda t: (t, 0)),
        out_shape=jax.ShapeDtypeStruct((n, D_MODEL), F32),
        scratch_shapes=[pltpu.VMEM((2, TSLOT, D_MODEL), F32), pltpu.SemaphoreType.DMA],
        compiler_params=_cparams(1),
        name="combine",
    )(pos_tiles, route, ys)


def _final_kernel(x_ref, y_ref, mod_ref, g_ref, o_ref):
    x = x_ref[0] + mod_ref[0][5:6] * y_ref[0]
    o_ref[0] = x * lax.rsqrt(jnp.mean(x * x, axis=-1, keepdims=True) + EPS) * g_ref[...]


def _final_call(x, y, mod, g_final, n_lat_tok):
    bsz = x.shape[0]
    tok = pl.BlockSpec((1, TM, D_MODEL), lambda b, t: (b, t, 0))
    return pl.pallas_call(
        _final_kernel,
        grid=(bsz, n_lat_tok // TM),
        in_specs=[tok, tok,
                  pl.BlockSpec((1, N_MOD, D_MODEL), lambda b, t: (b, 0, 0)),
                  pl.BlockSpec((1, D_MODEL), lambda b, t: (0, 0))],
        out_specs=tok,
        out_shape=jax.ShapeDtypeStruct((bsz, n_lat_tok, D_MODEL), F32),
        compiler_params=_cparams(2),
        name="final_norm",
    )(x, y, mod, g_final)


def _rope_tables(n_lat_tok):
    t = np.arange(n_lat_tok)
    pos = np.stack([t // GRID_W, t % GRID_W], axis=0).astype(np.float64)
    inv = ROPE_THETA ** (-np.arange(ROPE_HALF, dtype=np.float64) / ROPE_HALF)
    d = np.arange(LANES) % HEAD_DIM
    axis = d // (2 * ROPE_HALF)
    freq = d % ROPE_HALF
    ang = pos[axis, :].T * inv[freq][None, :]
    sign = np.where((d % (2 * ROPE_HALF)) < ROPE_HALF, -1.0, 1.0)
    cos = np.concatenate([np.cos(ang), np.ones((CTX_LEN, LANES))], axis=0)
    sin = np.concatenate([np.sin(ang) * sign[None, :], np.zeros((CTX_LEN, LANES))], axis=0)
    return jnp.asarray(cos, F32), jnp.asarray(sin, F32)


def kernel(x, c, ctx, c_ctx, w_ada, b_ada, g_norm1, w_in, lam_qk, g_subln, w_attn_out, w_four_out,
           w_gate, b_gate, w_out, g_norm2, w_router, b_router, w_e_gate, w_e_up, w_e_down, g_final):
    bsz, n_lat_tok, d = x.shape
    depth = w_ada.shape[0]
    assert d == D_MODEL and ctx.shape[1] == CTX_LEN and bsz < MOD_ROWS
    assert n_lat_tok % TQ == 0 and n_lat_tok % (DFT2 * DFT_GROUP) == 0
    n_tok = n_lat_tok + CTX_LEN
    assert (bsz * n_tok) % TSLOT == 0

    cond = jnp.zeros((MOD_ROWS, D_MODEL), F32).at[:bsz].set(c).at[bsz].set(c_ctx)
    mods = _ada_call(cond, w_ada, b_ada).reshape(depth, MOD_ROWS, N_MOD, D_MODEL)
    cos_t, sin_t = _rope_tables(n_lat_tok)
    w_r = jnp.zeros((D_MODEL, LANES), F32).at[:, :N_EXPERTS].set(w_router)
    w_r_hi = w_r.astype(BF16)
    w_r = jnp.concatenate([w_r_hi, (w_r - w_r_hi.astype(F32)).astype(BF16)], axis=1)
    b_r = jnp.zeros((1, LANES), F32).at[0, :N_EXPERTS].set(b_router)

    xs = jnp.concatenate([x, ctx], axis=1)
    y = None
    for i in range(depth):
        lam_init = 0.8 - 0.6 * math.exp(-0.3 * i)
        xs, q, k, v, f = _proj_call(
            xs, y, mods[i - 1] if i else None, mods[i], g_norm1[i].reshape(1, D_MODEL),
            w_in[i].astype(BF16), cos_t, sin_t, bsz)
        o = _attn_call(q, k, v, lam_qk[i], g_subln[i], lam_init)
        z = _fourier_call(f, n_lat_tok)
        xs, h2, route = _merge_call(
            xs, o, z, mods[i], g_norm1[i].reshape(1, D_MODEL), g_norm2[i].reshape(1, D_MODEL),
            w_gate[i].astype(BF16), b_gate[i].reshape(1, 2 * D_MODEL), w_attn_out[i].astype(BF16),
            w_four_out[i].astype(BF16), w_out[i].astype(BF16), w_r, b_r, bsz)
        y = _moe_call(h2.reshape(bsz * n_tok, D_MODEL), route.reshape(bsz * n_tok, LANES),
                      w_e_gate[i], w_e_up[i], w_e_down[i])
        y = y.reshape(bsz, n_tok, D_MODEL)
    return _final_call(xs, y, mods[depth - 1], g_final.reshape(1, D_MODEL), n_lat_tok)
```

```python
import functools
import math

import numpy as np
import jax
import jax.numpy as jnp
from jax import lax
from jax.experimental import pallas as pl
from jax.experimental.pallas import tpu as pltpu

F32 = jnp.float32
BF16 = jnp.bfloat16
HIGHEST = lax.Precision.HIGHEST

D_MODEL = 1024
GRID_W = 64
CTX_LEN = 256
N_HEADS = 8
HEAD_DIM = 64
V_DIM = 128
QK_WIDTH = 1024
V_WIDTH = 1024
F_GROUPS = 4
F_WIDTH = 512
F_GROUP_DIM = 128
IN_WIDTH = 3584
ROPE_THETA = 10000.0
ROPE_HALF = 16
N_EXPERTS = 16
N_GROUPS = 4
EXPERTS_PER_GROUP = 4
D_EXPERT = 512
EPS = 1e-6
N_MOD = 6
MOD_ROWS = 8

LANES = 128
SUBLANES = 8
TM = 256
TQ = 1024
MXU_TILE = 256
BF16_ROWS = 16
SUM_ROWS = 16
MAX_KEY_TILES = 16
QK_ROWS = 1024
LAG_LIMIT = 100.0
SUM_FLOOR = 2.0 ** -60
TMR = 256
TSLOT = 512
DMA_UNROLL = 8
ADA_TN = 512
DFT2 = 128
DFT_GROUP = 8
Q_SCALE = (HEAD_DIM ** -0.5) * math.log2(math.e)
VMEM_LIMIT = 56 * 1024 * 1024


def _cparams(n_axes):
    return pltpu.CompilerParams(dimension_semantics=("arbitrary",) * n_axes,
                                vmem_limit_bytes=VMEM_LIMIT)


def _rms_mod(x, g, shift, scale):
    y = x * lax.rsqrt(jnp.mean(x * x, axis=-1, keepdims=True) + EPS) * g
    return y * (1.0 + scale) + shift


def _ada_kernel(c_ref, w_ref, b_ref, o_ref):
    c = c_ref[...]
    s = c * jax.nn.sigmoid(c)
    o_ref[0] = jnp.dot(s, w_ref[0], preferred_element_type=F32, precision=HIGHEST) + b_ref[0]


def _ada_call(cond, w_ada, b_ada):
    depth = w_ada.shape[0]
    n_out = w_ada.shape[2]
    return pl.pallas_call(
        _ada_kernel,
        grid=(depth, n_out // ADA_TN),
        in_specs=[
            pl.BlockSpec((MOD_ROWS, D_MODEL), lambda l, j: (0, 0)),
            pl.BlockSpec((1, D_MODEL, ADA_TN), lambda l, j: (l, 0, j)),
            pl.BlockSpec((1, 1, ADA_TN), lambda l, j: (l, 0, j)),
        ],
        out_specs=pl.BlockSpec((1, MOD_ROWS, ADA_TN), lambda l, j: (l, 0, j)),
        out_shape=jax.ShapeDtypeStruct((depth, MOD_ROWS, n_out), F32),
        compiler_params=_cparams(2),
        name="ada",
    )(cond, w_ada, b_ada.reshape(depth, 1, n_out))


def _proj_kernel(*refs, has_y):
    if has_y:
        (x_ref, y_ref, pmod_ref, mod_ref, g1_ref, win_ref, cos_ref, sin_ref,
         xo_ref, q_ref, k_ref, v_ref, f_ref) = refs
        x = x_ref[0] + pmod_ref[0][5:6] * y_ref[0]
        xo_ref[0] = x
    else:
        (x_ref, mod_ref, g1_ref, win_ref, cos_ref, sin_ref,
         q_ref, k_ref, v_ref, f_ref) = refs
        x = x_ref[0]
    mod = mod_ref[0]
    hb = _rms_mod(x, g1_ref[...], mod[0:1], mod[1:2]).astype(BF16)
    cos = cos_ref[...]
    sin = sin_ref[...]
    lane = lax.broadcasted_iota(jnp.int32, (TM, LANES), 1)
    first = (lane % (2 * ROPE_HALF)) < ROPE_HALF

    def rope(xc):
        up = pltpu.roll(xc, LANES - ROPE_HALF, 1)
        dn = pltpu.roll(xc, ROPE_HALF, 1)
        return xc * cos + jnp.where(first, up, dn) * sin

    w2 = 2 * LANES
    for c in range(QK_WIDTH // w2):
        uq = jnp.dot(hb, win_ref[:, c * w2:(c + 1) * w2], preferred_element_type=F32)
        uk = jnp.dot(hb, win_ref[:, QK_WIDTH + c * w2:QK_WIDTH + (c + 1) * w2],
                     preferred_element_type=F32)
        for j in range(2):
            lo = c * w2 + j * LANES
            q_ref[0, :, lo:lo + LANES] = (rope(uq[:, j * LANES:(j + 1) * LANES]) * Q_SCALE).astype(BF16)
            k_ref[0, :, lo:lo + LANES] = rope(uk[:, j * LANES:(j + 1) * LANES]).astype(BF16)
    for c in range(V_WIDTH // w2):
        lo = 2 * QK_WIDTH + c * w2
        v_ref[0, :, c * w2:(c + 1) * w2] = jnp.dot(
            hb, win_ref[:, lo:lo + w2], preferred_element_type=F32).astype(BF16)
    for c in range(F_WIDTH // w2):
        lo = 2 * QK_WIDTH + V_WIDTH + c * w2
        f_ref[0, :, c * w2:(c + 1) * w2] = jnp.dot(
            hb, win_ref[:, lo:lo + w2], preferred_element_type=F32)


def _proj_call(x, y, pmod, mod, g1, w_in, cos_t, sin_t, n_batch):
    bsz, n_tok, _ = x.shape
    nt = n_tok // TM
    has_y = y is not None

    def mod_map(b, t):
        return (jnp.where(t == nt - 1, n_batch, b), 0, 0)

    tok = lambda w: pl.BlockSpec((1, TM, w), lambda b, t: (b, t, 0))
    modspec = pl.BlockSpec((1, N_MOD, D_MODEL), mod_map)
    in_specs = [tok(D_MODEL)]
    args = [x]
    if has_y:
        in_specs += [tok(D_MODEL), modspec]
        args += [y, pmod]
    in_specs += [
        modspec,
        pl.BlockSpec((1, D_MODEL), lambda b, t: (0, 0)),
        pl.BlockSpec((D_MODEL, IN_WIDTH), lambda b, t: (0, 0)),
        pl.BlockSpec((TM, LANES), lambda b, t: (t, 0)),
        pl.BlockSpec((TM, LANES), lambda b, t: (t, 0)),
    ]
    args += [mod, g1, w_in, cos_t, sin_t]
    out_specs = [tok(QK_WIDTH), tok(QK_WIDTH), tok(V_WIDTH), tok(F_WIDTH)]
    out_shape = [
        jax.ShapeDtypeStruct((bsz, n_tok, QK_WIDTH), BF16),
        jax.ShapeDtypeStruct((bsz, n_tok, QK_WIDTH), BF16),
        jax.ShapeDtypeStruct((bsz, n_tok, V_WIDTH), BF16),
        jax.ShapeDtypeStruct((bsz, n_tok, F_WIDTH), F32),
    ]
    if has_y:
        out_specs = [tok(D_MODEL)] + out_specs
        out_shape = [jax.ShapeDtypeStruct((bsz, n_tok, D_MODEL), F32)] + out_shape
    outs = pl.pallas_call(
        functools.partial(_proj_kernel, has_y=has_y),
        grid=(bsz, nt),
        in_specs=in_specs,
        out_specs=out_specs,
        out_shape=out_shape,
        compiler_params=_cparams(2),
        name="proj",
    )(*args)
    if has_y:
        return outs
    return [x] + list(outs)


def _split_maps_t(qt):
    row = lax.broadcasted_iota(jnp.int32, qt.shape, 0)
    zero = jnp.zeros_like(qt)
    return jnp.concatenate([jnp.where(row < HEAD_DIM, qt, zero),
                            jnp.where(row >= HEAD_DIM, qt, zero)], axis=1)


def _ones_rows(n_cols):
    row = lax.broadcasted_iota(jnp.int32, (SUM_ROWS, n_cols), 0)
    return jnp.where(row == 0, 1.0, 0.0).astype(BF16)


def _vt_block(vt, n_keys):
    return jnp.concatenate([vt, _ones_rows(n_keys)], axis=0)


def _softmax_block_t(q2t, kb, vbt_ext, m_prev, acc_prev):
    st = jnp.dot(kb, q2t, preferred_element_type=F32)
    m_new = jnp.maximum(m_prev, jnp.max(st, axis=0, keepdims=True))
    alpha = jnp.exp2(m_prev - m_new)
    pt = jnp.exp2(st - m_new).astype(BF16)
    acc = alpha * acc_prev + jnp.dot(vbt_ext, pt, preferred_element_type=F32)
    return m_new, acc


def _diff_combine_t(acc, lam_ref, gs_ref, lam_init):
    tq = acc.shape[1] // 2
    lp = lam_ref[...]
    lam = (jnp.exp(jnp.sum(lp[0:1] * lp[1:2], axis=1, keepdims=True))
           - jnp.exp(jnp.sum(lp[2:3] * lp[3:4], axis=1, keepdims=True)) + lam_init)
    pv = acc[:V_DIM]
    l = acc[V_DIM:V_DIM + 1]
    ot = pv[:, :tq] / l[:, :tq] - lam * (pv[:, tq:] / l[:, tq:])
    ot = ot * lax.rsqrt(jnp.mean(ot * ot, axis=0, keepdims=True) + EPS) * gs_ref[...]
    return (ot * (1.0 - lam_init)).T.astype(BF16)


def _attn_kernel(lam_ref, gs_ref, q_ref, k_ref, v_ref, o_ref, vt_ref, q2t_ref, m_ref, acc_ref,
                 lag_ref, pt_ref, *, n_lat_tok, tk, lam_init):
    n_tok = n_lat_tok + CTX_LEN

    @pl.when(pl.program_id(2) == 0)
    def _():
        for j in range(n_tok // MXU_TILE):
            sl = slice(j * MXU_TILE, (j + 1) * MXU_TILE)
            vt_ref[:, sl] = v_ref[0, sl, :].astype(F32).T.astype(BF16)

    q2t_ref[...] = _split_maps_t(q_ref[0].astype(F32).T.astype(BF16))

    def reset(m0):
        m_ref[...] = jnp.full(m_ref.shape, m0, F32)
        acc_ref[...] = jnp.zeros(acc_ref.shape, F32)

    def exact_chunk(off):
        m_new, acc = _softmax_block_t(
            q2t_ref[...], k_ref[0, pl.ds(off, MXU_TILE), :],
            _vt_block(vt_ref[:, pl.ds(off, MXU_TILE)], MXU_TILE), m_ref[...], acc_ref[...])
        m_ref[...] = m_new
        acc_ref[...] = acc

    def lagged_block(off, n_keys):
        m_prev = m_ref[...]
        pmax = None
        for r in range(0, n_keys, QK_ROWS):
            rows = min(QK_ROWS, n_keys - r)
            st = jnp.dot(k_ref[0, pl.ds(off + r, rows), :], q2t_ref[...], preferred_element_type=F32)
            p = jnp.exp2(st - m_prev).astype(BF16)
            pt_ref[r:r + rows, :] = p
            for t in range(0, rows, BF16_ROWS):
                tile = p[t:t + BF16_ROWS, :]
                pmax = tile if pmax is None else jnp.maximum(pmax, tile)
        acc = acc_ref[...] + jnp.dot(_vt_block(vt_ref[:, pl.ds(off, n_keys)], n_keys), pt_ref[0:n_keys, :],
                                     preferred_element_type=F32)
        excess = jnp.log2(jnp.max(pmax.astype(F32), axis=0, keepdims=True))
        m_new = m_prev + jnp.maximum(excess, 0.0)
        acc_ref[...] = acc * jnp.exp2(m_prev - m_new)
        m_ref[...] = m_new
        lag_ref[...] = jnp.maximum(lag_ref[...], excess)

    reset(0.0)
    lag_ref[...] = jnp.full(lag_ref.shape, -jnp.inf, F32)
    n_full = n_tok // tk - 1
    lax.fori_loop(0, n_full, lambda j, carry: (lagged_block(pl.multiple_of(j * tk, tk), tk), carry)[1], 0)
    lagged_block(n_full * tk, n_tok - n_full * tk)

    sum_min = jnp.min(acc_ref[V_DIM:V_DIM + 1, :])

    @pl.when(jnp.logical_or(jnp.max(lag_ref[...]) > LAG_LIMIT, sum_min < SUM_FLOOR))
    def _():
        reset(-jnp.inf)
        lax.fori_loop(0, n_tok // MXU_TILE,
                      lambda c, carry: (exact_chunk(pl.multiple_of(c * MXU_TILE, MXU_TILE)), carry)[1], 0)

    o_ref[0] = _diff_combine_t(acc_ref[...], lam_ref, gs_ref, lam_init)


def _attn_ctx_kernel(lam_ref, gs_ref, q_ref, k_ref, v_ref, o_any, o_ref, *, lam_init):
    del o_any
    q2t = _split_maps_t(q_ref[0].astype(F32).T.astype(BF16))
    vbt = _vt_block(v_ref[0].astype(F32).T.astype(BF16), CTX_LEN)
    m0 = jnp.full((1, 2 * CTX_LEN), -jnp.inf, F32)
    acc0 = jnp.zeros((V_DIM + SUM_ROWS, 2 * CTX_LEN), F32)
    _, acc = _softmax_block_t(q2t, k_ref[0], vbt, m0, acc0)
    o_ref[0] = _diff_combine_t(acc, lam_ref, gs_ref, lam_init)


def _key_block(n_tok):
    for mult in range(MAX_KEY_TILES, 0, -1):
        if n_tok % (mult * MXU_TILE) == 0:
            return mult * MXU_TILE
    raise ValueError(n_tok)


def _attn_call(q, k, v, lam_qk, g_subln, lam_init):
    bsz, n_tok, _ = q.shape
    n_lat_tok = n_tok - CTX_LEN
    tk = _key_block(n_lat_tok)
    tq_step = TQ
    n_col = 2 * TQ
    small = [pl.BlockSpec((4, HEAD_DIM), lambda *_: (0, 0)),
             pl.BlockSpec((V_DIM, 1), lambda *_: (0, 0))]
    gs = g_subln.reshape(V_DIM, 1)
    o = pl.pallas_call(
        functools.partial(_attn_kernel, n_lat_tok=n_lat_tok, tk=tk, lam_init=lam_init),
        grid=(bsz, N_HEADS, n_lat_tok // tq_step),
        in_specs=small + [
            pl.BlockSpec((1, tq_step, LANES), lambda b, h, t: (b, t, h)),
            pl.BlockSpec((1, n_tok, LANES), lambda b, h, t: (b, 0, h)),
            pl.BlockSpec((1, n_tok, LANES), lambda b, h, t: (b, 0, h)),
        ],
        out_specs=pl.BlockSpec((1, tq_step, V_DIM), lambda b, h, t: (b, t, h)),
        out_shape=jax.ShapeDtypeStruct((bsz, n_tok, V_WIDTH), BF16),
        scratch_shapes=[
            pltpu.VMEM((V_DIM, n_tok), BF16),
            pltpu.VMEM((LANES, n_col), BF16),
            pltpu.VMEM((1, n_col), F32),
            pltpu.VMEM((V_DIM + SUM_ROWS, n_col), F32),
            pltpu.VMEM((1, n_col), F32),
            pltpu.VMEM((n_tok - (n_tok // tk - 1) * tk, n_col), BF16),
        ],
        compiler_params=_cparams(3),
        name="diff_attn",
    )(lam_qk, gs, q, k, v)
    ctx_blk = n_lat_tok // CTX_LEN
    ctx_spec = pl.BlockSpec((1, CTX_LEN, LANES), lambda b, h: (b, ctx_blk, h))
    return pl.pallas_call(
        functools.partial(_attn_ctx_kernel, lam_init=lam_init),
        grid=(bsz, N_HEADS),
        in_specs=small + [ctx_spec, ctx_spec, ctx_spec, pl.BlockSpec(memory_space=pl.ANY)],
        out_specs=ctx_spec,
        out_shape=jax.ShapeDtypeStruct((bsz, n_tok, V_WIDTH), BF16),
        input_output_aliases={5: 0},
        compiler_params=_cparams(2),
        name="diff_attn_ctx",
    )(lam_qk, gs, q, k, v, o)


def _dft1_kernel(u_ref, c1_ref, s1_ref, twc_ref, tws_ref, o_ref):
    n1 = u_ref.shape[1]
    rows = n1 * DFT_GROUP
    u = u_ref[0].reshape(rows, F_WIDTH).astype(BF16)
    ar = jnp.dot(c1_ref[...], u, preferred_element_type=F32)
    ai = -jnp.dot(s1_ref[...], u, preferred_element_type=F32)
    twc = twc_ref[0]
    tws = tws_ref[0]
    out = []
    for part in (lambda a_r, a_i: a_r * twc + a_i * tws, lambda a_r, a_i: a_i * twc - a_r * tws):
        for g in range(F_GROUPS):
            sl = slice(g * LANES, (g + 1) * LANES)
            out.append(part(ar[:, sl], ai[:, sl]))
    o_ref[0] = jnp.concatenate(out, axis=1).reshape(n1, DFT_GROUP, 2 * F_WIDTH)


def _channel_dft(gr, gi, cc_ref, sc_ref, norm):
    grb = gr.astype(BF16)
    gib = gi.astype(BF16)
    out = []
    for g in range(F_GROUPS):
        sl = slice(g * LANES, (g + 1) * LANES)
        out.append((jnp.dot(grb[:, sl], cc_ref[...], preferred_element_type=F32)
                    + jnp.dot(gib[:, sl], sc_ref[...], preferred_element_type=F32)) * norm)
    return out


def _dft2_kernel(b_ref, c2_ref, s2_ref, cc_ref, sc_ref, o_ref, *, norm):
    for j in range(DFT_GROUP):
        bb = b_ref[0, j].astype(BF16)
        cb = jnp.dot(c2_ref[...], bb, preferred_element_type=F32)
        sb = jnp.dot(s2_ref[...], bb, preferred_element_type=F32)
        gr = cb[:, :F_WIDTH] + sb[:, F_WIDTH:]
        gi = cb[:, F_WIDTH:] - sb[:, :F_WIDTH]
        for g, z in enumerate(_channel_dft(gr, gi, cc_ref, sc_ref, norm)):
            o_ref[0, :, j, g * LANES:(g + 1) * LANES] = z


def _dft_ctx_kernel(f_ref, c_ref, s_ref, cc_ref, sc_ref, z_any, o_ref, *, norm):
    del z_any
    f = f_ref[0].astype(BF16)
    gr = jnp.dot(c_ref[...], f, preferred_element_type=F32)
    gi = -jnp.dot(s_ref[...], f, preferred_element_type=F32)
    for g, z in enumerate(_channel_dft(gr, gi, cc_ref, sc_ref, norm)):
        o_ref[0, :, g * LANES:(g + 1) * LANES] = z


def _dft_mats(n, repeat=1):
    idx = np.arange(n, dtype=np.float64)
    ang = 2.0 * np.pi * np.outer(idx, idx) / n
    eye = np.eye(repeat)
    return (jnp.asarray(np.kron(np.cos(ang), eye), BF16), jnp.asarray(np.kron(np.sin(ang), eye), BF16))


def _fourier_call(f, n_lat_tok):
    bsz, n_tok, _ = f.shape
    n1 = n_lat_tok // DFT2
    full = lambda *shape: pl.BlockSpec(shape, lambda *_: (0,) * len(shape))
    c1, s1 = _dft_mats(n1, DFT_GROUP)
    c2, s2 = _dft_mats(DFT2)
    cc, sc = _dft_mats(F_GROUP_DIM)
    cctx, sctx = _dft_mats(CTX_LEN)
    tw_ang = (2.0 * np.pi / n_lat_tok) * np.einsum(
        "sj,k->skj", np.arange(DFT2, dtype=np.float64).reshape(DFT2 // DFT_GROUP, DFT_GROUP),
        np.arange(n1, dtype=np.float64)).reshape(DFT2 // DFT_GROUP, n1 * DFT_GROUP)
    twc = jnp.asarray(np.broadcast_to(np.cos(tw_ang)[:, :, None], tw_ang.shape + (LANES,)), F32)
    tws = jnp.asarray(np.broadcast_to(np.sin(tw_ang)[:, :, None], tw_ang.shape + (LANES,)), F32)

    fv = f.reshape(bsz, n_tok // DFT2, DFT2, F_WIDTH)
    mid = pl.pallas_call(
        _dft1_kernel,
        grid=(bsz, DFT2 // DFT_GROUP),
        in_specs=[
            pl.BlockSpec((1, n1, DFT_GROUP, F_WIDTH), lambda b, j: (b, 0, j, 0)),
            full(n1 * DFT_GROUP, n1 * DFT_GROUP), full(n1 * DFT_GROUP, n1 * DFT_GROUP),
            pl.BlockSpec((1, n1 * DFT_GROUP, LANES), lambda b, j: (j, 0, 0)),
            pl.BlockSpec((1, n1 * DFT_GROUP, LANES), lambda b, j: (j, 0, 0)),
        ],
        out_specs=pl.BlockSpec((1, n1, DFT_GROUP, 2 * F_WIDTH), lambda b, j: (b, 0, j, 0)),
        out_shape=jax.ShapeDtypeStruct((bsz, n1, DFT2, 2 * F_WIDTH), F32),
        compiler_params=_cparams(2),
        name="dft_outer",
    )(fv, c1, s1, twc, tws)

    z = pl.pallas_call(
        functools.partial(_dft2_kernel, norm=1.0 / math.sqrt(n_lat_tok * F_GROUP_DIM)),
        grid=(bsz, n1 // DFT_GROUP),
        in_specs=[
            pl.BlockSpec((1, DFT_GROUP, DFT2, 2 * F_WIDTH), lambda b, j: (b, j, 0, 0)),
            full(DFT2, DFT2), full(DFT2, DFT2),
            full(F_GROUP_DIM, F_GROUP_DIM), full(F_GROUP_DIM, F_GROUP_DIM),
        ],
        out_specs=pl.BlockSpec((1, DFT2, DFT_GROUP, F_WIDTH), lambda b, j: (b, 0, j, 0)),
        out_shape=jax.ShapeDtypeStruct((bsz, n_tok // n1, n1, F_WIDTH), F32),
        compiler_params=_cparams(2),
        name="dft_inner",
    )(mid, c2, s2, cc, sc)
    z = z.reshape(bsz, n_tok, F_WIDTH)

    ctx_blk = n_lat_tok // CTX_LEN
    z = pl.pallas_call(
        functools.partial(_dft_ctx_kernel, norm=1.0 / math.sqrt(CTX_LEN * F_GROUP_DIM)),
        grid=(bsz,),
        in_specs=[
            pl.BlockSpec((1, CTX_LEN, F_WIDTH), lambda b: (b, ctx_blk, 0)),
            full(CTX_LEN, CTX_LEN), full(CTX_LEN, CTX_LEN),
            full(F_GROUP_DIM, F_GROUP_DIM), full(F_GROUP_DIM, F_GROUP_DIM),
            pl.BlockSpec(memory_space=pl.ANY),
        ],
        out_specs=pl.BlockSpec((1, CTX_LEN, F_WIDTH), lambda b: (b, ctx_blk, 0)),
        out_shape=jax.ShapeDtypeStruct((bsz, n_tok, F_WIDTH), F32),
        input_output_aliases={5: 0},
        compiler_params=_cparams(1),
        name="dft_ctx",
    )(f, cctx, sctx, cc, sc, z)
    return z


def _merge_kernel(x_ref, o_ref, z_ref, mod_ref, g1_ref, g2_ref, wgate_ref, bgate_ref,
                  wao_ref, wfo_ref, wout_ref, wr_ref, br_ref, xo_ref, h2_ref, route_ref):
    x = x_ref[0]
    mod = mod_ref[0]
    hb = _rms_mod(x, g1_ref[...], mod[0:1], mod[1:2]).astype(BF16)
    gates = jax.nn.sigmoid(jnp.dot(hb, wgate_ref[...], preferred_element_type=F32) + bgate_ref[...])
    ya = jnp.dot(o_ref[0], wao_ref[...], preferred_element_type=F32)
    yf = jnp.dot(z_ref[0].astype(BF16), wfo_ref[...], preferred_element_type=F32)
    merged = gates[:, :D_MODEL] * ya + gates[:, D_MODEL:] * yf
    out = jnp.dot(merged.astype(BF16), wout_ref[...], preferred_element_type=F32)
    xn = x + mod[2:3] * out
    xo_ref[0] = xn
    h2 = _rms_mod(xn, g2_ref[...], mod[3:4], mod[4:5])
    h2_ref[0] = h2

    h2_hi = h2.astype(BF16)
    h2_lo = (h2 - h2_hi.astype(F32)).astype(BF16)
    part = jnp.dot(h2_hi, wr_ref[...], preferred_element_type=F32)
    logits = (part[:, :LANES] + part[:, LANES:]
              + jnp.dot(h2_lo, wr_ref[:, :LANES], preferred_element_type=F32))
    aff = jax.nn.sigmoid(logits)
    lane = lax.broadcasted_iota(jnp.int32, (TM, LANES), 1)
    lanef = lane.astype(F32)
    neg = jnp.full((TM, LANES), -jnp.inf, F32)
    sel = jnp.where(lane < N_EXPERTS, aff + br_ref[...], neg)
    grp = lane // EXPERTS_PER_GROUP
    best = i1b = i2b = None
    for g in range(N_GROUPS):
        sg = jnp.where(grp == g, sel, neg)
        m1 = jnp.max(sg, axis=1, keepdims=True)
        i1 = jnp.min(jnp.where(sg == m1, lanef, float(LANES)), axis=1, keepdims=True)
        sg2 = jnp.where(lanef == i1, neg, sg)
        m2 = jnp.max(sg2, axis=1, keepdims=True)
        i2 = jnp.min(jnp.where(sg2 == m2, lanef, float(LANES)), axis=1, keepdims=True)
        score = m1 + m2
        if g == 0:
            best, i1b, i2b = score, i1, i2
        else:
            upd = score > best
            best = jnp.where(upd, score, best)
            i1b = jnp.where(upd, i1, i1b)
            i2b = jnp.where(upd, i2, i2b)
    hit1 = lanef == i1b
    hit2 = lanef == i2b
    a1 = jnp.sum(jnp.where(hit1, aff, 0.0), axis=1, keepdims=True)
    a2 = jnp.sum(jnp.where(hit2, aff, 0.0), axis=1, keepdims=True)
    den = a1 + a2
    route_ref[0] = (jnp.where(lane == 0, i1b, 0.0) + jnp.where(lane == 1, i2b, 0.0)
                    + jnp.where(lane == 2, a1 / den, 0.0) + jnp.where(lane == 3, a2 / den, 0.0))


def _merge_call(x, o, z, mod, g1, g2, w_gate, b_gate, w_ao, w_fo, w_out, w_r, b_r, n_batch):
    bsz, n_tok, _ = x.shape
    nt = n_tok // TM
    tok = lambda w: pl.BlockSpec((1, TM, w), lambda b, t: (b, t, 0))
    full = lambda *shape: pl.BlockSpec(shape, lambda b, t: (0,) * len(shape))
    return pl.pallas_call(
        _merge_kernel,
        grid=(bsz, nt),
        in_specs=[
            tok(D_MODEL), tok(V_WIDTH), tok(F_WIDTH),
            pl.BlockSpec((1, N_MOD, D_MODEL),
                         lambda b, t: (jnp.where(t == nt - 1, n_batch, b), 0, 0)),
            full(1, D_MODEL), full(1, D_MODEL),
            full(D_MODEL, 2 * D_MODEL), full(1, 2 * D_MODEL),
            full(V_WIDTH, D_MODEL), full(F_WIDTH, D_MODEL), full(D_MODEL, D_MODEL),
            full(D_MODEL, 2 * LANES), full(1, LANES),
        ],
        out_specs=[tok(D_MODEL), tok(D_MODEL), tok(LANES)],
        out_shape=[
            jax.ShapeDtypeStruct((bsz, n_tok, D_MODEL), F32),
            jax.ShapeDtypeStruct((bsz, n_tok, D_MODEL), F32),
            jax.ShapeDtypeStruct((bsz, n_tok, LANES), F32),
        ],
        compiler_params=_cparams(2),
        name="merge_route",
    )(x, o, z, mod, g1, g2, w_gate, b_gate, w_ao, w_fo, w_out, w_r, b_r)


def _slot_copy(src_ref, src_row, dst_ref, dst_row, sem):
    return pltpu.make_async_copy(src_ref.at[pl.ds(src_row, 1)], dst_ref.at[pl.ds(dst_row, 1)], sem)


def _dispatch_kernel(pad_start_ref, pad_len_ref, pos_ref, h_ref, xs_ref, zero_ref, sem, pad_sem):
    @pl.when(pl.program_id(0) == 0)
    def _():
        zero_ref[...] = jnp.zeros(zero_ref.shape, F32)

        def pad_copy(e, r):
            return _slot_copy(zero_ref, 0, xs_ref, pad_start_ref[e] + r, pad_sem)

        for e in range(N_EXPERTS):
            lax.fori_loop(0, pad_len_ref[e], lambda r, carry, e=e: (pad_copy(e, r).start(), carry)[1], 0)
        for e in range(N_EXPERTS):
            lax.fori_loop(0, pad_len_ref[e], lambda r, carry, e=e: (pad_copy(e, r).wait(), carry)[1], 0)

    def copy(r, k):
        return _slot_copy(h_ref, r, xs_ref, pos_ref[0, 0, 2 * r + k], sem)

    def issue(r, carry):
        copy(r, 0).start(priority=0)
        copy(r, 1).start(priority=1)
        return carry

    def drain(r, carry):
        copy(r, 0).wait()
        copy(r, 1).wait()
        return carry

    lax.fori_loop(0, TSLOT, issue, 0, unroll=DMA_UNROLL)
    lax.fori_loop(0, TSLOT, drain, 0, unroll=DMA_UNROLL)


def _experts_kernel(te_ref, nv_ref, xs_ref, wg_ref, wu_ref, wd_ref, ys_ref):
    del te_ref
    t = pl.program_id(0)

    @pl.when(t < nv_ref[0])
    def _():
        h = xs_ref[...].astype(BF16)
        a = jnp.dot(h, wg_ref[0].astype(BF16), preferred_element_type=F32)
        u = jnp.dot(h, wu_ref[0].astype(BF16), preferred_element_type=F32)
        he = (a * jax.nn.sigmoid(a) * u).astype(BF16)
        ys_ref[...] = jnp.dot(he, wd_ref[0].astype(BF16), preferred_element_type=F32)

    @pl.when(t >= nv_ref[0])
    def _():
        ys_ref[...] = jnp.zeros(ys_ref.shape, F32)


def _combine_kernel(pos_ref, route_ref, ys_ref, y_ref, buf_ref, sem):
    def copy(r, k):
        return _slot_copy(ys_ref, pos_ref[0, 0, 2 * r + k], buf_ref.at[k], r, sem)

    def issue(r, carry):
        copy(r, 0).start(priority=0)
        copy(r, 1).start(priority=1)
        return carry

    def drain(r, carry):
        copy(r, 0).wait()
        copy(r, 1).wait()
        return carry

    lax.fori_loop(0, TSLOT, issue, 0, unroll=DMA_UNROLL)
    lax.fori_loop(0, TSLOT, drain, 0, unroll=DMA_UNROLL)
    route = route_ref[...]
    y_ref[...] = route[:, 2:3] * buf_ref[0] + route[:, 3:4] * buf_ref[1]


def _moe_call(h2, route, w_eg, w_eu, w_ed):
    n = h2.shape[0]
    n_slots = 2 * n
    n_rows = n_slots + N_EXPERTS * TMR
    n_tiles = n_rows // TMR

    e_slot = route[:, :2].astype(jnp.int32).reshape(n_slots)
    onehot = (e_slot[:, None] == jnp.arange(N_EXPERTS, dtype=jnp.int32)[None, :]).astype(jnp.int32)
    csum = jnp.cumsum(onehot, axis=0)
    rank = jnp.sum(csum * onehot, axis=1) - 1
    counts = csum[-1]
    padded = ((counts + TMR - 1) // TMR) * TMR
    ends = jnp.cumsum(padded)
    pos = jnp.sum(onehot * (ends - padded)[None, :], axis=1) + rank
    tile_expert = jnp.minimum(
        jnp.sum((jnp.arange(n_tiles, dtype=jnp.int32) * TMR)[:, None] >= ends[None, :], axis=1),
        N_EXPERTS - 1).astype(jnp.int32)
    n_valid = (ends[-1] // TMR).astype(jnp.int32).reshape(1)
    pos = jnp.clip(pos, 0, n_rows - 1)
    pos_tiles = pos.astype(jnp.int32).reshape(n // TSLOT, 1, 2 * TSLOT)

    pos_block = (1, 1, 2 * TSLOT)
    xs = pl.pallas_call(
        _dispatch_kernel,
        grid_spec=pltpu.PrefetchScalarGridSpec(
            num_scalar_prefetch=2,
            grid=(n // TSLOT,),
            in_specs=[pl.BlockSpec(pos_block, lambda t, ps, pn: (t, 0, 0), memory_space=pltpu.SMEM),
                      pl.BlockSpec((TSLOT, D_MODEL), lambda t, ps, pn: (t, 0))],
            out_specs=pl.BlockSpec(memory_space=pl.ANY),
            scratch_shapes=[pltpu.VMEM((SUBLANES, D_MODEL), F32),
                            pltpu.SemaphoreType.DMA, pltpu.SemaphoreType.DMA],
        ),
        out_shape=jax.ShapeDtypeStruct((n_rows, D_MODEL), F32),
        compiler_params=_cparams(1),
        name="dispatch",
    )((ends - padded + counts).astype(jnp.int32), (padded - counts).astype(jnp.int32), pos_tiles, h2)

    w_map = lambda t, te, nv: (te[t], 0, 0)
    ys = pl.pallas_call(
        _experts_kernel,
        grid_spec=pltpu.PrefetchScalarGridSpec(
            num_scalar_prefetch=2,
            grid=(n_tiles,),
            in_specs=[
                pl.BlockSpec((TMR, D_MODEL), lambda t, te, nv: (t, 0)),
                pl.BlockSpec((1, D_MODEL, D_EXPERT), w_map),
                pl.BlockSpec((1, D_MODEL, D_EXPERT), w_map),
                pl.BlockSpec((1, D_EXPERT, D_MODEL), w_map),
            ],
            out_specs=pl.BlockSpec((TMR, D_MODEL), lambda t, te, nv: (t, 0)),
        ),
        out_shape=jax.ShapeDtypeStruct((n_rows, D_MODEL), F32),
        compiler_params=_cparams(1),
        name="experts",
    )(tile_expert, n_valid, xs, w_eg, w_eu, w_ed)

    return pl.pallas_call(
        _combine_kernel,
        grid=(n // TSLOT,),
        in_specs=[pl.BlockSpec(pos_block, lambda t: (t, 0, 0), memory_space=pltpu.SMEM),
                  pl.BlockSpec((TSLOT, LANES), lambda t: (t, 0)),
                  pl.BlockSpec(memory_space=pl.ANY)],
        out_specs=pl.BlockSpec((TSLOT, D_MODEL), lambda t: (t, 0)),
        out_shape=jax.ShapeDtypeStruct((n, D_MODEL), F32),
        scratch_shapes=[pltpu.VMEM((2, TSLOT, D_MODEL), F32), pltpu.SemaphoreType.DMA],
        compiler_params=_cparams(1),
        name="combine",
    )(pos_tiles, route, ys)


def _final_kernel(x_ref, y_ref, mod_ref, g_ref, o_ref):
    x = x_ref[0] + mod_ref[0][5:6] * y_ref[0]
    o_ref[0] = x * lax.rsqrt(jnp.mean(x * x, axis=-1, keepdims=True) + EPS) * g_ref[...]


def _final_call(x, y, mod, g_final, n_lat_tok):
    bsz = x.shape[0]
    tok = pl.BlockSpec((1, TM, D_MODEL), lambda b, t: (b, t, 0))
    return pl.pallas_call(
        _final_kernel,
        grid=(bsz, n_lat_tok // TM),
        in_specs=[tok, tok,
                  pl.BlockSpec((1, N_MOD, D_MODEL), lambda b, t: (b, 0, 0)),
                  pl.BlockSpec((1, D_MODEL), lambda b, t: (0, 0))],
        out_specs=tok,
        out_shape=jax.ShapeDtypeStruct((bsz, n_lat_tok, D_MODEL), F32),
        compiler_params=_cparams(2),
        name="final_norm",
    )(x, y, mod, g_final)


def _rope_tables(n_lat_tok):
    t = np.arange(n_lat_tok)
    pos = np.stack([t // GRID_W, t % GRID_W], axis=0).astype(np.float64)
    inv = ROPE_THETA ** (-np.arange(ROPE_HALF, dtype=np.float64) / ROPE_HALF)
    d = np.arange(LANES) % HEAD_DIM
    axis = d // (2 * ROPE_HALF)
    freq = d % ROPE_HALF
    ang = pos[axis, :].T * inv[freq][None, :]
    sign = np.where((d % (2 * ROPE_HALF)) < ROPE_HALF, -1.0, 1.0)
    cos = np.concatenate([np.cos(ang), np.ones((CTX_LEN, LANES))], axis=0)
    sin = np.concatenate([np.sin(ang) * sign[None, :], np.zeros((CTX_LEN, LANES))], axis=0)
    return jnp.asarray(cos, F32), jnp.asarray(sin, F32)


def kernel(x, c, ctx, c_ctx, w_ada, b_ada, g_norm1, w_in, lam_qk, g_subln, w_attn_out, w_four_out,
           w_gate, b_gate, w_out, g_norm2, w_router, b_router, w_e_gate, w_e_up, w_e_down, g_final):
    bsz, n_lat_tok, d = x.shape
    depth = w_ada.shape[0]
    assert d == D_MODEL and ctx.shape[1] == CTX_LEN and bsz < MOD_ROWS
    assert n_lat_tok % TQ == 0 and n_lat_tok % (DFT2 * DFT_GROUP) == 0
    n_tok = n_lat_tok + CTX_LEN
    assert (bsz * n_tok) % TSLOT == 0

    cond = jnp.zeros((MOD_ROWS, D_MODEL), F32).at[:bsz].set(c).at[bsz].set(c_ctx)
    mods = _ada_call(cond, w_ada, b_ada).reshape(depth, MOD_ROWS, N_MOD, D_MODEL)
    cos_t, sin_t = _rope_tables(n_lat_tok)
    w_r = jnp.zeros((D_MODEL, LANES), F32).at[:, :N_EXPERTS].set(w_router)
    w_r_hi = w_r.astype(BF16)
    w_r = jnp.concatenate([w_r_hi, (w_r - w_r_hi.astype(F32)).astype(BF16)], axis=1)
    b_r = jnp.zeros((1, LANES), F32).at[0, :N_EXPERTS].set(b_router)

    xs = jnp.concatenate([x, ctx], axis=1)
    y = None
    for i in range(depth):
        lam_init = 0.8 - 0.6 * math.exp(-0.3 * i)
        xs, q, k, v, f = _proj_call(
            xs, y, mods[i - 1] if i else None, mods[i], g_norm1[i].reshape(1, D_MODEL),
            w_in[i].astype(BF16), cos_t, sin_t, bsz)
        o = _attn_call(q, k, v, lam_qk[i], g_subln[i], lam_init)
        z = _fourier_call(f, n_lat_tok)
        xs, h2, route = _merge_call(
            xs, o, z, mods[i], g_norm1[i].reshape(1, D_MODEL), g_norm2[i].reshape(1, D_MODEL),
            w_gate[i].astype(BF16), b_gate[i].reshape(1, 2 * D_MODEL), w_attn_out[i].astype(BF16),
            w_four_out[i].astype(BF16), w_out[i].astype(BF16), w_r, b_r, bsz)
        y = _moe_call(h2.reshape(bsz * n_tok, D_MODEL), route.reshape(bsz * n_tok, LANES),
                      w_e_gate[i], w_e_up[i], w_e_down[i])
        y = y.reshape(bsz, n_tok, D_MODEL)
    return _final_call(xs, y, mods[depth - 1], g_final.reshape(1, D_MODEL), n_lat_tok)
```

```python
import functools
import math

import numpy as np
import jax
import jax.numpy as jnp
from jax import lax
from jax.experimental import pallas as pl
from jax.experimental.pallas import tpu as pltpu

F32 = jnp.float32
BF16 = jnp.bfloat16
HIGHEST = lax.Precision.HIGHEST

D_MODEL = 1024
GRID_W = 64
CTX_LEN = 256
N_HEADS = 8
HEAD_DIM = 64
V_DIM = 128
QK_WIDTH = 1024
V_WIDTH = 1024
F_GROUPS = 4
F_WIDTH = 512
F_GROUP_DIM = 128
IN_WIDTH = 3584
ROPE_THETA = 10000.0
ROPE_HALF = 16
N_EXPERTS = 16
N_GROUPS = 4
EXPERTS_PER_GROUP = 4
D_EXPERT = 512
EPS = 1e-6
N_MOD = 6
MOD_ROWS = 8

LANES = 128
SUBLANES = 8
TM = 256
TQ = 1024
MXU_TILE = 256
BF16_ROWS = 16
SUM_ROWS = 16
MAX_KEY_TILES = 16
QK_ROWS = 2048
LAG_LIMIT = 100.0
SUM_FLOOR = 2.0 ** -60
TMR = 256
TSLOT = 512
DMA_UNROLL = 8
ADA_TN = 512
DFT2 = 128
DFT_GROUP = 8
Q_SCALE = (HEAD_DIM ** -0.5) * math.log2(math.e)
VMEM_LIMIT = 56 * 1024 * 1024


def _cparams(n_axes):
    return pltpu.CompilerParams(dimension_semantics=("arbitrary",) * n_axes,
                                vmem_limit_bytes=VMEM_LIMIT)


def _rms_mod(x, g, shift, scale):
    y = x * lax.rsqrt(jnp.mean(x * x, axis=-1, keepdims=True) + EPS) * g
    return y * (1.0 + scale) + shift


def _ada_kernel(c_ref, w_ref, b_ref, o_ref):
    c = c_ref[...]
    s = c * jax.nn.sigmoid(c)
    o_ref[0] = jnp.dot(s, w_ref[0], preferred_element_type=F32, precision=HIGHEST) + b_ref[0]


def _ada_call(cond, w_ada, b_ada):
    depth = w_ada.shape[0]
    n_out = w_ada.shape[2]
    return pl.pallas_call(
        _ada_kernel,
        grid=(depth, n_out // ADA_TN),
        in_specs=[
            pl.BlockSpec((MOD_ROWS, D_MODEL), lambda l, j: (0, 0)),
            pl.BlockSpec((1, D_MODEL, ADA_TN), lambda l, j: (l, 0, j)),
            pl.BlockSpec((1, 1, ADA_TN), lambda l, j: (l, 0, j)),
        ],
        out_specs=pl.BlockSpec((1, MOD_ROWS, ADA_TN), lambda l, j: (l, 0, j)),
        out_shape=jax.ShapeDtypeStruct((depth, MOD_ROWS, n_out), F32),
        compiler_params=_cparams(2),
        name="ada",
    )(cond, w_ada, b_ada.reshape(depth, 1, n_out))


def _proj_kernel(*refs, has_y):
    if has_y:
        (x_ref, y_ref, pmod_ref, mod_ref, g1_ref, win_ref, cos_ref, sin_ref,
         xo_ref, q_ref, k_ref, v_ref, f_ref) = refs
        x = x_ref[0] + pmod_ref[0][5:6] * y_ref[0]
        xo_ref[0] = x
    else:
        (x_ref, mod_ref, g1_ref, win_ref, cos_ref, sin_ref,
         q_ref, k_ref, v_ref, f_ref) = refs
        x = x_ref[0]
    mod = mod_ref[0]
    hb = _rms_mod(x, g1_ref[...], mod[0:1], mod[1:2]).astype(BF16)
    cos = cos_ref[...]
    sin = sin_ref[...]
    lane = lax.broadcasted_iota(jnp.int32, (TM, LANES), 1)
    first = (lane % (2 * ROPE_HALF)) < ROPE_HALF

    def rope(xc):
        up = pltpu.roll(xc, LANES - ROPE_HALF, 1)
        dn = pltpu.roll(xc, ROPE_HALF, 1)
        return xc * cos + jnp.where(first, up, dn) * sin

    w2 = 2 * LANES
    for c in range(QK_WIDTH // w2):
        uq = jnp.dot(hb, win_ref[:, c * w2:(c + 1) * w2], preferred_element_type=F32)
        uk = jnp.dot(hb, win_ref[:, QK_WIDTH + c * w2:QK_WIDTH + (c + 1) * w2],
                     preferred_element_type=F32)
        for j in range(2):
            lo = c * w2 + j * LANES
            q_ref[0, :, lo:lo + LANES] = (rope(uq[:, j * LANES:(j + 1) * LANES]) * Q_SCALE).astype(BF16)
            k_ref[0, :, lo:lo + LANES] = rope(uk[:, j * LANES:(j + 1) * LANES]).astype(BF16)
    for c in range(V_WIDTH // w2):
        lo = 2 * QK_WIDTH + c * w2
        v_ref[0, :, c * w2:(c + 1) * w2] = jnp.dot(
            hb, win_ref[:, lo:lo + w2], preferred_element_type=F32).astype(BF16)
    for c in range(F_WIDTH // w2):
        lo = 2 * QK_WIDTH + V_WIDTH + c * w2
        f_ref[0, :, c * w2:(c + 1) * w2] = jnp.dot(
            hb, win_ref[:, lo:lo + w2], preferred_element_type=F32)


def _proj_call(x, y, pmod, mod, g1, w_in, cos_t, sin_t, n_batch):
    bsz, n_tok, _ = x.shape
    nt = n_tok // TM
    has_y = y is not None

    def mod_map(b, t):
        return (jnp.where(t == nt - 1, n_batch, b), 0, 0)

    tok = lambda w: pl.BlockSpec((1, TM, w), lambda b, t: (b, t, 0))
    modspec = pl.BlockSpec((1, N_MOD, D_MODEL), mod_map)
    in_specs = [tok(D_MODEL)]
    args = [x]
    if has_y:
        in_specs += [tok(D_MODEL), modspec]
        args += [y, pmod]
    in_specs += [
        modspec,
        pl.BlockSpec((1, D_MODEL), lambda b, t: (0, 0)),
        pl.BlockSpec((D_MODEL, IN_WIDTH), lambda b, t: (0, 0)),
        pl.BlockSpec((TM, LANES), lambda b, t: (t, 0)),
        pl.BlockSpec((TM, LANES), lambda b, t: (t, 0)),
    ]
    args += [mod, g1, w_in, cos_t, sin_t]
    out_specs = [tok(QK_WIDTH), tok(QK_WIDTH), tok(V_WIDTH), tok(F_WIDTH)]
    out_shape = [
        jax.ShapeDtypeStruct((bsz, n_tok, QK_WIDTH), BF16),
        jax.ShapeDtypeStruct((bsz, n_tok, QK_WIDTH), BF16),
        jax.ShapeDtypeStruct((bsz, n_tok, V_WIDTH), BF16),
        jax.ShapeDtypeStruct((bsz, n_tok, F_WIDTH), F32),
    ]
    if has_y:
        out_specs = [tok(D_MODEL)] + out_specs
        out_shape = [jax.ShapeDtypeStruct((bsz, n_tok, D_MODEL), F32)] + out_shape
    outs = pl.pallas_call(
        functools.partial(_proj_kernel, has_y=has_y),
        grid=(bsz, nt),
        in_specs=in_specs,
        out_specs=out_specs,
        out_shape=out_shape,
        compiler_params=_cparams(2),
        name="proj",
    )(*args)
    if has_y:
        return outs
    return [x] + list(outs)


def _split_maps_t(qt):
    row = lax.broadcasted_iota(jnp.int32, qt.shape, 0)
    zero = jnp.zeros_like(qt)
    return jnp.concatenate([jnp.where(row < HEAD_DIM, qt, zero),
                            jnp.where(row >= HEAD_DIM, qt, zero)], axis=1)


def _ones_rows(n_cols):
    row = lax.broadcasted_iota(jnp.int32, (SUM_ROWS, n_cols), 0)
    return jnp.where(row == 0, 1.0, 0.0).astype(BF16)


def _vt_block(vt, n_keys):
    return jnp.concatenate([vt, _ones_rows(n_keys)], axis=0)


def _softmax_block_t(q2t, kb, vbt_ext, m_prev, acc_prev):
    st = jnp.dot(kb, q2t, preferred_element_type=F32)
    m_new = jnp.maximum(m_prev, jnp.max(st, axis=0, keepdims=True))
    alpha = jnp.exp2(m_prev - m_new)
    pt = jnp.exp2(st - m_new).astype(BF16)
    acc = alpha * acc_prev + jnp.dot(vbt_ext, pt, preferred_element_type=F32)
    return m_new, acc


def _diff_combine_t(acc, lam_ref, gs_ref, lam_init):
    tq = acc.shape[1] // 2
    lp = lam_ref[...]
    lam = (jnp.exp(jnp.sum(lp[0:1] * lp[1:2], axis=1, keepdims=True))
           - jnp.exp(jnp.sum(lp[2:3] * lp[3:4], axis=1, keepdims=True)) + lam_init)
    pv = acc[:V_DIM]
    l = acc[V_DIM:V_DIM + 1]
    ot = pv[:, :tq] / l[:, :tq] - lam * (pv[:, tq:] / l[:, tq:])
    ot = ot * lax.rsqrt(jnp.mean(ot * ot, axis=0, keepdims=True) + EPS) * gs_ref[...]
    return (ot * (1.0 - lam_init)).T.astype(BF16)


def _attn_kernel(lam_ref, gs_ref, q_ref, k_ref, v_ref, o_ref, vt_ref, q2t_ref, m_ref, acc_ref,
                 lag_ref, pt_ref, *, n_lat_tok, tk, lam_init):
    n_tok = n_lat_tok + CTX_LEN

    @pl.when(pl.program_id(2) == 0)
    def _():
        for j in range(n_tok // MXU_TILE):
            sl = slice(j * MXU_TILE, (j + 1) * MXU_TILE)
            vt_ref[:, sl] = v_ref[0, sl, :].astype(F32).T.astype(BF16)

    q2t_ref[...] = _split_maps_t(q_ref[0].astype(F32).T.astype(BF16))

    def reset(m0):
        m_ref[...] = jnp.full(m_ref.shape, m0, F32)
        acc_ref[...] = jnp.zeros(acc_ref.shape, F32)

    def exact_chunk(off):
        m_new, acc = _softmax_block_t(
            q2t_ref[...], k_ref[0, pl.ds(off, MXU_TILE), :],
            _vt_block(vt_ref[:, pl.ds(off, MXU_TILE)], MXU_TILE), m_ref[...], acc_ref[...])
        m_ref[...] = m_new
        acc_ref[...] = acc

    def lagged_block(off, n_keys):
        m_prev = m_ref[...]
        pmax = None
        for r in range(0, n_keys, QK_ROWS):
            rows = min(QK_ROWS, n_keys - r)
            st = jnp.dot(k_ref[0, pl.ds(off + r, rows), :], q2t_ref[...], preferred_element_type=F32)
            p = jnp.exp2(st - m_prev).astype(BF16)
            pt_ref[r:r + rows, :] = p
            for t in range(0, rows, BF16_ROWS):
                tile = p[t:t + BF16_ROWS, :]
                pmax = tile if pmax is None else jnp.maximum(pmax, tile)
        acc = acc_ref[...] + jnp.dot(_vt_block(vt_ref[:, pl.ds(off, n_keys)], n_keys), pt_ref[0:n_keys, :],
                                     preferred_element_type=F32)
        excess = jnp.log2(jnp.max(pmax.astype(F32), axis=0, keepdims=True))
        m_new = m_prev + jnp.maximum(excess, 0.0)
        acc_ref[...] = acc * jnp.exp2(m_prev - m_new)
        m_ref[...] = m_new
        lag_ref[...] = jnp.maximum(lag_ref[...], excess)

    reset(0.0)
    lag_ref[...] = jnp.full(lag_ref.shape, -jnp.inf, F32)
    n_full = n_tok // tk - 1
    lax.fori_loop(0, n_full, lambda j, carry: (lagged_block(pl.multiple_of(j * tk, tk), tk), carry)[1], 0)
    lagged_block(n_full * tk, n_tok - n_full * tk)

    sum_min = jnp.min(acc_ref[V_DIM:V_DIM + 1, :])

    @pl.when(jnp.logical_or(jnp.max(lag_ref[...]) > LAG_LIMIT, sum_min < SUM_FLOOR))
    def _():
        reset(-jnp.inf)
        lax.fori_loop(0, n_tok // MXU_TILE,
                      lambda c, carry: (exact_chunk(pl.multiple_of(c * MXU_TILE, MXU_TILE)), carry)[1], 0)

    o_ref[0] = _diff_combine_t(acc_ref[...], lam_ref, gs_ref, lam_init)


def _attn_ctx_kernel(lam_ref, gs_ref, q_ref, k_ref, v_ref, o_any, o_ref, *, lam_init):
    del o_any
    q2t = _split_maps_t(q_ref[0].astype(F32).T.astype(BF16))
    vbt = _vt_block(v_ref[0].astype(F32).T.astype(BF16), CTX_LEN)
    m0 = jnp.full((1, 2 * CTX_LEN), -jnp.inf, F32)
    acc0 = jnp.zeros((V_DIM + SUM_ROWS, 2 * CTX_LEN), F32)
    _, acc = _softmax_block_t(q2t, k_ref[0], vbt, m0, acc0)
    o_ref[0] = _diff_combine_t(acc, lam_ref, gs_ref, lam_init)


def _key_block(n_tok):
    for mult in range(MAX_KEY_TILES, 0, -1):
        if n_tok % (mult * MXU_TILE) == 0:
            return mult * MXU_TILE
    raise ValueError(n_tok)


def _attn_call(q, k, v, lam_qk, g_subln, lam_init):
    bsz, n_tok, _ = q.shape
    n_lat_tok = n_tok - CTX_LEN
    tk = _key_block(n_lat_tok)
    tq_step = TQ
    n_col = 2 * TQ
    small = [pl.BlockSpec((4, HEAD_DIM), lambda *_: (0, 0)),
             pl.BlockSpec((V_DIM, 1), lambda *_: (0, 0))]
    gs = g_subln.reshape(V_DIM, 1)
    o = pl.pallas_call(
        functools.partial(_attn_kernel, n_lat_tok=n_lat_tok, tk=tk, lam_init=lam_init),
        grid=(bsz, N_HEADS, n_lat_tok // tq_step),
        in_specs=small + [
            pl.BlockSpec((1, tq_step, LANES), lambda b, h, t: (b, t, h)),
            pl.BlockSpec((1, n_tok, LANES), lambda b, h, t: (b, 0, h)),
            pl.BlockSpec((1, n_tok, LANES), lambda b, h, t: (b, 0, h)),
        ],
        out_specs=pl.BlockSpec((1, tq_step, V_DIM), lambda b, h, t: (b, t, h)),
        out_shape=jax.ShapeDtypeStruct((bsz, n_tok, V_WIDTH), BF16),
        scratch_shapes=[
            pltpu.VMEM((V_DIM, n_tok), BF16),
            pltpu.VMEM((LANES, n_col), BF16),
            pltpu.VMEM((1, n_col), F32),
            pltpu.VMEM((V_DIM + SUM_ROWS, n_col), F32),
            pltpu.VMEM((1, n_col), F32),
            pltpu.VMEM((n_tok - (n_tok // tk - 1) * tk, n_col), BF16),
        ],
        compiler_params=_cparams(3),
        name="diff_attn",
    )(lam_qk, gs, q, k, v)
    ctx_blk = n_lat_tok // CTX_LEN
    ctx_spec = pl.BlockSpec((1, CTX_LEN, LANES), lambda b, h: (b, ctx_blk, h))
    return pl.pallas_call(
        functools.partial(_attn_ctx_kernel, lam_init=lam_init),
        grid=(bsz, N_HEADS),
        in_specs=small + [ctx_spec, ctx_spec, ctx_spec, pl.BlockSpec(memory_space=pl.ANY)],
        out_specs=ctx_spec,
        out_shape=jax.ShapeDtypeStruct((bsz, n_tok, V_WIDTH), BF16),
        input_output_aliases={5: 0},
        compiler_params=_cparams(2),
        name="diff_attn_ctx",
    )(lam_qk, gs, q, k, v, o)


def _dft1_kernel(u_ref, c1_ref, s1_ref, twc_ref, tws_ref, o_ref):
    n1 = u_ref.shape[1]
    rows = n1 * DFT_GROUP
    u = u_ref[0].reshape(rows, F_WIDTH).astype(BF16)
    ar = jnp.dot(c1_ref[...], u, preferred_element_type=F32)
    ai = -jnp.dot(s1_ref[...], u, preferred_element_type=F32)
    twc = twc_ref[0]
    tws = tws_ref[0]
    out = []
    for part in (lambda a_r, a_i: a_r * twc + a_i * tws, lambda a_r, a_i: a_i * twc - a_r * tws):
        for g in range(F_GROUPS):
            sl = slice(g * LANES, (g + 1) * LANES)
            out.append(part(ar[:, sl], ai[:, sl]))
    o_ref[0] = jnp.concatenate(out, axis=1).reshape(n1, DFT_GROUP, 2 * F_WIDTH)


def _channel_dft(gr, gi, cc_ref, sc_ref, norm):
    grb = gr.astype(BF16)
    gib = gi.astype(BF16)
    out = []
    for g in range(F_GROUPS):
        sl = slice(g * LANES, (g + 1) * LANES)
        out.append((jnp.dot(grb[:, sl], cc_ref[...], preferred_element_type=F32)
                    + jnp.dot(gib[:, sl], sc_ref[...], preferred_element_type=F32)) * norm)
    return out


def _dft2_kernel(b_ref, c2_ref, s2_ref, cc_ref, sc_ref, o_ref, *, norm):
    for j in range(DFT_GROUP):
        bb = b_ref[0, j].astype(BF16)
        cb = jnp.dot(c2_ref[...], bb, preferred_element_type=F32)
        sb = jnp.dot(s2_ref[...], bb, preferred_element_type=F32)
        gr = cb[:, :F_WIDTH] + sb[:, F_WIDTH:]
        gi = cb[:, F_WIDTH:] - sb[:, :F_WIDTH]
        for g, z in enumerate(_channel_dft(gr, gi, cc_ref, sc_ref, norm)):
            o_ref[0, :, j, g * LANES:(g + 1) * LANES] = z


def _dft_ctx_kernel(f_ref, c_ref, s_ref, cc_ref, sc_ref, z_any, o_ref, *, norm):
    del z_any
    f = f_ref[0].astype(BF16)
    gr = jnp.dot(c_ref[...], f, preferred_element_type=F32)
    gi = -jnp.dot(s_ref[...], f, preferred_element_type=F32)
    for g, z in enumerate(_channel_dft(gr, gi, cc_ref, sc_ref, norm)):
        o_ref[0, :, g * LANES:(g + 1) * LANES] = z


def _dft_mats(n, repeat=1):
    idx = np.arange(n, dtype=np.float64)
    ang = 2.0 * np.pi * np.outer(idx, idx) / n
    eye = np.eye(repeat)
    return (jnp.asarray(np.kron(np.cos(ang), eye), BF16), jnp.asarray(np.kron(np.sin(ang), eye), BF16))


def _fourier_call(f, n_lat_tok):
    bsz, n_tok, _ = f.shape
    n1 = n_lat_tok // DFT2
    full = lambda *shape: pl.BlockSpec(shape, lambda *_: (0,) * len(shape))
    c1, s1 = _dft_mats(n1, DFT_GROUP)
    c2, s2 = _dft_mats(DFT2)
    cc, sc = _dft_mats(F_GROUP_DIM)
    cctx, sctx = _dft_mats(CTX_LEN)
    tw_ang = (2.0 * np.pi / n_lat_tok) * np.einsum(
        "sj,k->skj", np.arange(DFT2, dtype=np.float64).reshape(DFT2 // DFT_GROUP, DFT_GROUP),
        np.arange(n1, dtype=np.float64)).reshape(DFT2 // DFT_GROUP, n1 * DFT_GROUP)
    twc = jnp.asarray(np.broadcast_to(np.cos(tw_ang)[:, :, None], tw_ang.shape + (LANES,)), F32)
    tws = jnp.asarray(np.broadcast_to(np.sin(tw_ang)[:, :, None], tw_ang.shape + (LANES,)), F32)

    fv = f.reshape(bsz, n_tok // DFT2, DFT2, F_WIDTH)
    mid = pl.pallas_call(
        _dft1_kernel,
        grid=(bsz, DFT2 // DFT_GROUP),
        in_specs=[
            pl.BlockSpec((1, n1, DFT_GROUP, F_WIDTH), lambda b, j: (b, 0, j, 0)),
            full(n1 * DFT_GROUP, n1 * DFT_GROUP), full(n1 * DFT_GROUP, n1 * DFT_GROUP),
            pl.BlockSpec((1, n1 * DFT_GROUP, LANES), lambda b, j: (j, 0, 0)),
            pl.BlockSpec((1, n1 * DFT_GROUP, LANES), lambda b, j: (j, 0, 0)),
        ],
        out_specs=pl.BlockSpec((1, n1, DFT_GROUP, 2 * F_WIDTH), lambda b, j: (b, 0, j, 0)),
        out_shape=jax.ShapeDtypeStruct((bsz, n1, DFT2, 2 * F_WIDTH), F32),
        compiler_params=_cparams(2),
        name="dft_outer",
    )(fv, c1, s1, twc, tws)

    z = pl.pallas_call(
        functools.partial(_dft2_kernel, norm=1.0 / math.sqrt(n_lat_tok * F_GROUP_DIM)),
        grid=(bsz, n1 // DFT_GROUP),
        in_specs=[
            pl.BlockSpec((1, DFT_GROUP, DFT2, 2 * F_WIDTH), lambda b, j: (b, j, 0, 0)),
            full(DFT2, DFT2), full(DFT2, DFT2),
            full(F_GROUP_DIM, F_GROUP_DIM), full(F_GROUP_DIM, F_GROUP_DIM),
        ],
        out_specs=pl.BlockSpec((1, DFT2, DFT_GROUP, F_WIDTH), lambda b, j: (b, 0, j, 0)),
        out_shape=jax.ShapeDtypeStruct((bsz, n_tok // n1, n1, F_WIDTH), F32),
        compiler_params=_cparams(2),
        name="dft_inner",
    )(mid, c2, s2, cc, sc)
    z = z.reshape(bsz, n_tok, F_WIDTH)

    ctx_blk = n_lat_tok // CTX_LEN
    z = pl.pallas_call(
        functools.partial(_dft_ctx_kernel, norm=1.0 / math.sqrt(CTX_LEN * F_GROUP_DIM)),
        grid=(bsz,),
        in_specs=[
            pl.BlockSpec((1, CTX_LEN, F_WIDTH), lambda b: (b, ctx_blk, 0)),
            full(CTX_LEN, CTX_LEN), full(CTX_LEN, CTX_LEN),
            full(F_GROUP_DIM, F_GROUP_DIM), full(F_GROUP_DIM, F_GROUP_DIM),
            pl.BlockSpec(memory_space=pl.ANY),
        ],
        out_specs=pl.BlockSpec((1, CTX_LEN, F_WIDTH), lambda b: (b, ctx_blk, 0)),
        out_shape=jax.ShapeDtypeStruct((bsz, n_tok, F_WIDTH), F32),
        input_output_aliases={5: 0},
        compiler_params=_cparams(1),
        name="dft_ctx",
    )(f, cctx, sctx, cc, sc, z)
    return z


def _merge_kernel(x_ref, o_ref, z_ref, mod_ref, g1_ref, g2_ref, wgate_ref, bgate_ref,
                  wao_ref, wfo_ref, wout_ref, wr_ref, br_ref, xo_ref, h2_ref, route_ref):
    x = x_ref[0]
    mod = mod_ref[0]
    hb = _rms_mod(x, g1_ref[...], mod[0:1], mod[1:2]).astype(BF16)
    gates = jax.nn.sigmoid(jnp.dot(hb, wgate_ref[...], preferred_element_type=F32) + bgate_ref[...])
    ya = jnp.dot(o_ref[0], wao_ref[...], preferred_element_type=F32)
    yf = jnp.dot(z_ref[0].astype(BF16), wfo_ref[...], preferred_element_type=F32)
    merged = gates[:, :D_MODEL] * ya + gates[:, D_MODEL:] * yf
    out = jnp.dot(merged.astype(BF16), wout_ref[...], preferred_element_type=F32)
    xn = x + mod[2:3] * out
    xo_ref[0] = xn
    h2 = _rms_mod(xn, g2_ref[...], mod[3:4], mod[4:5])
    h2_ref[0] = h2

    h2_hi = h2.astype(BF16)
    h2_lo = (h2 - h2_hi.astype(F32)).astype(BF16)
    part = jnp.dot(h2_hi, wr_ref[...], preferred_element_type=F32)
    logits = (part[:, :LANES] + part[:, LANES:]
              + jnp.dot(h2_lo, wr_ref[:, :LANES], preferred_element_type=F32))
    aff = jax.nn.sigmoid(logits)
    lane = lax.broadcasted_iota(jnp.int32, (TM, LANES), 1)
    lanef = lane.astype(F32)
    neg = jnp.full((TM, LANES), -jnp.inf, F32)
    sel = jnp.where(lane < N_EXPERTS, aff + br_ref[...], neg)
    grp = lane // EXPERTS_PER_GROUP
    best = i1b = i2b = None
    for g in range(N_GROUPS):
        sg = jnp.where(grp == g, sel, neg)
        m1 = jnp.max(sg, axis=1, keepdims=True)
        i1 = jnp.min(jnp.where(sg == m1, lanef, float(LANES)), axis=1, keepdims=True)
        sg2 = jnp.where(lanef == i1, neg, sg)
        m2 = jnp.max(sg2, axis=1, keepdims=True)
        i2 = jnp.min(jnp.where(sg2 == m2, lanef, float(LANES)), axis=1, keepdims=True)
        score = m1 + m2
        if g == 0:
            best, i1b, i2b = score, i1, i2
        else:
            upd = score > best
            best = jnp.where(upd, score, best)
            i1b = jnp.where(upd, i1, i1b)
            i2b = jnp.where(upd, i2, i2b)
    hit1 = lanef == i1b
    hit2 = lanef == i2b
    a1 = jnp.sum(jnp.where(hit1, aff, 0.0), axis=1, keepdims=True)
    a2 = jnp.sum(jnp.where(hit2, aff, 0.0), axis=1, keepdims=True)
    den = a1 + a2
    route_ref[0] = (jnp.where(lane == 0, i1b, 0.0) + jnp.where(lane == 1, i2b, 0.0)
                    + jnp.where(lane == 2, a1 / den, 0.0) + jnp.where(lane == 3, a2 / den, 0.0))


def _merge_call(x, o, z, mod, g1, g2, w_gate, b_gate, w_ao, w_fo, w_out, w_r, b_r, n_batch):
    bsz, n_tok, _ = x.shape
    nt = n_tok // TM
    tok = lambda w: pl.BlockSpec((1, TM, w), lambda b, t: (b, t, 0))
    full = lambda *shape: pl.BlockSpec(shape, lambda b, t: (0,) * len(shape))
    return pl.pallas_call(
        _merge_kernel,
        grid=(bsz, nt),
        in_specs=[
            tok(D_MODEL), tok(V_WIDTH), tok(F_WIDTH),
            pl.BlockSpec((1, N_MOD, D_MODEL),
                         lambda b, t: (jnp.where(t == nt - 1, n_batch, b), 0, 0)),
            full(1, D_MODEL), full(1, D_MODEL),
            full(D_MODEL, 2 * D_MODEL), full(1, 2 * D_MODEL),
            full(V_WIDTH, D_MODEL), full(F_WIDTH, D_MODEL), full(D_MODEL, D_MODEL),
            full(D_MODEL, 2 * LANES), full(1, LANES),
        ],
        out_specs=[tok(D_MODEL), tok(D_MODEL), tok(LANES)],
        out_shape=[
            jax.ShapeDtypeStruct((bsz, n_tok, D_MODEL), F32),
            jax.ShapeDtypeStruct((bsz, n_tok, D_MODEL), F32),
            jax.ShapeDtypeStruct((bsz, n_tok, LANES), F32),
        ],
        compiler_params=_cparams(2),
        name="merge_route",
    )(x, o, z, mod, g1, g2, w_gate, b_gate, w_ao, w_fo, w_out, w_r, b_r)


def _slot_copy(src_ref, src_row, dst_ref, dst_row, sem):
    return pltpu.make_async_copy(src_ref.at[pl.ds(src_row, 1)], dst_ref.at[pl.ds(dst_row, 1)], sem)


def _dispatch_kernel(pad_start_ref, pad_len_ref, pos_ref, h_ref, xs_ref, zero_ref, sem, pad_sem):
    @pl.when(pl.program_id(0) == 0)
    def _():
        zero_ref[...] = jnp.zeros(zero_ref.shape, F32)

        def pad_copy(e, r):
            return _slot_copy(zero_ref, 0, xs_ref, pad_start_ref[e] + r, pad_sem)

        for e in range(N_EXPERTS):
            lax.fori_loop(0, pad_len_ref[e], lambda r, carry, e=e: (pad_copy(e, r).start(), carry)[1], 0)
        for e in range(N_EXPERTS):
            lax.fori_loop(0, pad_len_ref[e], lambda r, carry, e=e: (pad_copy(e, r).wait(), carry)[1], 0)

    def copy(r, k):
        return _slot_copy(h_ref, r, xs_ref, pos_ref[0, 0, 2 * r + k], sem)

    def issue(r, carry):
        copy(r, 0).start(priority=0)
        copy(r, 1).start(priority=1)
        return carry

    def drain(r, carry):
        copy(r, 0).wait()
        copy(r, 1).wait()
        return carry

    lax.fori_loop(0, TSLOT, issue, 0, unroll=DMA_UNROLL)
    lax.fori_loop(0, TSLOT, drain, 0, unroll=DMA_UNROLL)


def _experts_kernel(te_ref, nv_ref, xs_ref, wg_ref, wu_ref, wd_ref, ys_ref):
    del te_ref
    t = pl.program_id(0)

    @pl.when(t < nv_ref[0])
    def _():
        h = xs_ref[...].astype(BF16)
        a = jnp.dot(h, wg_ref[0].astype(BF16), preferred_element_type=F32)
        u = jnp.dot(h, wu_ref[0].astype(BF16), preferred_element_type=F32)
        he = (a * jax.nn.sigmoid(a) * u).astype(BF16)
        ys_ref[...] = jnp.dot(he, wd_ref[0].astype(BF16), preferred_element_type=F32)

    @pl.when(t >= nv_ref[0])
    def _():
        ys_ref[...] = jnp.zeros(ys_ref.shape, F32)


def _combine_kernel(pos_ref, route_ref, ys_ref, y_ref, buf_ref, sem):
    def copy(r, k):
        return _slot_copy(ys_ref, pos_ref[0, 0, 2 * r + k], buf_ref.at[k], r, sem)

    def issue(r, carry):
        copy(r, 0).start(priority=0)
        copy(r, 1).start(priority=1)
        return carry

    def drain(r, carry):
        copy(r, 0).wait()
        copy(r, 1).wait()
        return carry

    lax.fori_loop(0, TSLOT, issue, 0, unroll=DMA_UNROLL)
    lax.fori_loop(0, TSLOT, drain, 0, unroll=DMA_UNROLL)
    route = route_ref[...]
    y_ref[...] = route[:, 2:3] * buf_ref[0] + route[:, 3:4] * buf_ref[1]


def _moe_call(h2, route, w_eg, w_eu, w_ed):
    n = h2.shape[0]
    n_slots = 2 * n
    n_rows = n_slots + N_EXPERTS * TMR
    n_tiles = n_rows // TMR

    e_slot = route[:, :2].astype(jnp.int32).reshape(n_slots)
    onehot = (e_slot[:, None] == jnp.arange(N_EXPERTS, dtype=jnp.int32)[None, :]).astype(jnp.int32)
    csum = jnp.cumsum(onehot, axis=0)
    rank = jnp.sum(csum * onehot, axis=1) - 1
    counts = csum[-1]
    padded = ((counts + TMR - 1) // TMR) * TMR
    ends = jnp.cumsum(padded)
    pos = jnp.sum(onehot * (ends - padded)[None, :], axis=1) + rank
    tile_expert = jnp.minimum(
        jnp.sum((jnp.arange(n_tiles, dtype=jnp.int32) * TMR)[:, None] >= ends[None, :], axis=1),
        N_EXPERTS - 1).astype(jnp.int32)
    n_valid = (ends[-1] // TMR).astype(jnp.int32).reshape(1)
    pos = jnp.clip(pos, 0, n_rows - 1)
    pos_tiles = pos.astype(jnp.int32).reshape(n // TSLOT, 1, 2 * TSLOT)

    pos_block = (1, 1, 2 * TSLOT)
    xs = pl.pallas_call(
        _dispatch_kernel,
        grid_spec=pltpu.PrefetchScalarGridSpec(
            num_scalar_prefetch=2,
            grid=(n // TSLOT,),
            in_specs=[pl.BlockSpec(pos_block, lambda t, ps, pn: (t, 0, 0), memory_space=pltpu.SMEM),
                      pl.BlockSpec((TSLOT, D_MODEL), lambda t, ps, pn: (t, 0))],
            out_specs=pl.BlockSpec(memory_space=pl.ANY),
            scratch_shapes=[pltpu.VMEM((SUBLANES, D_MODEL), F32),
                            pltpu.SemaphoreType.DMA, pltpu.SemaphoreType.DMA],
        ),
        out_shape=jax.ShapeDtypeStruct((n_rows, D_MODEL), F32),
        compiler_params=_cparams(1),
        name="dispatch",
    )((ends - padded + counts).astype(jnp.int32), (padded - counts).astype(jnp.int32), pos_tiles, h2)

    w_map = lambda t, te, nv: (te[t], 0, 0)
    ys = pl.pallas_call(
        _experts_kernel,
        grid_spec=pltpu.PrefetchScalarGridSpec(
            num_scalar_prefetch=2,
            grid=(n_tiles,),
            in_specs=[
                pl.BlockSpec((TMR, D_MODEL), lambda t, te, nv: (t, 0)),
                pl.BlockSpec((1, D_MODEL, D_EXPERT), w_map),
                pl.BlockSpec((1, D_MODEL, D_EXPERT), w_map),
                pl.BlockSpec((1, D_EXPERT, D_MODEL), w_map),
            ],
            out_specs=pl.BlockSpec((TMR, D_MODEL), lambda t, te, nv: (t, 0)),
        ),
        out_shape=jax.ShapeDtypeStruct((n_rows, D_MODEL), F32),
        compiler_params=_cparams(1),
        name="experts",
    )(tile_expert, n_valid, xs, w_eg, w_eu, w_ed)

    return pl.pallas_call(
        _combine_kernel,
        grid=(n // TSLOT,),
        in_specs=[pl.BlockSpec(pos_block, lambda t: (t, 0, 0), memory_space=pltpu.SMEM),
                  pl.BlockSpec((TSLOT, LANES), lambda t: (t, 0)),
                  pl.BlockSpec(memory_space=pl.ANY)],
        out_specs=pl.BlockSpec((TSLOT, D_MODEL), lambda t: (t, 0)),
        out_shape=jax.ShapeDtypeStruct((n, D_MODEL), F32),
        scratch_shapes=[pltpu.VMEM((2, TSLOT, D_MODEL), F32), pltpu.SemaphoreType.DMA],
        compiler_params=_cparams(1),
        name="combine",
    )(pos_tiles, route, ys)


def _final_kernel(x_ref, y_ref, mod_ref, g_ref, o_ref):
    x = x_ref[0] + mod_ref[0][5:6] * y_ref[0]
    o_ref[0] = x * lax.rsqrt(jnp.mean(x * x, axis=-1, keepdims=True) + EPS) * g_ref[...]


def _final_call(x, y, mod, g_final, n_lat_tok):
    bsz = x.shape[0]
    tok = pl.BlockSpec((1, TM, D_MODEL), lambda b, t: (b, t, 0))
    return pl.pallas_call(
        _final_kernel,
        grid=(bsz, n_lat_tok // TM),
        in_specs=[tok, tok,
                  pl.BlockSpec((1, N_MOD, D_MODEL), lambda b, t: (b, 0, 0)),
                  pl.BlockSpec((1, D_MODEL), lambda b, t: (0, 0))],
        out_specs=tok,
        out_shape=jax.ShapeDtypeStruct((bsz, n_lat_tok, D_MODEL), F32),
        compiler_params=_cparams(2),
        name="final_norm",
    )(x, y, mod, g_final)


def _rope_tables(n_lat_tok):
    t = np.arange(n_lat_tok)
    pos = np.stack([t // GRID_W, t % GRID_W], axis=0).astype(np.float64)
    inv = ROPE_THETA ** (-np.arange(ROPE_HALF, dtype=np.float64) / ROPE_HALF)
    d = np.arange(LANES) % HEAD_DIM
    axis = d // (2 * ROPE_HALF)
    freq = d % ROPE_HALF
    ang = pos[axis, :].T * inv[freq][None, :]
    sign = np.where((d % (2 * ROPE_HALF)) < ROPE_HALF, -1.0, 1.0)
    cos = np.concatenate([np.cos(ang), np.ones((CTX_LEN, LANES))], axis=0)
    sin = np.concatenate([np.sin(ang) * sign[None, :], np.zeros((CTX_LEN, LANES))], axis=0)
    return jnp.asarray(cos, F32), jnp.asarray(sin, F32)


def kernel(x, c, ctx, c_ctx, w_ada, b_ada, g_norm1, w_in, lam_qk, g_subln, w_attn_out, w_four_out,
           w_gate, b_gate, w_out, g_norm2, w_router, b_router, w_e_gate, w_e_up, w_e_down, g_final):
    bsz, n_lat_tok, d = x.shape
    depth = w_ada.shape[0]
    assert d == D_MODEL and ctx.shape[1] == CTX_LEN and bsz < MOD_ROWS
    assert n_lat_tok % TQ == 0 and n_lat_tok % (DFT2 * DFT_GROUP) == 0
    n_tok = n_lat_tok + CTX_LEN
    assert (bsz * n_tok) % TSLOT == 0

    cond = jnp.zeros((MOD_ROWS, D_MODEL), F32).at[:bsz].set(c).at[bsz].set(c_ctx)
    mods = _ada_call(cond, w_ada, b_ada).reshape(depth, MOD_ROWS, N_MOD, D_MODEL)
    cos_t, sin_t = _rope_tables(n_lat_tok)
    w_r = jnp.zeros((D_MODEL, LANES), F32).at[:, :N_EXPERTS].set(w_router)
    w_r_hi = w_r.astype(BF16)
    w_r = jnp.concatenate([w_r_hi, (w_r - w_r_hi.astype(F32)).astype(BF16)], axis=1)
    b_r = jnp.zeros((1, LANES), F32).at[0, :N_EXPERTS].set(b_router)

    xs = jnp.concatenate([x, ctx], axis=1)
    y = None
    for i in range(depth):
        lam_init = 0.8 - 0.6 * math.exp(-0.3 * i)
        xs, q, k, v, f = _proj_call(
            xs, y, mods[i - 1] if i else None, mods[i], g_norm1[i].reshape(1, D_MODEL),
            w_in[i].astype(BF16), cos_t, sin_t, bsz)
        o = _attn_call(q, k, v, lam_qk[i], g_subln[i], lam_init)
        z = _fourier_call(f, n_lat_tok)
        xs, h2, route = _merge_call(
            xs, o, z, mods[i], g_norm1[i].reshape(1, D_MODEL), g_norm2[i].reshape(1, D_MODEL),
            w_gate[i].astype(BF16), b_gate[i].reshape(1, 2 * D_MODEL), w_attn_out[i].astype(BF16),
            w_four_out[i].astype(BF16), w_out[i].astype(BF16), w_r, b_r, bsz)
        y = _moe_call(h2.reshape(bsz * n_tok, D_MODEL), route.reshape(bsz * n_tok, LANES),
                      w_e_gate[i], w_e_up[i], w_e_down[i])
        y = y.reshape(bsz, n_tok, D_MODEL)
    return _final_call(xs, y, mods[depth - 1], g_final.reshape(1, D_MODEL), n_lat_tok)
```
